```python
import jax, jax.numpy as jnp
from jax import lax
import numpy as np

D_MODEL = 1024
BATCH = 32
SEQ = 256
DEPTH = 4
DEC_BATCH = 2
DEC_SEQ = 2048
PAST_LEN = 512

GRID_W = 64
CHUNK = 128
EPS = 1e-6
GMLP_GROUPS = 4
GMLP_W = 256
CONV_W = 256
CONV_K = 31
RET_HEADS = 4
RET_DK = 64
RET_DV = 64
RET_W = RET_HEADS * RET_DV
MLA_HEADS = 4
MLA_Q_LORA = 192
MLA_KV_LORA = 128
MLA_NOPE = 64
MLA_ROPE = 32
MLA_V = 64
MLA_W = MLA_HEADS * MLA_V
MIX_W = GMLP_W + CONV_W + RET_W + MLA_W
ROPE_THETA = 10000.0
N_EXPERTS = 16
EXPERT_FF = 1024
EC_CAPACITY = 2
IN_SIZES = (GMLP_W, GMLP_W, CONV_W, CONV_W, RET_HEADS * RET_DK, RET_HEADS * RET_DK, RET_W, RET_W, RET_W, MLA_Q_LORA, MLA_KV_LORA, MLA_ROPE)
IN_W = sum(IN_SIZES)
F32 = jnp.float32

kernel_name = 'hybrid_prefix_diffusion_trunk'


def rms_norm(x, g):
    xf = x.astype(F32)
    y = xf * lax.rsqrt(jnp.mean(xf * xf, axis=-1, keepdims=True) + EPS)
    return y.astype(x.dtype) * g


def layer_norm(x, g, b):
    xf = x.astype(F32)
    mu = jnp.mean(xf, axis=-1, keepdims=True)
    var = jnp.mean(jnp.square(xf - mu), axis=-1, keepdims=True)
    return ((xf - mu) * lax.rsqrt(var + EPS)).astype(x.dtype) * g + b


def head_group_norm(o, g):
    of = o.astype(F32)
    mu = jnp.mean(of, axis=-1, keepdims=True)
    var = jnp.mean(jnp.square(of - mu), axis=-1, keepdims=True)
    y = ((of - mu) * lax.rsqrt(var + EPS)).astype(o.dtype)
    return y.reshape(o.shape[0], o.shape[1], -1) * g


def chunk_spatial_gating(u, v, g, w_s, b_s):
    bsz, n, _ = v.shape
    v = rms_norm(v, g).reshape(bsz, n // CHUNK, CHUNK, GMLP_GROUPS, GMLP_W // GMLP_GROUPS)
    mixed = jnp.einsum('hij,bcjhd->bcihd', w_s, v) + b_s.T[None, None, :, :, None]
    return u * mixed.reshape(bsz, n, GMLP_W)


def conformer_conv(a, gt, w_dw, b_dw, ln_g, ln_b, w_pw):
    y = a * jax.nn.sigmoid(gt)
    y = lax.conv_general_dilated(y, w_dw[:, None, :], window_strides=(1,), padding=[(CONV_K // 2, CONV_K // 2)], dimension_numbers=('NWC', 'WIO', 'NWC'), feature_group_count=CONV_W) + b_dw
    return jax.nn.silu(layer_norm(y, ln_g, ln_b)) @ w_pw


def retention_scan(q, k, v, log2_1m_gamma, s0):
    bsz, n, h, dk = q.shape
    nc = n // CHUNK
    dt = q.dtype
    lg = jnp.log1p(-jnp.exp2(log2_1m_gamma.astype(F32)))
    pos = jnp.arange(CHUNK, dtype=F32)
    diff = pos[:, None] - pos[None, :]
    intra = jnp.where(diff >= 0, jnp.exp(jnp.maximum(diff, 0.0) * lg[:, None, None]), 0.0).astype(dt)
    q_dec = jnp.exp((pos + 1.0) * lg[:, None]).astype(dt)
    k_dec = jnp.exp((CHUNK - 1.0 - pos) * lg[:, None]).astype(dt)
    c_dec = jnp.exp(CHUNK * lg).astype(dt)[None, :, None, None]
    qc = q.reshape(bsz, nc, CHUNK, h, dk)
    kc = k.reshape(bsz, nc, CHUNK, h, dk)
    vc = v.reshape(bsz, nc, CHUNK, h, -1)
    scores = jnp.einsum('bcihd,bcjhd->bchij', qc, kc) * intra
    inner = jnp.einsum('bchij,bcjhe->bcihe', scores, vc)
    kv = jnp.einsum('bcjhd,hj,bcjhe->cbhde', kc, k_dec, vc)

    def step(s, kv_c):
        return c_dec * s + kv_c, s

    s_fin, s_prev = lax.scan(step, s0.astype(dt), kv)
    cross = jnp.einsum('bcihd,cbhde,hi->bcihe', qc, s_prev, q_dec)
    return (inner + cross).reshape(bsz, n, h, -1), s_fin


def bidirectional_retention(q, k, v, gate_f, gate_b, p_f, p_b, gn_g, s0_f, s0_b):
    o_f, s_f = retention_scan(q, k, v, p_f, s0_f)
    o_b, s_b = retention_scan(q[:, ::-1], k[:, ::-1], v[:, ::-1], p_b, s0_b)
    y = jax.nn.silu(gate_f) * head_group_norm(o_f, gn_g) + jax.nn.silu(gate_b) * head_group_norm(o_b[:, ::-1], gn_g)
    return y, s_f, s_b


def axial_rope_tables(n):
    rows = n // GRID_W
    row = jnp.repeat(jnp.arange(rows, dtype=F32), GRID_W)
    col = jnp.tile(jnp.arange(GRID_W, dtype=F32), rows)
    inv = ROPE_THETA ** (-jnp.arange(0, MLA_ROPE // 2, 2, dtype=F32) / (MLA_ROPE // 2))
    ra = row[:, None] * inv
    ca = col[:, None] * inv
    ang = jnp.concatenate([ra, ra, ca, ca], axis=-1)
    return jnp.cos(ang), jnp.sin(ang)


def apply_axial_rope(x, cos, sin):
    x1, x2, x3, x4 = jnp.split(x, 4, axis=-1)
    rot = jnp.concatenate([-x2, x1, -x4, x3], axis=-1)
    return x * cos.astype(x.dtype) + rot * sin.astype(x.dtype)


def block_softmax_attention(q, k, v):
    bsz, n, h, dk = q.shape
    scale = dk ** -0.5
    qb = q.reshape(bsz, n // CHUNK, CHUNK, h, dk).transpose(1, 0, 2, 3, 4)

    def one_block(qblk):
        s = jnp.einsum('bqhd,bkhd->bhqk', qblk, k).astype(F32) * scale
        p = jax.nn.softmax(s, axis=-1).astype(v.dtype)
        return jnp.einsum('bhqk,bkhd->bqhd', p, v)

    o = lax.map(one_block, qb)
    return o.transpose(1, 0, 2, 3, 4).reshape(bsz, n, h * v.shape[-1])


def mla_expand(ckv, kpe, w_ukv):
    bsz, m, _ = ckv.shape
    kv = (ckv @ w_ukv).reshape(bsz, m, MLA_HEADS, MLA_NOPE + MLA_V)
    k = jnp.concatenate([kv[..., :MLA_NOPE], jnp.broadcast_to(kpe[:, :, None, :], (bsz, m, MLA_HEADS, MLA_ROPE))], axis=-1)
    return k, kv[..., MLA_NOPE:]


def latent_attention(cq, ckv_raw, kpe, q_norm_g, w_uq, kv_norm_g, w_ukv, rope, ctx_ckv, ctx_kpe):
    bsz, n, _ = cq.shape
    q = (rms_norm(cq, q_norm_g) @ w_uq).reshape(bsz, n, MLA_HEADS, MLA_NOPE + MLA_ROPE)
    ckv = rms_norm(ckv_raw, kv_norm_g)
    if rope is None:
        k_ckv, k_pe = ckv, kpe
    else:
        cos, sin = rope
        q = jnp.concatenate([q[..., :MLA_NOPE], apply_axial_rope(q[..., MLA_NOPE:], cos[:, None], sin[:, None])], axis=-1)
        k_ckv = jnp.concatenate([ctx_ckv, ckv], axis=1)
        k_pe = jnp.concatenate([ctx_kpe, apply_axial_rope(kpe, cos, sin)], axis=1)
    k, v = mla_expand(k_ckv, k_pe, w_ukv)
    return block_softmax_attention(q, k, v), ckv, kpe


def expert_choice_ffn(h, router, w_gate, w_up, w_down):
    bsz, n, d = h.shape
    cap = EC_CAPACITY * n // N_EXPERTS
    aff = jax.nn.softmax((h @ router).astype(F32), axis=-1)
    gate, idx = lax.top_k(jnp.swapaxes(aff, 1, 2), cap)

    def per_request(hb, gb, ib):
        xe = hb[ib]
        a = jnp.einsum('ecd,edf->ecf', xe, w_gate)
        b = jnp.einsum('ecd,edf->ecf', xe, w_up)
        ye = jnp.einsum('ecf,efd->ecd', jax.nn.silu(a) * b, w_down) * gb[..., None].astype(hb.dtype)
        return jnp.zeros_like(hb).at[ib.reshape(-1)].add(ye.reshape(-1, d))

    return jax.vmap(per_request)(h, gate, idx)


def trunk_layer(x, mod, lw, rope=None, ctx=None):
    bsz, n, _ = x.shape
    sh1, sc1, g1, sh2, sc2, g2 = jnp.split(mod[:, None, :], 6, axis=-1)
    h = rms_norm(x, lw['norm1_g']) * (1 + sc1) + sh1
    splits = np.cumsum(IN_SIZES)[:-1].tolist()
    u, vg, ca, cg, rq, rk, rv, rgf, rgb, cq, ckv_raw, kpe = jnp.split(h @ lw['w_in'], splits, axis=-1)
    y_a = chunk_spatial_gating(u, vg, lw['sg_norm_g'], lw['sg_w'], lw['sg_b'])
    y_b = conformer_conv(ca, cg, lw['conv_w'], lw['conv_b'], lw['conv_ln_g'], lw['conv_ln_b'], lw['conv_pw'])
    rq = rq.reshape(bsz, n, RET_HEADS, RET_DK)
    rk = rk.reshape(bsz, n, RET_HEADS, RET_DK) * (RET_DK ** -0.5)
    rv = rv.reshape(bsz, n, RET_HEADS, RET_DV)
    if ctx is None:
        s0 = jnp.zeros((bsz, RET_HEADS, RET_DK, RET_DV), x.dtype)
        s0_f, s0_b, ctx_ckv, ctx_kpe = s0, s0, None, None
    else:
        ctx_ckv, ctx_kpe, ret = ctx
        s0_f, s0_b = ret[:, 0], ret[:, 1]
    y_c, s_f, s_b = bidirectional_retention(rq, rk, rv, rgf, rgb, lw['ret_decay_f'], lw['ret_decay_b'], lw['ret_gn_g'], s0_f, s0_b)
    y_d, ckv, kpe = latent_attention(cq, ckv_raw, kpe, lw['q_norm_g'], lw['w_uq'], lw['kv_norm_g'], lw['w_ukv'], rope, ctx_ckv, ctx_kpe)
    x = x + g1 * (jnp.concatenate([y_a, y_b, y_c, y_d], axis=-1) @ lw['w_out'])
    h2 = rms_norm(x, lw['norm2_g']) * (1 + sc2) + sh2
    x = x + g2 * expert_choice_ffn(h2, lw['router'], lw['w_gate'], lw['w_up'], lw['w_down'])
    return x, ckv, kpe, jnp.stack([s_f, s_b], axis=1)


def setup_inputs(seed: int = 0) -> dict:
    key = jax.random.key(seed)
    keys = jax.random.split(key, 40)

    def nrm(i, shape, scale=1.0):
        return jax.random.normal(keys[i], shape, F32) * scale

    def gain(i, shape):
        return 1.0 + nrm(i, shape, 0.02)

    L = DEPTH
    head_idx = jnp.arange(RET_HEADS, dtype=F32)
    return {
        'x_prompt': nrm(0, (BATCH, SEQ, D_MODEL)),
        'x_sample': nrm(1, (DEC_BATCH, DEC_SEQ, D_MODEL)),
        'cache_mla_ckv': nrm(2, (DEC_BATCH, DEPTH, PAST_LEN, MLA_KV_LORA)),
        'cache_mla_kpe': nrm(3, (DEC_BATCH, DEPTH, PAST_LEN, MLA_ROPE)),
        'state_ret': nrm(4, (DEC_BATCH, DEPTH, 2, RET_HEADS, RET_DK, RET_DV)),
        'c': nrm(5, (DEC_BATCH, D_MODEL)),
        'c_ctx': nrm(6, (D_MODEL,)),
        'w_mod': nrm(7, (L, D_MODEL, 6 * D_MODEL), 0.5 * D_MODEL ** -0.5),
        'b_mod': nrm(8, (L, 6 * D_MODEL), 0.02),
        'norm1_g': gain(9, (L, D_MODEL)),
        'w_in': nrm(10, (L, D_MODEL, IN_W), D_MODEL ** -0.5),
        'sg_norm_g': gain(11, (L, GMLP_W)),
        'sg_w': nrm(12, (L, GMLP_GROUPS, CHUNK, CHUNK), CHUNK ** -0.5),
        'sg_b': nrm(13, (L, GMLP_GROUPS, CHUNK), 0.02),
        'conv_w': nrm(14, (L, CONV_K, CONV_W), CONV_K ** -0.5),
        'conv_b': nrm(15, (L, CONV_W), 0.02),
        'conv_ln_g': gain(16, (L, CONV_W)),
        'conv_ln_b': nrm(17, (L, CONV_W), 0.02),
        'conv_pw': nrm(18, (L, CONV_W, CONV_W), CONV_W ** -0.5),
        'ret_decay_f': -5.0 - head_idx + nrm(19, (L, RET_HEADS), 0.1),
        'ret_decay_b': -5.0 - head_idx + nrm(20, (L, RET_HEADS), 0.1),
        'ret_gn_g': gain(21, (L, RET_W)),
        'q_norm_g': gain(22, (L, MLA_Q_LORA)),
        'w_uq': nrm(23, (L, MLA_Q_LORA, MLA_HEADS * (MLA_NOPE + MLA_ROPE)), MLA_Q_LORA ** -0.5),
        'kv_norm_g': gain(24, (L, MLA_KV_LORA)),
        'w_ukv': nrm(25, (L, MLA_KV_LORA, MLA_HEADS * (MLA_NOPE + MLA_V)), MLA_KV_LORA ** -0.5),
        'w_out': nrm(26, (L, MIX_W, D_MODEL), MIX_W ** -0.5),
        'norm2_g': gain(27, (L, D_MODEL)),
        'router': nrm(28, (L, D_MODEL, N_EXPERTS), D_MODEL ** -0.5),
        'w_gate': nrm(29, (L, N_EXPERTS, D_MODEL, EXPERT_FF), D_MODEL ** -0.5),
        'w_up': nrm(30, (L, N_EXPERTS, D_MODEL, EXPERT_FF), D_MODEL ** -0.5),
        'w_down': nrm(31, (L, N_EXPERTS, EXPERT_FF, D_MODEL), EXPERT_FF ** -0.5),
        'final_norm_g': gain(32, (D_MODEL,)),
    }


def reference(x_prompt, x_sample, cache_mla_ckv, cache_mla_kpe, state_ret, c, c_ctx, w_mod, b_mod, norm1_g, w_in, sg_norm_g, sg_w, sg_b, conv_w, conv_b, conv_ln_g, conv_ln_b, conv_pw, ret_decay_f, ret_decay_b, ret_gn_g, q_norm_g, w_uq, kv_norm_g, w_ukv, w_out, norm2_g, router, w_gate, w_up, w_down, final_norm_g):
    rope = axial_rope_tables(x_sample.shape[1])
    xp, xs = x_prompt, x_sample
    ckvs, kpes, rets = [], [], []
    for l in range(DEPTH):
        lw = dict(norm1_g=norm1_g[l], w_in=w_in[l], sg_norm_g=sg_norm_g[l], sg_w=sg_w[l], sg_b=sg_b[l],
                  conv_w=conv_w[l], conv_b=conv_b[l], conv_ln_g=conv_ln_g[l], conv_ln_b=conv_ln_b[l], conv_pw=conv_pw[l],
                  ret_decay_f=ret_decay_f[l], ret_decay_b=ret_decay_b[l], ret_gn_g=ret_gn_g[l],
                  q_norm_g=q_norm_g[l], w_uq=w_uq[l], kv_norm_g=kv_norm_g[l], w_ukv=w_ukv[l],
                  w_out=w_out[l], norm2_g=norm2_g[l], router=router[l], w_gate=w_gate[l], w_up=w_up[l], w_down=w_down[l])
        mod_ctx = (jax.nn.silu(c_ctx) @ w_mod[l] + b_mod[l])[None, :]
        mod_lat = jax.nn.silu(c) @ w_mod[l] + b_mod[l]
        xp, ckv, kpe, st = trunk_layer(xp, mod_ctx, lw)
        ckvs.append(ckv)
        kpes.append(kpe)
        rets.append(st)
        xs = trunk_layer(xs, mod_lat, lw, rope, (cache_mla_ckv[:, l], cache_mla_kpe[:, l], state_ret[:, l]))[0]
    y_prompt = rms_norm(xp, final_norm_g)
    y_sample = rms_norm(xs, final_norm_g)
    return (y_prompt, y_sample, jnp.stack(ckvs, axis=1), jnp.stack(kpes, axis=1), jnp.stack(rets, axis=1))
```

```python
import functools

import jax
import jax.numpy as jnp
import numpy as np
from jax import lax
from jax.experimental import pallas as pl
from jax.experimental.pallas import tpu as pltpu

F32 = jnp.float32
BF16 = jnp.bfloat16

D = 1024
DEPTH = 4
CHUNK = 128
EPS = 1e-6
GRID_W = 64
CONV_K = 31
HEADS = 4
HEAD_W = 64
Q_LORA = 192
KV_LORA = 128
ROPE = 32
ROPE_THETA = 10000.0
N_EXPERTS = 16
FF = 1024
EC_CAPACITY = 2
MAIN_W = 2304
IN_EXT_W = 2688
ATT_SCALE = (HEAD_W + ROPE) ** -0.5
LANES = 128
SUBLANES = 8
VMEM_LIMIT = 56 * 1024 * 1024


def _iota(shape, dim):
    return lax.broadcasted_iota(jnp.int32, shape, dim)


def _dot(a, b):
    return jnp.dot(a, b, preferred_element_type=F32)


def _dot_nt(a, b):
    return lax.dot_general(a, b, (((1,), (1,)), ((), ())), preferred_element_type=F32)


def _split_bf16(x):
    hi = x.astype(BF16)
    lo = (x - hi.astype(F32)).astype(BF16)
    return hi, lo


def _sigmoid(x):
    return 1.0 / (1.0 + jnp.exp(-x))


def _silu(x):
    return x * _sigmoid(x)


def _rms(x):
    return x * lax.rsqrt(jnp.mean(x * x, axis=-1, keepdims=True) + EPS)


def _wspec(arr, l):
    nd = arr.ndim
    return pl.BlockSpec((None,) + tuple(arr.shape[1:]), lambda *_: (l,) + (0,) * (nd - 1))


def _params(n_axes, sem="parallel"):
    return pltpu.CompilerParams(dimension_semantics=(sem,) * n_axes, vmem_limit_bytes=VMEM_LIMIT)


def _mod_kernel(c_ref, w_ref, b_ref, o_ref):
    cv = c_ref[...]
    s_hi, s_lo = _split_bf16(_silu(cv))
    w_hi, w_lo = _split_bf16(w_ref[...])
    o_ref[...] = _dot(s_hi, w_hi) + _dot(s_lo, w_hi) + _dot(s_hi, w_lo) + b_ref[...]


def _mod_call(cvec, w_mod, b_mod):
    nt = 1536
    return pl.pallas_call(
        _mod_kernel,
        grid=(DEPTH, 6 * D // nt),
        in_specs=[
            pl.BlockSpec((8, D), lambda l, j: (0, 0)),
            pl.BlockSpec((None, D, nt), lambda l, j: (l, 0, j)),
            pl.BlockSpec((None, 1, nt), lambda l, j: (l, 0, j)),
        ],
        out_specs=pl.BlockSpec((None, 8, nt), lambda l, j: (l, 0, j)),
        out_shape=jax.ShapeDtypeStruct((DEPTH, 8, 6 * D), F32),
        compiler_params=_params(2),
        name="mod",
    )(cvec, w_mod, b_mod.reshape(DEPTH, 1, 6 * D))


def _inproj_kernel(rope, x_ref, mod_ref, g1_ref, w_ref, kvg_ref, qg_ref, wq_ref, wqr_ref, wk_ref, wv_ref, *rest):
    if rope:
        cq_ref, sq_ref, ck_ref, sk_ref, main_ref, q_ref, k_ref, v_ref = rest
    else:
        main_ref, q_ref, k_ref, v_ref, ckv_ref, slab_ref = rest
    x = x_ref[...]
    mod = mod_ref[...]
    sh1 = mod[:, 0:D]
    sc1 = mod[:, D:2 * D]
    h = _rms(x) * g1_ref[...]
    h = h * (1.0 + sc1) + sh1
    hb = h.astype(BF16)
    for cb in range(MAIN_W // 256):
        main_ref[:, cb * 256:(cb + 1) * 256] = _dot(hb, w_ref[:, cb * 256:(cb + 1) * 256])
    tail = _dot(hb, w_ref[:, MAIN_W:IN_EXT_W])
    ckv_n = _rms(tail[:, 0:128]) * kvg_ref[...]
    c256 = tail[:, 128:384]
    lane256 = _iota((1, 256), 1)
    ms = jnp.sum(jnp.where(lane256 < Q_LORA, c256 * c256, 0.0), axis=-1, keepdims=True) * (1.0 / Q_LORA)
    cqn = (c256 * lax.rsqrt(ms + EPS) * qg_ref[...]).astype(BF16)
    ckvb = ckv_n.astype(BF16)
    q = _dot(cqn, wq_ref[...])
    kn = _dot(ckvb, wk_ref[...])
    v_ref[...] = _dot(ckvb, wv_ref[...]).astype(BF16)
    slab = tail[:, 256:384]
    if rope:
        qr = _dot(cqn, wqr_ref[...])
        kx = slab * ck_ref[...] + pltpu.roll(slab * sk_ref[...], LANES - ROPE, 1)
        cq_t = cq_ref[...]
        sq_t = sq_ref[...]
        for hh in range(HEADS):
            sl = slice(hh * LANES, (hh + 1) * LANES)
            q_ref[:, sl] = (q[:, sl] * cq_t + qr[:, sl] * sq_t).astype(BF16)
    else:
        lane128 = _iota((1, LANES), 1)
        kx = jnp.where((lane128 >= HEAD_W) & (lane128 < HEAD_W + ROPE), slab, 0.0)
        q_ref[...] = q.astype(BF16)
        ckv_ref[...] = ckv_n
        slab_ref[...] = slab
    for hh in range(HEADS):
        sl = slice(hh * LANES, (hh + 1) * LANES)
        k_ref[:, sl] = (kn[:, sl] + kx).astype(BF16)


def _inproj_call(l, x, mod_p, n_seq_tokens, W, rope_tabs):
    t = x.shape[0]
    tm = 256
    tps = n_seq_tokens // tm
    bm = mod_p.shape[0]
    rope = rope_tabs is not None
    row = lambda i: (i, 0)
    ins = [x, mod_p, W["norm1_g"], W["w_in"], W["kv_g"], W["q_g"], W["wq"], W["wqr"], W["wk"], W["wv"]]
    in_specs = [
        pl.BlockSpec((tm, D), row),
        pl.BlockSpec((None, 1, 6 * D), (lambda i: (i // tps, 0, 0)) if bm > 1 else (lambda i: (0, 0, 0))),
    ] + [_wspec(a, l) for a in ins[2:]]
    out_shape = [
        jax.ShapeDtypeStruct((t, MAIN_W), F32),
        jax.ShapeDtypeStruct((t, 512), BF16),
        jax.ShapeDtypeStruct((t, 512), BF16),
        jax.ShapeDtypeStruct((t, 256), BF16),
    ]
    out_specs = [pl.BlockSpec((tm, MAIN_W), row), pl.BlockSpec((tm, 512), row), pl.BlockSpec((tm, 512), row),
                 pl.BlockSpec((tm, 256), row)]
    if rope:
        ins += list(rope_tabs)
        in_specs += [pl.BlockSpec((tm, LANES), lambda i: (i % tps, 0))] * 4
    else:
        out_shape += [jax.ShapeDtypeStruct((t, 128), F32), jax.ShapeDtypeStruct((t, 128), F32)]
        out_specs += [pl.BlockSpec((tm, 128), row), pl.BlockSpec((tm, 128), row)]
    return pl.pallas_call(
        functools.partial(_inproj_kernel, rope),
        grid=(t // tm,),
        in_specs=in_specs,
        out_specs=out_specs,
        out_shape=out_shape,
        compiler_params=_params(1),
        name="in_proj",
    )(*ins)


def _kvcache_kernel(ckv_ref, kpe_ref, wk_ref, wv_ref, k_ref, v_ref):
    cb = ckv_ref[...].astype(BF16)
    kn = _dot(cb, wk_ref[...])
    kx = kpe_ref[...]
    for hh in range(HEADS):
        sl = slice(hh * LANES, (hh + 1) * LANES)
        k_ref[:, sl] = (kn[:, sl] + kx).astype(BF16)
    v_ref[...] = _dot(cb, wv_ref[...]).astype(BF16)


def _kvcache_call(cache_ckv, kpe128, wk, wv):
    b, depth, m, _ = cache_ckv.shape
    blk = lambda w: pl.BlockSpec((None, None, m, w), lambda l, i: (i, l, 0, 0))
    oblk = lambda w: pl.BlockSpec((None, None, m, w), lambda l, i: (l, i, 0, 0))
    wblk = lambda a: pl.BlockSpec((None,) + tuple(a.shape[1:]), lambda l, i: (l, 0, 0))
    return pl.pallas_call(
        _kvcache_kernel,
        grid=(depth, b),
        in_specs=[blk(128), blk(128), wblk(wk), wblk(wv)],
        out_specs=[oblk(512), oblk(256)],
        out_shape=[jax.ShapeDtypeStruct((depth, b, m, 512), BF16), jax.ShapeDtypeStruct((depth, b, m, 256), BF16)],
        compiler_params=_params(2),
        name="kv_cache",
    )(cache_ckv, kpe128, wk, wv)


def _gmlp_kernel(rb, m_ref, sgg_ref, ws_ref, bm_ref, o_ref):
    group = lax.shift_right_logical(_iota((1, 256), 1), 6)
    for cidx in range(rb // CHUNK):
        rows = slice(cidx * CHUNK, (cidx + 1) * CHUNK)
        u = m_ref[rows, 0:256]
        vn = _rms(m_ref[rows, 256:512]) * sgg_ref[...]
        acc = jnp.zeros((CHUNK, 256), F32)
        for g in range(HEADS):
            acc = acc + _dot(ws_ref[g], jnp.where(group == g, vn, 0.0).astype(BF16))
        o_ref[rows, :] = u * (acc + bm_ref[...])


def _gmlp_call(l, main, W):
    t = main.shape[0]
    rb = 512
    ins = [main, W["sg_g"], W["sg_w"], W["sg_bm"]]
    return pl.pallas_call(
        functools.partial(_gmlp_kernel, rb),
        grid=(t // rb,),
        in_specs=[pl.BlockSpec((rb, 512), lambda i: (i, 0))] + [_wspec(a, l) for a in ins[1:]],
        out_specs=pl.BlockSpec((rb, 256), lambda i: (i, 0)),
        out_shape=jax.ShapeDtypeStruct((t, 256), F32),
        compiler_params=_params(1),
        name="gmlp",
    )(*ins)


HALO = 16


def _conv_kernel(rb, nblk, cur_ref, prev_ref, next_ref, wdw_ref, bdw_ref, lng_ref, lnb_ref, wpw_ref, o_ref, pad_ref,
                 sh_ref):
    i = pl.program_id(0)
    keep_prev = (i % nblk != 0).astype(F32)
    keep_next = (i % nblk != nblk - 1).astype(F32)

    def glu(blk):
        return blk[:, 0:256] * _sigmoid(blk[:, 256:512])

    pad_ref[0:HALO, :] = glu(prev_ref[...]) * keep_prev
    pad_ref[HALO:HALO + rb, :] = glu(cur_ref[...])
    pad_ref[HALO + rb:2 * HALO + rb, :] = glu(next_ref[...]) * keep_next
    span = rb + 2 * HALO - SUBLANES
    for s in range(SUBLANES):
        sh_ref[s] = pad_ref[s:s + span, :]
    off = HALO - CONV_K // 2
    for j in range(rb // CHUNK):
        acc = jnp.zeros((CHUNK, 256), F32)
        for k in range(CONV_K):
            start = j * CHUNK + (off + k) // SUBLANES * SUBLANES
            acc = acc + sh_ref[(off + k) % SUBLANES, start:start + CHUNK, :] * wdw_ref[k:k + 1, :]
        y = acc + bdw_ref[...]
        mu = jnp.mean(y, axis=-1, keepdims=True)
        dlt = y - mu
        var = jnp.mean(dlt * dlt, axis=-1, keepdims=True)
        z = _silu(dlt * lax.rsqrt(var + EPS) * lng_ref[...] + lnb_ref[...])
        o_ref[j * CHUNK:(j + 1) * CHUNK, :] = _dot(z.astype(BF16), wpw_ref[...])


def _conv_call(l, main, n_seq_tokens, W):
    t = main.shape[0]
    rb = min(n_seq_tokens, 512)
    nblk = n_seq_tokens // rb
    per = rb // HALO
    last = t // HALO - 1
    ins = [main, main, main, W["conv_w"], W["conv_b"], W["conv_ln_g"], W["conv_ln_b"], W["conv_pw"]]
    return pl.pallas_call(
        functools.partial(_conv_kernel, rb, nblk),
        grid=(t // rb,),
        in_specs=[
            pl.BlockSpec((rb, 512), lambda i: (i, 1)),
            pl.BlockSpec((HALO, 512), lambda i: (jnp.maximum(i * per - 1, 0), 1)),
            pl.BlockSpec((HALO, 512), lambda i: (jnp.minimum((i + 1) * per, last), 1)),
        ] + [_wspec(a, l) for a in ins[3:]],
        out_specs=pl.BlockSpec((rb, 256), lambda i: (i, 0)),
        out_shape=jax.ShapeDtypeStruct((t, 256), F32),
        scratch_shapes=[pltpu.VMEM((rb + 2 * HALO, 256), F32),
                        pltpu.VMEM((SUBLANES, rb + 2 * HALO - SUBLANES, 256), F32)],
        compiler_params=_params(1),
        name="conv",
    )(*ins)


def _ret_kernel(nc, has_s0, rq_ref, rk_ref, rv_ref, gf_ref, gb_ref, intra_ref, qd_ref, kdt_ref, cd_ref, gng_ref, *rest):
    if has_s0:
        s0_ref, y_ref, sf_ref, sb_ref, s_ref, kvb_ref, of_ref, ob_ref = rest
    else:
        y_ref, sf_ref, sb_ref, s_ref, kvb_ref, of_ref, ob_ref = rest
    same_head = lax.shift_right_logical(_iota((256, 256), 0), 6) == lax.shift_right_logical(_iota((256, 256), 1), 6)
    avg = jnp.where(same_head, 1.0 / HEAD_W, 0.0).astype(BF16)
    row_head = lax.shift_right_logical(_iota((256, CHUNK), 0), 6)
    lane_head = lax.shift_right_logical(_iota((CHUNK, 256), 1), 6)

    def head_mean(x):
        hi, lo = _split_bf16(x)
        return _dot(hi, avg) + _dot(lo, avg)

    def rows_of(cidx):
        return pl.ds(pl.multiple_of(cidx * CHUNK, CHUNK), CHUNK)

    def compact(st):
        return st[:, 0:64] + st[:, 64:128] + st[:, 128:192] + st[:, 192:256]

    s_ref[...] = s0_ref[0] if has_s0 else jnp.zeros((256, 256), F32)

    def fwd_body(cidx, carry):
        rows = rows_of(cidx)
        qb = rq_ref[rows, :].astype(BF16)
        v = rv_ref[rows, :]
        vb = v.astype(BF16)
        kt = rk_ref[rows, :].T
        kbd = jnp.concatenate([jnp.where(row_head == hh, kt, 0.0).astype(BF16) for hh in range(HEADS)], axis=1)
        s = _dot(qb, kbd)
        p = jnp.concatenate([(s * intra_ref[0]).astype(BF16), (s * intra_ref[1]).astype(BF16)], axis=0)
        vbd = jnp.concatenate([jnp.where(lane_head == hh, v, 0.0).astype(BF16) for hh in range(HEADS)], axis=0)
        inner = _dot(p, vbd)
        kts = jnp.concatenate([(kt * kdt_ref[0]).astype(BF16), (kt * kdt_ref[1]).astype(BF16)], axis=0)
        kv = _dot(kts, vb)
        kvb_ref[cidx] = jnp.where(same_head, kv[256:512, :], 0.0)
        ob_ref[rows, :] = inner[CHUNK:2 * CHUNK, :]
        st = s_ref[...]
        of_ref[rows, :] = inner[0:CHUNK, :] + _dot(qb, st.astype(BF16)) * qd_ref[0]
        s_ref[...] = cd_ref[0] * st + jnp.where(same_head, kv[0:256, :], 0.0)
        return carry

    lax.fori_loop(0, nc, fwd_body, 0, unroll=2)
    sf_ref[...] = compact(s_ref[...])
    s_ref[...] = s0_ref[1] if has_s0 else jnp.zeros((256, 256), F32)

    def bwd_body(it, carry):
        cidx = nc - 1 - it
        rows = rows_of(cidx)
        st = s_ref[...]
        ob_ref[rows, :] = ob_ref[rows, :] + _dot(rq_ref[rows, :].astype(BF16), st.astype(BF16)) * qd_ref[1]
        s_ref[...] = cd_ref[1] * st + kvb_ref[cidx]
        return carry

    lax.fori_loop(0, nc, bwd_body, 0, unroll=2)
    sb_ref[...] = compact(s_ref[...])

    def norm_body(blk, carry):
        rows = pl.ds(pl.multiple_of(blk * 2 * CHUNK, 2 * CHUNK), 2 * CHUNK)
        o = jnp.concatenate([of_ref[rows, :], ob_ref[rows, :]], axis=0)
        dlt = o - head_mean(o)
        nrm = dlt * lax.rsqrt(head_mean(dlt * dlt) + EPS) * gng_ref[...]
        y_ref[rows, :] = (_silu(gf_ref[rows, :]) * nrm[0:2 * CHUNK, :]
                          + _silu(gb_ref[rows, :]) * nrm[2 * CHUNK:4 * CHUNK, :])
        return carry

    lax.fori_loop(0, nc // 2, norm_body, 0)


def _ret_call(l, main, n_seq_tokens, W, s0_bd):
    t = main.shape[0]
    b = t // n_seq_tokens
    nc = n_seq_tokens // CHUNK
    has_s0 = s0_bd is not None
    col = lambda j: pl.BlockSpec((n_seq_tokens, 256), lambda i: (i, j))
    ins = [main] * 5 + [W["ret_intra"], W["ret_qd"], W["ret_kdt"], W["ret_cd"], W["ret_gn_g"]]
    in_specs = [col(4), col(5), col(6), col(7), col(8)] + [_wspec(a, l) for a in ins[5:]]
    if has_s0:
        ins.append(s0_bd)
        in_specs.append(pl.BlockSpec((None, 2, 256, 256), lambda i: (i, 0, 0, 0)))
    st = pl.BlockSpec((None, 256, HEAD_W), lambda i: (i, 0, 0))
    return pl.pallas_call(
        functools.partial(_ret_kernel, nc, has_s0),
        grid=(b,),
        in_specs=in_specs,
        out_specs=[pl.BlockSpec((n_seq_tokens, 256), lambda i: (i, 0)), st, st],
        out_shape=[jax.ShapeDtypeStruct((t, 256), F32), jax.ShapeDtypeStruct((b, 256, HEAD_W), F32),
                   jax.ShapeDtypeStruct((b, 256, HEAD_W), F32)],
        scratch_shapes=[pltpu.VMEM((256, 256), F32), pltpu.VMEM((nc, 256, 256), F32),
                        pltpu.VMEM((n_seq_tokens, 256), F32), pltpu.VMEM((n_seq_tokens, 256), F32)],
        compiler_params=_params(1),
        name="ret",
    )(*ins)


def _attn_kernel(nparts, q_ref, *refs):
    k_refs = refs[0:2 * nparts:2]
    v_refs = refs[1:2 * nparts:2]
    o_ref = refs[2 * nparts]
    head = lax.shift_right_logical(_iota((1, 256), 1), 6)
    o = jnp.zeros(o_ref.shape, F32)
    for hh in range(HEADS):
        sl = slice(hh * LANES, (hh + 1) * LANES)
        qh = q_ref[:, sl]
        ss = [_dot_nt(qh, k_ref[:, sl]) * ATT_SCALE for k_ref in k_refs]
        m = ss[0].max(axis=-1, keepdims=True)
        for s in ss[1:]:
            m = jnp.maximum(m, s.max(axis=-1, keepdims=True))
        es = [jnp.exp(s - m) for s in ss]
        den = es[0].sum(axis=-1, keepdims=True)
        for e in es[1:]:
            den = den + e.sum(axis=-1, keepdims=True)
        inv = 1.0 / den
        oh = _dot((es[0] * inv).astype(BF16), v_refs[0][...])
        for e, v_ref in zip(es[1:], v_refs[1:]):
            oh = oh + _dot((e * inv).astype(BF16), v_ref[...])
        o = o + jnp.where(head == hh, oh, 0.0)
    o_ref[...] = o


def _attn_call(q, kv_parts, n_seq_tokens):
    t = q.shape[0]
    tq = 256
    nq = n_seq_tokens // tq
    b = t // n_seq_tokens
    ins = [q]
    in_specs = [pl.BlockSpec((tq, 512), lambda i, j: (i * nq + j, 0))]
    for k, v, m in kv_parts:
        ins += [k, v]
        in_specs += [pl.BlockSpec((m, 512), lambda i, j: (i, 0)), pl.BlockSpec((m, 256), lambda i, j: (i, 0))]
    return pl.pallas_call(
        functools.partial(_attn_kernel, len(kv_parts)),
        grid=(b, nq),
        in_specs=in_specs,
        out_specs=pl.BlockSpec((tq, 256), lambda i, j: (i * nq + j, 0)),
        out_shape=jax.ShapeDtypeStruct((t, 256), F32),
        compiler_params=_params(2),
        name="attn",
    )(*ins)


def _outproj_kernel(ya_ref, yb_ref, yc_ref, yd_ref, x_ref, mod_ref, wo_ref, g2_ref, rhi_ref, rlo_ref,
                    x1_ref, h2_ref, aff_ref):
    mix = _dot(ya_ref[...].astype(BF16), wo_ref[0:256, :])
    mix = mix + _dot(yb_ref[...].astype(BF16), wo_ref[256:512, :])
    mix = mix + _dot(yc_ref[...].astype(BF16), wo_ref[512:768, :])
    mix = mix + _dot(yd_ref[...].astype(BF16), wo_ref[768:1024, :])
    mod = mod_ref[...]
    x1 = x_ref[...] + mod[:, 2 * D:3 * D] * mix
    x1_ref[...] = x1
    h2 = _rms(x1) * g2_ref[...]
    h2 = h2 * (1.0 + mod[:, 4 * D:5 * D]) + mod[:, 3 * D:4 * D]
    h_hi, h_lo = _split_bf16(h2)
    h2_ref[...] = h_hi
    logits = _dot(h_hi, rhi_ref[...]) + _dot(h_lo, rhi_ref[...]) + _dot(h_hi, rlo_ref[...])
    lane = _iota((1, LANES), 1)
    logits = jnp.where(lane < N_EXPERTS, logits, -1e30)
    e = jnp.exp(logits - logits.max(axis=-1, keepdims=True))
    aff_ref[...] = e / e.sum(axis=-1, keepdims=True)


def _outproj_call(l, ys, x, mod_p, n_seq_tokens, W):
    t = x.shape[0]
    tm = 256
    tps = n_seq_tokens // tm
    bm = mod_p.shape[0]
    row = lambda i: (i, 0)
    ins = list(ys) + [x, mod_p, W["w_out"], W["norm2_g"], W["r_hi"], W["r_lo"]]
    in_specs = [pl.BlockSpec((tm, 256), row)] * 4 + [
        pl.BlockSpec((tm, D), row),
        pl.BlockSpec((None, 1, 6 * D), (lambda i: (i // tps, 0, 0)) if bm > 1 else (lambda i: (0, 0, 0))),
    ] + [_wspec(a, l) for a in ins[6:]]
    return pl.pallas_call(
        _outproj_kernel,
        grid=(t // tm,),
        in_specs=in_specs,
        out_specs=[pl.BlockSpec((tm, D), row), pl.BlockSpec((tm, D), row), pl.BlockSpec((tm, LANES), row)],
        out_shape=[jax.ShapeDtypeStruct((t, D), F32), jax.ShapeDtypeStruct((t, D), BF16),
                   jax.ShapeDtypeStruct((t, LANES), F32)],
        compiler_params=_params(1),
        name="out_proj",
    )(*ins)


GROUP_ROWS = 512
CUM_BLK = 256


def _route_kernel(n, cap, nseq, aff_ref, h2_ref, slot_ref, xe_ref, gs_ref, cum_ref, slot_t_ref):
    lane_row = _iota((1, LANES), 1)
    a = aff_ref[0:n, :]
    for s in range(1, nseq):
        a = a + pltpu.roll(aff_ref[s * n:(s + 1) * n, :], N_EXPERTS * s, 1)
    capf = jnp.float32(cap)

    def bisect(_, lohi):
        lo, hi = lohi
        mid = lo + lax.shift_right_logical(hi - lo, 1)
        cnt = jnp.sum(jnp.where(a >= pltpu.bitcast(mid, F32), 1.0, 0.0), axis=0, keepdims=True)
        ok = cnt >= capf
        return jnp.where(ok, mid, lo), jnp.where(ok, hi, mid)

    lo0 = jnp.zeros((1, LANES), jnp.int32)
    hi0 = jnp.full((1, LANES), 0x3F800001, jnp.int32)
    lo, hi = lax.fori_loop(0, 31, bisect, (lo0, hi0))
    gt = a >= pltpu.bitcast(hi, F32)
    eq = (a >= pltpu.bitcast(lo, F32)) & jnp.logical_not(gt)
    need = capf - jnp.sum(jnp.where(gt, 1.0, 0.0), axis=0, keepdims=True)
    tri = jnp.where(_iota((CUM_BLK, CUM_BLK), 0) >= _iota((CUM_BLK, CUM_BLK), 1), 1.0, 0.0).astype(BF16)

    def cumsum_rows(flags):
        carry = jnp.zeros((1, LANES), F32)
        for rb in range(n // CUM_BLK):
            rows = slice(rb * CUM_BLK, (rb + 1) * CUM_BLK)
            part = _dot(tri, flags[rows, :].astype(BF16)) + carry
            cum_ref[rows, :] = part
            carry = part[CUM_BLK - 1:CUM_BLK, :]
        return cum_ref[...]

    eq_rank = cumsum_rows(jnp.where(eq, 1.0, 0.0))
    sel = gt | (eq & (eq_rank <= need))
    pos = cumsum_rows(jnp.where(sel, 1.0, 0.0))
    slot = jnp.where(sel, pos - 1.0, -1.0)
    for s in range(nseq):
        own = slot if s == 0 else pltpu.roll(slot, LANES - N_EXPERTS * s, 1)
        slot_ref[s * n:(s + 1) * n, :] = jnp.where(lane_row < N_EXPERTS, own, -1.0)
    slot_t_ref[...] = slot.T
    a_hi, a_lo = _split_bf16(a)
    a_hilo = jnp.concatenate([a_hi, a_lo], axis=1)
    ones = jnp.ones((LANES, LANES), BF16)
    gexp = GROUP_ROWS // cap
    shift = int(np.log2(cap))
    row_e = lax.shift_right_logical(_iota((GROUP_ROWS, LANES), 0), shift)
    slot_id = _iota((cap, n), 0).astype(F32)
    lane = _iota((GROUP_ROWS, LANES), 1)
    for s in range(nseq):
        h2 = h2_ref[s * n:(s + 1) * n, :]
        for g in range(N_EXPERTS * cap // GROUP_ROWS):
            pieces = []
            for j in range(gexp):
                e_lane = N_EXPERTS * s + g * gexp + j
                mine_row = jnp.broadcast_to(slot_t_ref[e_lane:e_lane + 1, :], (cap, n))
                pieces.append(jnp.where(mine_row == slot_id, 1.0, 0.0).astype(BF16))
            onehot = pieces[0] if gexp == 1 else jnp.concatenate(pieces, axis=0)
            xe = _dot(onehot, h2).astype(BF16)
            gboth = _dot(onehot, a_hilo)
            mine = lane == row_e + (g * gexp + N_EXPERTS * s)
            g_hi, g_lo = _split_bf16(jnp.where(mine, gboth[:, 0:LANES] + gboth[:, LANES:2 * LANES], 0.0))
            gsb = _dot(g_hi, ones) + _dot(g_lo, ones)
            for j in range(gexp):
                xe_ref[g * gexp + j, s * cap:(s + 1) * cap, :] = xe[j * cap:(j + 1) * cap, :]
                gs_ref[g * gexp + j, s * cap:(s + 1) * cap, :] = gsb[j * cap:(j + 1) * cap, :]


def _route_call(aff, h2, n_seq_tokens):
    t = aff.shape[0]
    n = n_seq_tokens
    b = t // n
    cap = EC_CAPACITY * n // N_EXPERTS
    nseq = min(b, LANES // N_EXPERTS, max(1, 2048 // n))
    return pl.pallas_call(
        functools.partial(_route_kernel, n, cap, nseq),
        grid=(b // nseq,),
        in_specs=[pl.BlockSpec((nseq * n, LANES), lambda i: (i, 0)), pl.BlockSpec((nseq * n, D), lambda i: (i, 0))],
        out_specs=[pl.BlockSpec((nseq * n, LANES), lambda i: (i, 0)),
                   pl.BlockSpec((N_EXPERTS, nseq * cap, D), lambda i: (0, i, 0)),
                   pl.BlockSpec((N_EXPERTS, nseq * cap, LANES), lambda i: (0, i, 0))],
        out_shape=[jax.ShapeDtypeStruct((t, LANES), F32),
                   jax.ShapeDtypeStruct((N_EXPERTS, b * cap, D), BF16),
                   jax.ShapeDtypeStruct((N_EXPERTS, b * cap, LANES), F32)],
        scratch_shapes=[pltpu.VMEM((n, LANES), F32), pltpu.VMEM((LANES, n), F32)],
        compiler_params=_params(1),
        name="route",
    )(aff, h2)


FF_TILE = 512


def _ffn_kernel(xc_ref, xl_ref, gc_ref, gl_ref, wg_ref, wu_ref, wd_ref, yc_ref, yl_ref, accc_ref, accl_ref):
    f = pl.program_id(1)
    wg = wg_ref[...].astype(BF16)
    wu = wu_ref[...].astype(BF16)
    wd = wd_ref[...].astype(BF16)
    for x_ref, g_ref, y_ref, acc_ref in ((xc_ref, gc_ref, yc_ref, accc_ref), (xl_ref, gl_ref, yl_ref, accl_ref)):
        x = x_ref[...]
        a = _dot(x, wg)
        hmid = (_silu(a) * _dot(x, wu)).astype(BF16)
        contrib = _dot(hmid, wd)

        @pl.when(f == 0)
        def _():
            acc_ref[...] = contrib

        @pl.when(f == FF // FF_TILE - 1)
        def _():
            gate = jnp.concatenate([g_ref[...]] * (D // LANES), axis=1)
            y_ref[...] = ((acc_ref[...] + contrib) * gate).astype(BF16)


def _ffn_call(l, xe_c, xe_l, gs_c, gs_l, w_gate, w_up, w_down):
    rc = xe_c.shape[1]
    rl = xe_l.shape[1]
    ex = lambda r, w: pl.BlockSpec((None, r, w), lambda e, f: (e, 0, 0))
    return pl.pallas_call(
        _ffn_kernel,
        grid=(N_EXPERTS, FF // FF_TILE),
        in_specs=[
            ex(rc, D), ex(rl, D), ex(rc, LANES), ex(rl, LANES),
            pl.BlockSpec((None, None, D, FF_TILE), lambda e, f: (l, e, 0, f)),
            pl.BlockSpec((None, None, D, FF_TILE), lambda e, f: (l, e, 0, f)),
            pl.BlockSpec((None, None, FF_TILE, D), lambda e, f: (l, e, f, 0)),
        ],
        out_specs=[ex(rc, D), ex(rl, D)],
        out_shape=[jax.ShapeDtypeStruct((N_EXPERTS, rc, D), BF16), jax.ShapeDtypeStruct((N_EXPERTS, rl, D), BF16)],
        scratch_shapes=[pltpu.VMEM((rc, D), F32), pltpu.VMEM((rl, D), F32)],
        compiler_params=pltpu.CompilerParams(dimension_semantics=("parallel", "arbitrary"),
                                             vmem_limit_bytes=VMEM_LIMIT),
        name="ffn",
    )(xe_c, xe_l, gs_c, gs_l, w_gate, w_up, w_down)


COMB_ROWS = 256


def _combine_kernel(cap, final, ye_ref, slot_ref, x1_ref, mod_ref, *rest):
    if final:
        fg_ref, x2_ref, yf_ref = rest
    else:
        (x2_ref,) = rest
    gexp = GROUP_ROWS // cap
    shift = int(np.log2(cap))
    col_e = lax.shift_right_logical(_iota((LANES, GROUP_ROWS), 1), shift)
    col_s = (_iota((1, GROUP_ROWS), 1) & (cap - 1)).astype(F32)
    lane_e = _iota((LANES, GROUP_ROWS), 0)
    sb = slot_ref[...].astype(BF16)
    acc = jnp.zeros((COMB_ROWS, D), F32)
    for g in range(N_EXPERTS * cap // GROUP_ROWS):
        expand_m = jnp.where(lane_e == col_e + g * gexp, 1.0, 0.0).astype(BF16)
        onehot_t = jnp.where(_dot(sb, expand_m) == col_s, 1.0, 0.0).astype(BF16)
        if cap % LANES == 0:
            for j in range(gexp):
                acc = acc + _dot(onehot_t[:, j * cap:(j + 1) * cap], ye_ref[g * gexp + j])
        else:
            ye = jnp.concatenate([ye_ref[g * gexp + j] for j in range(gexp)], axis=0)
            acc = acc + _dot(onehot_t, ye)
    mod = mod_ref[...]
    x2 = x1_ref[...] + mod[:, 5 * D:6 * D] * acc
    x2_ref[...] = x2
    if final:
        yf_ref[...] = _rms(x2) * fg_ref[...]


def _combine_call(ye, slot, x1, mod_p, n_seq_tokens, final_g):
    t = x1.shape[0]
    n = n_seq_tokens
    b = t // n
    nr = n // COMB_ROWS
    cap = EC_CAPACITY * n // N_EXPERTS
    bm = mod_p.shape[0]
    final = final_g is not None
    row = lambda i, j: (i * nr + j, 0)
    ins = [ye, slot, x1, mod_p]
    in_specs = [
        pl.BlockSpec((N_EXPERTS, cap, D), lambda i, j: (0, i, 0)),
        pl.BlockSpec((COMB_ROWS, LANES), row),
        pl.BlockSpec((COMB_ROWS, D), row),
        pl.BlockSpec((None, 1, 6 * D), (lambda i, j: (i, 0, 0)) if bm > 1 else (lambda i, j: (0, 0, 0))),
    ]
    out_shape = [jax.ShapeDtypeStruct((t, D), F32)]
    out_specs = [pl.BlockSpec((COMB_ROWS, D), row)]
    if final:
        ins.append(final_g)
        in_specs.append(pl.BlockSpec((1, D), lambda i, j: (0, 0)))
        out_shape.append(jax.ShapeDtypeStruct((t, D), F32))
        out_specs.append(pl.BlockSpec((COMB_ROWS, D), row))
    return pl.pallas_call(
        functools.partial(_combine_kernel, cap, final),
        grid=(b, nr),
        in_specs=in_specs,
        out_specs=out_specs,
        out_shape=out_shape,
        compiler_params=_params(2),
        name="combine",
    )(*ins)


def _rot_cols(w):
    a, b, c, d = (w[..., 8 * i:8 * (i + 1)] for i in range(4))
    return jnp.concatenate([-b, a, -d, c], axis=-1)


def _rope_tables(n):
    rows = n // GRID_W
    row = jnp.repeat(jnp.arange(rows, dtype=F32), GRID_W)
    col = jnp.tile(jnp.arange(GRID_W, dtype=F32), rows)
    inv = ROPE_THETA ** (-jnp.arange(0, ROPE // 2, 2, dtype=F32) / (ROPE // 2))
    ra = row[:, None] * inv
    ca = col[:, None] * inv
    ang = jnp.concatenate([ra, ra, ca, ca], axis=-1)
    cos, sin = jnp.cos(ang), jnp.sin(ang)
    z32, z64, z96 = (jnp.zeros((n, w), F32) for w in (32, 64, 96))
    cq = jnp.concatenate([jnp.ones((n, 64), F32), cos, z32], axis=1)
    sq = jnp.concatenate([z64, sin, z32], axis=1)
    ck = jnp.concatenate([z64, cos, z32], axis=1)
    sk = jnp.concatenate([z96, sin], axis=1)
    return cq, sq, ck, sk


def _retention_tables(p_f, p_b):
    pos = jnp.arange(CHUNK, dtype=F32)
    diff = pos[:, None] - pos[None, :]

    def one(p, backward):
        lg = jnp.log1p(-jnp.exp2(p.astype(F32)))[:, :, None]
        dd = -diff if backward else diff
        intra = jnp.where(dd >= 0, jnp.exp(jnp.maximum(dd, 0.0) * lg[..., None]), 0.0)
        qexp = (CHUNK - pos) if backward else (pos + 1.0)
        kexp = pos if backward else (CHUNK - 1.0 - pos)
        qd = jnp.exp(qexp * lg)
        kd = jnp.exp(kexp * lg)
        cd = jnp.exp(CHUNK * lg)
        nl = p.shape[0]
        intra_w = jnp.swapaxes(intra, 1, 2).reshape(nl, CHUNK, HEADS * CHUNK)
        qd_w = jnp.repeat(jnp.swapaxes(qd, 1, 2), HEAD_W, axis=2)
        kd_t = jnp.repeat(kd, HEAD_W, axis=1)
        cdw = jnp.broadcast_to(jnp.repeat(cd, HEAD_W, axis=1), (nl, 256, 256))
        return intra_w, qd_w, kd_t, cdw

    f = one(p_f, False)
    b = one(p_b, True)
    return tuple(jnp.stack([x, y], axis=1) for x, y in zip(f, b))


def _prepare(norm1_g, w_in, sg_norm_g, sg_w, sg_b, conv_w, conv_b, conv_ln_g, conv_ln_b, conv_pw, ret_decay_f,
             ret_decay_b, ret_gn_g, q_norm_g, w_uq, kv_norm_g, w_ukv, w_out, norm2_g, router):
    L = DEPTH
    row = lambda a: a.reshape(L, 1, -1)
    kpe_cols = w_in[:, :, 2624:2656]
    w_main = jnp.concatenate([w_in[:, :, 0:1280], w_in[:, :, 1280:1536] * (HEAD_W ** -0.5), w_in[:, :, 1536:2304]], axis=2)
    w_ext = jnp.concatenate([w_main, w_in[:, :, 2496:2624], w_in[:, :, 2304:2496], kpe_cols, _rot_cols(kpe_cols)], axis=2)
    uq = w_uq.reshape(L, Q_LORA, HEADS, HEAD_W + ROPE)
    pad_rows = lambda a: jnp.pad(a, ((0, 0), (0, 256 - Q_LORA), (0, 0)))
    wq = pad_rows(jnp.pad(uq, ((0, 0), (0, 0), (0, 0), (0, LANES - HEAD_W - ROPE))).reshape(L, Q_LORA, 512))
    uq_rot = jnp.pad(_rot_cols(uq[..., HEAD_W:]), ((0, 0), (0, 0), (0, 0), (HEAD_W, LANES - HEAD_W - ROPE)))
    wqr = pad_rows(uq_rot.reshape(L, Q_LORA, 512))
    ukv = w_ukv.reshape(L, KV_LORA, HEADS, 2 * HEAD_W)
    wk = jnp.pad(ukv[..., :HEAD_W], ((0, 0), (0, 0), (0, 0), (0, LANES - HEAD_W))).reshape(L, KV_LORA, 512)
    wv = ukv[..., HEAD_W:].reshape(L, KV_LORA, 256)
    intra, qd, kd, cd = _retention_tables(ret_decay_f, ret_decay_b)
    r_pad = jnp.pad(router, ((0, 0), (0, 0), (0, LANES - N_EXPERTS)))
    r_hi = r_pad.astype(BF16)
    r_lo = (r_pad - r_hi.astype(F32)).astype(BF16)
    return dict(
        norm1_g=row(norm1_g), w_in=w_ext.astype(BF16), kv_g=row(kv_norm_g),
        q_g=jnp.pad(row(q_norm_g), ((0, 0), (0, 0), (0, 256 - Q_LORA))),
        wq=wq.astype(BF16), wqr=wqr.astype(BF16), wk=wk.astype(BF16), wv=wv.astype(BF16),
        sg_g=row(sg_norm_g), sg_w=sg_w.astype(BF16), sg_bm=jnp.repeat(jnp.swapaxes(sg_b, 1, 2), HEAD_W, axis=2),
        conv_w=jnp.pad(conv_w, ((0, 0), (0, 1), (0, 0))), conv_b=row(conv_b), conv_ln_g=row(conv_ln_g),
        conv_ln_b=row(conv_ln_b), conv_pw=conv_pw.astype(BF16),
        ret_intra=intra, ret_qd=qd, ret_kdt=kd, ret_cd=cd, ret_gn_g=row(ret_gn_g),
        w_out=w_out.astype(BF16), norm2_g=row(norm2_g), r_hi=r_hi, r_lo=r_lo,
    )


def _block_diag_states(state):
    eye = jnp.eye(HEADS, dtype=state.dtype)
    bd = state[:, :, :, :, None, :] * eye[None, None, :, None, :, None]
    return bd.reshape(state.shape[0], 2, HEADS * HEAD_W, HEADS * HEAD_W)


def _mixers(l, x, mod_p, n, W, rope_tabs, s0_bd, cache_kv):
    outs = _inproj_call(l, x, mod_p, n, W, rope_tabs)
    main, q, k, v = outs[:4]
    ya = _gmlp_call(l, main, W)
    yb = _conv_call(l, main, n, W)
    yc, s_f, s_b = _ret_call(l, main, n, W, s0_bd)
    parts = [(k, v, n)]
    if cache_kv is not None:
        kc, vc = cache_kv
        parts = [(kc, vc, kc.shape[0] // (x.shape[0] // n))] + parts
    yd = _attn_call(q, parts, n)
    x1, h2, aff = _outproj_call(l, (ya, yb, yc, yd), x, mod_p, n, W)
    slot, xe, gs = _route_call(aff, h2, n)
    return x1, slot, xe, gs, outs[4:], (s_f, s_b)


def kernel(x_prompt, x_sample, cache_mla_ckv, cache_mla_kpe, state_ret, c, c_ctx, w_mod, b_mod, norm1_g, w_in, sg_norm_g, sg_w, sg_b, conv_w, conv_b, conv_ln_g, conv_ln_b, conv_pw, ret_decay_f, ret_decay_b, ret_gn_g, q_norm_g, w_uq, kv_norm_g, w_ukv, w_out, norm2_g, router, w_gate, w_up, w_down, final_norm_g):
    bc, nc_tok, _ = x_prompt.shape
    bl, nl_tok, _ = x_sample.shape
    W = _prepare(norm1_g, w_in, sg_norm_g, sg_w, sg_b, conv_w, conv_b, conv_ln_g, conv_ln_b, conv_pw, ret_decay_f,
                 ret_decay_b, ret_gn_g, q_norm_g, w_uq, kv_norm_g, w_ukv, w_out, norm2_g, router)
    cvec = jnp.zeros((8, D), F32).at[0].set(c_ctx).at[1:1 + bl].set(c)
    mod_all = _mod_call(cvec, w_mod, b_mod)
    rope_tabs = _rope_tables(nl_tok)
    kpe128 = jnp.pad(cache_mla_kpe, ((0, 0), (0, 0), (0, 0), (HEAD_W, LANES - HEAD_W - ROPE)))
    kc_all, vc_all = _kvcache_call(cache_mla_ckv, kpe128, W["wk"], W["wv"])
    s0_all = _block_diag_states(jnp.swapaxes(state_ret, 0, 1).reshape(DEPTH * bl, 2, HEADS, HEAD_W, HEAD_W))
    s0_all = s0_all.reshape(DEPTH, bl, 2, 256, 256)
    fg = final_norm_g.reshape(1, D)
    xp = x_prompt.reshape(bc * nc_tok, D)
    xs = x_sample.reshape(bl * nl_tok, D)
    ckvs, kpes, rets = [], [], []
    yp = ys = None
    for l in range(DEPTH):
        mod_c = mod_all[l, 0:1].reshape(1, 1, 6 * D)
        mod_l = mod_all[l, 1:1 + bl].reshape(bl, 1, 6 * D)
        x1c, slot_c, xe_c, gs_c, (ckv, slab), (s_f, s_b) = _mixers(l, xp, mod_c, nc_tok, W, None, None, None)
        kc = kc_all[l].reshape(-1, 512)
        vc = vc_all[l].reshape(-1, 256)
        x1l, slot_l, xe_l, gs_l, _, _ = _mixers(l, xs, mod_l, nl_tok, W, rope_tabs, s0_all[l], (kc, vc))
        ye_c, ye_l = _ffn_call(l, xe_c, xe_l, gs_c, gs_l, w_gate, w_up, w_down)
        last = l == DEPTH - 1
        res_c = _combine_call(ye_c, slot_c, x1c, mod_c, nc_tok, fg if last else None)
        res_l = _combine_call(ye_l, slot_l, x1l, mod_l, nl_tok, fg if last else None)
        xp, xs = res_c[0], res_l[0]
        if last:
            yp, ys = res_c[1], res_l[1]
        ckvs.append(ckv.reshape(bc, nc_tok, KV_LORA))
        kpes.append(slab[:, HEAD_W:HEAD_W + ROPE].reshape(bc, nc_tok, ROPE))
        rets.append(jnp.stack([s_f, s_b], axis=1).reshape(bc, 2, HEADS, HEAD_W, HEAD_W))
    return (yp.reshape(bc, nc_tok, D), ys.reshape(bl, nl_tok, D), jnp.stack(ckvs, axis=1), jnp.stack(kpes, axis=1),
            jnp.stack(rets, axis=1))
```

```python
import functools

import jax
import jax.numpy as jnp
import numpy as np
from jax import lax
from jax.experimental import pallas as pl
from jax.experimental.pallas import tpu as pltpu

F32 = jnp.float32
BF16 = jnp.bfloat16

D = 1024
DEPTH = 4
CHUNK = 128
EPS = 1e-6
GRID_W = 64
CONV_K = 31
HEADS = 4
HEAD_W = 64
Q_LORA = 192
KV_LORA = 128
ROPE = 32
ROPE_THETA = 10000.0
N_EXPERTS = 16
FF = 1024
EC_CAPACITY = 2
MAIN_W = 2304
TAIL_W = 384
ATT_SCALE = (HEAD_W + ROPE) ** -0.5
LANES = 128
SUBLANES = 8
VMEM_LIMIT = 56 * 1024 * 1024


def _iota(shape, dim):
    return lax.broadcasted_iota(jnp.int32, shape, dim)


def _dot(a, b):
    return jnp.dot(a, b, preferred_element_type=F32)


def _split_bf16(x):
    hi = x.astype(BF16)
    lo = (x - hi.astype(F32)).astype(BF16)
    return hi, lo


def _sigmoid(x):
    return 1.0 / (1.0 + jnp.exp(-x))


def _silu(x):
    return x * _sigmoid(x)


def _rms(x):
    return x * lax.rsqrt(jnp.mean(x * x, axis=-1, keepdims=True) + EPS)


def _wspec(arr, l):
    nd = arr.ndim
    return pl.BlockSpec((None,) + tuple(arr.shape[1:]), lambda *_: (l,) + (0,) * (nd - 1))


def _params(n_axes, sem="parallel"):
    return pltpu.CompilerParams(dimension_semantics=(sem,) * n_axes, vmem_limit_bytes=VMEM_LIMIT)


def _mod_kernel(c_ref, w_ref, b_ref, o_ref):
    cv = c_ref[...]
    s_hi, s_lo = _split_bf16(_silu(cv))
    w_hi, w_lo = _split_bf16(w_ref[...])
    o_ref[...] = _dot(s_hi, w_hi) + _dot(s_lo, w_hi) + _dot(s_hi, w_lo) + b_ref[...]


def _mod_call(cvec, w_mod, b_mod):
    nt = 1536
    return pl.pallas_call(
        _mod_kernel,
        grid=(DEPTH, 6 * D // nt),
        in_specs=[
            pl.BlockSpec((8, D), lambda l, j: (0, 0)),
            pl.BlockSpec((None, D, nt), lambda l, j: (l, 0, j)),
            pl.BlockSpec((None, 1, nt), lambda l, j: (l, 0, j)),
        ],
        out_specs=pl.BlockSpec((None, 8, nt), lambda l, j: (l, 0, j)),
        out_shape=jax.ShapeDtypeStruct((DEPTH, 8, 6 * D), F32),
        compiler_params=_params(2),
        name="mod",
    )(cvec, w_mod, b_mod.reshape(DEPTH, 1, 6 * D))


N_MAIN_BLOCKS = MAIN_W // 256
RET_K_BLOCK = 5


def _inproj_kernel(rope, x_ref, mod_ref, g1_ref, w_ref, wt_ref, kvg_ref, qg_ref, wq_ref, wqr_ref, wk_ref, wv_ref,
                   sgg_ref, sgw_ref, sgb_ref, *rest):
    if rope:
        cq_ref, sq_ref, ck_ref, sk_ref, ya_ref, cv_ref, rqkv_ref, rg_ref, qt_ref, k_ref, vt_ref, wb_ref = rest
    else:
        ya_ref, cv_ref, rqkv_ref, rg_ref, qt_ref, k_ref, vt_ref, ckv_ref, slab_ref, wb_ref = rest

    @pl.when(pl.program_id(0) == 0)
    def _():
        for cb in range(N_MAIN_BLOCKS):
            blk = w_ref[:, cb * 256:(cb + 1) * 256]
            if cb == RET_K_BLOCK:
                blk = blk * (HEAD_W ** -0.5)
            wb_ref[:, cb * 256:(cb + 1) * 256] = blk.astype(BF16)

    x = x_ref[...]
    mod = mod_ref[...]
    h = _rms(x) * g1_ref[...]
    h = h * (1.0 + mod[:, D:2 * D]) + mod[:, 0:D]
    hb = h.astype(BF16)
    proj = lambda cb: _dot(hb, wb_ref[:, cb * 256:(cb + 1) * 256])
    u = proj(0)
    vn = _rms(proj(1)) * sgg_ref[...]
    group = lax.shift_right_logical(_iota((1, 256), 1), 6)
    for cidx in range(x.shape[0] // CHUNK):
        rows = slice(cidx * CHUNK, (cidx + 1) * CHUNK)
        acc = jnp.zeros((CHUNK, 256), F32)
        for g in range(HEADS):
            acc = acc + _dot(sgw_ref[g], jnp.where(group == g, vn[rows, :], 0.0).astype(BF16))
        ya_ref[rows, :] = u[rows, :] * (acc + sgb_ref[...])
    cv_ref[:, 0:256] = proj(2)
    cv_ref[:, 256:512] = proj(3)
    for j in range(3):
        rqkv_ref[:, j * 256:(j + 1) * 256] = proj(4 + j).astype(BF16)
    rg_ref[:, 0:256] = proj(7)
    rg_ref[:, 256:512] = proj(8)
    tail = _dot(hb, wt_ref[...])
    ckv_n = _rms(tail[:, 0:128]) * kvg_ref[...]
    c256 = tail[:, 128:384]
    lane256 = _iota((1, 256), 1)
    ms = jnp.sum(jnp.where(lane256 < Q_LORA, c256 * c256, 0.0), axis=-1, keepdims=True) * (1.0 / Q_LORA)
    cqn = (c256 * lax.rsqrt(ms + EPS) * qg_ref[...]).astype(BF16)
    ckvb = ckv_n.astype(BF16)
    q = _dot(cqn, wq_ref[...])
    kn = _dot(ckvb, wk_ref[...])
    vt_ref[...] = _dot(ckvb, wv_ref[...]).T.astype(BF16)
    slab = tail[:, 256:384]
    if rope:
        qr = _dot(cqn, wqr_ref[...])
        kx = slab * ck_ref[...] + pltpu.roll(slab * sk_ref[...], LANES - ROPE, 1)
        q = q * jnp.concatenate([cq_ref[...]] * HEADS, axis=1) + qr * jnp.concatenate([sq_ref[...]] * HEADS, axis=1)
    else:
        lane128 = _iota((1, LANES), 1)
        kx = jnp.where((lane128 >= HEAD_W) & (lane128 < HEAD_W + ROPE), slab, 0.0)
        ckv_ref[...] = ckv_n
        slab_ref[...] = slab
    qt_ref[...] = (q * ATT_SCALE).T.astype(BF16)
    for hh in range(HEADS):
        sl = slice(hh * LANES, (hh + 1) * LANES)
        k_ref[:, sl] = (kn[:, sl] + kx).astype(BF16)


def _inproj_call(l, x, mod_p, n_seq_tokens, W, w_in, rope_tabs):
    t = x.shape[0]
    tm = 256
    tps = n_seq_tokens // tm
    bm = mod_p.shape[0]
    rope = rope_tabs is not None
    row = lambda i: (i, 0)
    col = lambda i: (0, i)
    rest = [W["w_tail"], W["kv_g"], W["q_g"], W["wq"], W["wqr"], W["wk"], W["wv"], W["sg_g"], W["sg_w"], W["sg_bm"]]
    ins = [x, mod_p, W["norm1_g"], w_in] + rest
    in_specs = [
        pl.BlockSpec((tm, D), row),
        pl.BlockSpec((None, 1, 6 * D), (lambda i: (i // tps, 0, 0)) if bm > 1 else (lambda i: (0, 0, 0))),
        _wspec(W["norm1_g"], l),
        pl.BlockSpec((None, D, MAIN_W), lambda i: (l, 0, 0)),
    ] + [_wspec(a, l) for a in rest]
    out_shape = [
        jax.ShapeDtypeStruct((t, 256), F32), jax.ShapeDtypeStruct((t, 512), F32),
        jax.ShapeDtypeStruct((t, 768), BF16), jax.ShapeDtypeStruct((t, 512), F32),
        jax.ShapeDtypeStruct((512, t), BF16), jax.ShapeDtypeStruct((t, 512), BF16),
        jax.ShapeDtypeStruct((256, t), BF16),
    ]
    out_specs = [pl.BlockSpec((tm, 256), row), pl.BlockSpec((tm, 512), row), pl.BlockSpec((tm, 768), row),
                 pl.BlockSpec((tm, 512), row), pl.BlockSpec((512, tm), col), pl.BlockSpec((tm, 512), row),
                 pl.BlockSpec((256, tm), col)]
    if rope:
        ins += list(rope_tabs)
        in_specs += [pl.BlockSpec((tm, LANES), lambda i: (i % tps, 0))] * 4
    else:
        out_shape += [jax.ShapeDtypeStruct((t, 128), F32), jax.ShapeDtypeStruct((t, 128), F32)]
        out_specs += [pl.BlockSpec((tm, 128), row), pl.BlockSpec((tm, 128), row)]
    return pl.pallas_call(
        functools.partial(_inproj_kernel, rope),
        grid=(t // tm,),
        in_specs=in_specs,
        out_specs=out_specs,
        out_shape=out_shape,
        scratch_shapes=[pltpu.VMEM((D, MAIN_W), BF16)],
        compiler_params=_params(1, "arbitrary"),
        name="in_proj",
    )(*ins)


def _kvcache_kernel(ckv_ref, kpe_ref, wk_ref, wv_ref, k_ref, vt_ref):
    cb = ckv_ref[...].astype(BF16)
    kn = _dot(cb, wk_ref[...])
    kx = kpe_ref[...]
    for hh in range(HEADS):
        sl = slice(hh * LANES, (hh + 1) * LANES)
        k_ref[:, sl] = (kn[:, sl] + kx).astype(BF16)
    vt_ref[...] = _dot(cb, wv_ref[...]).T.astype(BF16)


def _kvcache_call(cache_ckv, kpe128, wk, wv):
    b, depth, m, _ = cache_ckv.shape
    blk = lambda w: pl.BlockSpec((None, None, m, w), lambda l, i: (i, l, 0, 0))
    wblk = lambda a: pl.BlockSpec((None,) + tuple(a.shape[1:]), lambda l, i: (l, 0, 0))
    return pl.pallas_call(
        _kvcache_kernel,
        grid=(depth, b),
        in_specs=[blk(128), blk(128), wblk(wk), wblk(wv)],
        out_specs=[pl.BlockSpec((None, None, m, 512), lambda l, i: (l, i, 0, 0)),
                   pl.BlockSpec((None, None, 256, m), lambda l, i: (l, i, 0, 0))],
        out_shape=[jax.ShapeDtypeStruct((depth, b, m, 512), BF16), jax.ShapeDtypeStruct((depth, b, 256, m), BF16)],
        compiler_params=_params(2),
        name="kv_cache",
    )(cache_ckv, kpe128, wk, wv)


HALO = 16


def _conv_kernel(rb, nblk, cur_ref, prev_ref, next_ref, wdw_ref, bdw_ref, lng_ref, lnb_ref, wpw_ref, o_ref, pad_ref,
                 sh_ref):
    i = pl.program_id(0)
    keep_prev = (i % nblk != 0).astype(F32)
    keep_next = (i % nblk != nblk - 1).astype(F32)

    def glu(blk):
        return blk[:, 0:256] * _sigmoid(blk[:, 256:512])

    pad_ref[0:HALO, :] = glu(prev_ref[...]) * keep_prev
    pad_ref[HALO:HALO + rb, :] = glu(cur_ref[...])
    pad_ref[HALO + rb:2 * HALO + rb, :] = glu(next_ref[...]) * keep_next
    span = rb + 2 * HALO - SUBLANES
    for s in range(SUBLANES):
        sh_ref[s] = pad_ref[s:s + span, :]
    off = HALO - CONV_K // 2
    for j in range(rb // CHUNK):
        acc = jnp.zeros((CHUNK, 256), F32)
        for k in range(CONV_K):
            start = j * CHUNK + (off + k) // SUBLANES * SUBLANES
            acc = acc + sh_ref[(off + k) % SUBLANES, start:start + CHUNK, :] * wdw_ref[k:k + 1, :]
        y = acc + bdw_ref[...]
        mu = jnp.mean(y, axis=-1, keepdims=True)
        dlt = y - mu
        var = jnp.mean(dlt * dlt, axis=-1, keepdims=True)
        z = _silu(dlt * lax.rsqrt(var + EPS) * lng_ref[...] + lnb_ref[...])
        o_ref[j * CHUNK:(j + 1) * CHUNK, :] = _dot(z.astype(BF16), wpw_ref[...])


def _conv_call(l, cv, n_seq_tokens, W):
    t = cv.shape[0]
    rb = min(n_seq_tokens, 512)
    nblk = n_seq_tokens // rb
    per = rb // HALO
    last = t // HALO - 1
    ins = [cv, cv, cv, W["conv_w"], W["conv_b"], W["conv_ln_g"], W["conv_ln_b"], W["conv_pw"]]
    return pl.pallas_call(
        functools.partial(_conv_kernel, rb, nblk),
        grid=(t // rb,),
        in_specs=[
            pl.BlockSpec((rb, 512), lambda i: (i, 0)),
            pl.BlockSpec((HALO, 512), lambda i: (jnp.maximum(i * per - 1, 0), 0)),
            pl.BlockSpec((HALO, 512), lambda i: (jnp.minimum((i + 1) * per, last), 0)),
        ] + [_wspec(a, l) for a in ins[3:]],
        out_specs=pl.BlockSpec((rb, 256), lambda i: (i, 0)),
        out_shape=jax.ShapeDtypeStruct((t, 256), F32),
        scratch_shapes=[pltpu.VMEM((rb + 2 * HALO, 256), F32),
                        pltpu.VMEM((SUBLANES, rb + 2 * HALO - SUBLANES, 256), F32)],
        compiler_params=_params(1),
        name="conv",
    )(*ins)


def _ret_kernel(nc, has_s0, rq_ref, rk_ref, rv_ref, gf_ref, gb_ref, intra_ref, qd_ref, kdt_ref, cd_ref, gng_ref, *rest):
    if has_s0:
        s0_ref, y_ref, sf_ref, sb_ref, s_ref, kvb_ref, of_ref, ob_ref = rest
    else:
        y_ref, sf_ref, sb_ref, s_ref, kvb_ref, of_ref, ob_ref = rest
    same_head = lax.shift_right_logical(_iota((256, 256), 0), 6) == lax.shift_right_logical(_iota((256, 256), 1), 6)
    avg = jnp.where(same_head, 1.0 / HEAD_W, 0.0).astype(BF16)
    row_head = lax.shift_right_logical(_iota((256, CHUNK), 0), 6)
    lane_head = lax.shift_right_logical(_iota((CHUNK, 256), 1), 6)

    def head_mean(x):
        hi, lo = _split_bf16(x)
        return _dot(hi, avg) + _dot(lo, avg)

    def rows_of(cidx):
        return pl.ds(pl.multiple_of(cidx * CHUNK, CHUNK), CHUNK)

    def compact(st):
        return st[:, 0:64] + st[:, 64:128] + st[:, 128:192] + st[:, 192:256]

    s_ref[...] = s0_ref[0] if has_s0 else jnp.zeros((256, 256), F32)

    def fwd_body(cidx, carry):
        rows = rows_of(cidx)
        qb = rq_ref[rows, :]
        vb = rv_ref[rows, :]
        kt = rk_ref[rows, :].astype(F32).T
        kbd = jnp.concatenate([jnp.where(row_head == hh, kt, 0.0).astype(BF16) for hh in range(HEADS)], axis=1)
        s = _dot(qb, kbd)
        p = jnp.concatenate([(s * intra_ref[0]).astype(BF16), (s * intra_ref[1]).astype(BF16)], axis=0)
        zero = jnp.zeros_like(vb)
        vbd = jnp.concatenate([jnp.where(lane_head == hh, vb, zero) for hh in range(HEADS)], axis=0)
        inner = _dot(p, vbd)
        kts = jnp.concatenate([(kt * kdt_ref[0]).astype(BF16), (kt * kdt_ref[1]).astype(BF16)], axis=0)
        kv = _dot(kts, vb)
        kvb_ref[cidx] = jnp.where(same_head, kv[256:512, :], 0.0)
        ob_ref[rows, :] = inner[CHUNK:2 * CHUNK, :]
        st = s_ref[...]
        of_ref[rows, :] = inner[0:CHUNK, :] + _dot(qb, st.astype(BF16)) * qd_ref[0]
        s_ref[...] = cd_ref[0] * st + jnp.where(same_head, kv[0:256, :], 0.0)
        return carry

    lax.fori_loop(0, nc, fwd_body, 0, unroll=2)
    sf_ref[...] = compact(s_ref[...])
    s_ref[...] = s0_ref[1] if has_s0 else jnp.zeros((256, 256), F32)

    def bwd_body(it, carry):
        cidx = nc - 1 - it
        rows = rows_of(cidx)
        st = s_ref[...]
        ob_ref[rows, :] = ob_ref[rows, :] + _dot(rq_ref[rows, :], st.astype(BF16)) * qd_ref[1]
        s_ref[...] = cd_ref[1] * st + kvb_ref[cidx]
        return carry

    lax.fori_loop(0, nc, bwd_body, 0, unroll=2)
    sb_ref[...] = compact(s_ref[...])

    def norm_body(blk, carry):
        rows = pl.ds(pl.multiple_of(blk * 2 * CHUNK, 2 * CHUNK), 2 * CHUNK)
        o = jnp.concatenate([of_ref[rows, :], ob_ref[rows, :]], axis=0)
        dlt = o - head_mean(o)
        nrm = dlt * lax.rsqrt(head_mean(dlt * dlt) + EPS) * gng_ref[...]
        y_ref[rows, :] = (_silu(gf_ref[rows, :]) * nrm[0:2 * CHUNK, :]
                          + _silu(gb_ref[rows, :]) * nrm[2 * CHUNK:4 * CHUNK, :])
        return carry

    lax.fori_loop(0, nc // 2, norm_body, 0)


def _ret_call(l, rqkv, rg, n_seq_tokens, W, s0_bd):
    t = rqkv.shape[0]
    b = t // n_seq_tokens
    nc = n_seq_tokens // CHUNK
    has_s0 = s0_bd is not None
    col = lambda j: pl.BlockSpec((n_seq_tokens, 256), lambda i: (i, j))
    ins = [rqkv] * 3 + [rg] * 2 + [W["ret_intra"], W["ret_qd"], W["ret_kdt"], W["ret_cd"], W["ret_gn_g"]]
    in_specs = [col(0), col(1), col(2), col(0), col(1)] + [_wspec(a, l) for a in ins[5:]]
    if has_s0:
        ins.append(s0_bd)
        in_specs.append(pl.BlockSpec((None, 2, 256, 256), lambda i: (i, 0, 0, 0)))
    st = pl.BlockSpec((None, 256, HEAD_W), lambda i: (i, 0, 0))
    return pl.pallas_call(
        functools.partial(_ret_kernel, nc, has_s0),
        grid=(b,),
        in_specs=in_specs,
        out_specs=[pl.BlockSpec((n_seq_tokens, 256), lambda i: (i, 0)), st, st],
        out_shape=[jax.ShapeDtypeStruct((t, 256), F32), jax.ShapeDtypeStruct((b, 256, HEAD_W), F32),
                   jax.ShapeDtypeStruct((b, 256, HEAD_W), F32)],
        scratch_shapes=[pltpu.VMEM((256, 256), F32), pltpu.VMEM((nc, 256, 256), F32),
                        pltpu.VMEM((n_seq_tokens, 256), F32), pltpu.VMEM((n_seq_tokens, 256), F32)],
        compiler_params=_params(1),
        name="ret",
    )(*ins)


def _attn_kernel(nparts, qt_ref, *refs):
    k_refs = refs[0:2 * nparts:2]
    vt_refs = refs[1:2 * nparts:2]
    o_ref = refs[2 * nparts]
    outs = []
    for hh in range(HEADS):
        sl = slice(hh * LANES, (hh + 1) * LANES)
        qth = qt_ref[sl, :]
        ss = [_dot(k_ref[:, sl], qth) for k_ref in k_refs]
        m = ss[0].max(axis=0, keepdims=True)
        for s in ss[1:]:
            m = jnp.maximum(m, s.max(axis=0, keepdims=True))
        es = [jnp.exp(s - m) for s in ss]
        den = es[0].sum(axis=0, keepdims=True)
        for e in es[1:]:
            den = den + e.sum(axis=0, keepdims=True)
        vsl = slice(hh * HEAD_W, (hh + 1) * HEAD_W)
        oh = _dot(vt_refs[0][vsl, :], es[0].astype(BF16))
        for e, vt_ref in zip(es[1:], vt_refs[1:]):
            oh = oh + _dot(vt_ref[vsl, :], e.astype(BF16))
        outs.append(oh * (1.0 / den))
    o_ref[...] = jnp.concatenate(outs, axis=0).T


def _attn_call(l, qt, k, vt, cache, n_seq_tokens):
    t = k.shape[0]
    n = n_seq_tokens
    tq = 256
    nq = n // tq
    b = t // n
    ins = [qt]
    in_specs = [pl.BlockSpec((512, tq), lambda i, j: (0, i * nq + j))]
    if cache is not None:
        kc, vtc = cache
        m = kc.shape[2]
        ins += [kc, vtc]
        in_specs += [pl.BlockSpec((None, None, m, 512), lambda i, j: (l, i, 0, 0)),
                     pl.BlockSpec((None, None, 256, m), lambda i, j: (l, i, 0, 0))]
    ins += [k, vt]
    in_specs += [pl.BlockSpec((n, 512), lambda i, j: (i, 0)), pl.BlockSpec((256, n), lambda i, j: (0, i))]
    return pl.pallas_call(
        functools.partial(_attn_kernel, len(ins) // 2),
        grid=(b, nq),
        in_specs=in_specs,
        out_specs=pl.BlockSpec((tq, 256), lambda i, j: (i * nq + j, 0)),
        out_shape=jax.ShapeDtypeStruct((t, 256), F32),
        compiler_params=_params(2),
        name="attn",
    )(*ins)


def _outproj_kernel(ya_ref, yb_ref, yc_ref, yd_ref, x_ref, mod_ref, wo_ref, g2_ref, rhi_ref, rlo_ref,
                    x1_ref, h2_ref, aff_ref, wb_ref):
    @pl.when(pl.program_id(0) == 0)
    def _():
        for rb in range(4):
            wb_ref[rb * 256:(rb + 1) * 256, :] = wo_ref[rb * 256:(rb + 1) * 256, :].astype(BF16)

    mix = _dot(ya_ref[...].astype(BF16), wb_ref[0:256, :])
    mix = mix + _dot(yb_ref[...].astype(BF16), wb_ref[256:512, :])
    mix = mix + _dot(yc_ref[...].astype(BF16), wb_ref[512:768, :])
    mix = mix + _dot(yd_ref[...].astype(BF16), wb_ref[768:1024, :])
    mod = mod_ref[...]
    x1 = x_ref[...] + mod[:, 2 * D:3 * D] * mix
    x1_ref[...] = x1
    h2 = _rms(x1) * g2_ref[...]
    h2 = h2 * (1.0 + mod[:, 4 * D:5 * D]) + mod[:, 3 * D:4 * D]
    h_hi, h_lo = _split_bf16(h2)
    h2_ref[...] = h_hi
    logits = _dot(h_hi, rhi_ref[...]) + _dot(h_lo, rhi_ref[...]) + _dot(h_hi, rlo_ref[...])
    lane = _iota((1, LANES), 1)
    logits = jnp.where(lane < N_EXPERTS, logits, -1e30)
    e = jnp.exp(logits - logits.max(axis=-1, keepdims=True))
    aff_ref[...] = e / e.sum(axis=-1, keepdims=True)


def _outproj_call(l, ys, x, mod_p, n_seq_tokens, W, w_out):
    t = x.shape[0]
    tm = 256
    tps = n_seq_tokens // tm
    bm = mod_p.shape[0]
    row = lambda i: (i, 0)
    ins = list(ys) + [x, mod_p, w_out, W["norm2_g"], W["r_hi"], W["r_lo"]]
    in_specs = [pl.BlockSpec((tm, 256), row)] * 4 + [
        pl.BlockSpec((tm, D), row),
        pl.BlockSpec((None, 1, 6 * D), (lambda i: (i // tps, 0, 0)) if bm > 1 else (lambda i: (0, 0, 0))),
    ] + [_wspec(a, l) for a in ins[6:]]
    return pl.pallas_call(
        _outproj_kernel,
        grid=(t // tm,),
        in_specs=in_specs,
        out_specs=[pl.BlockSpec((tm, D), row), pl.BlockSpec((tm, D), row), pl.BlockSpec((tm, LANES), row)],
        out_shape=[jax.ShapeDtypeStruct((t, D), F32), jax.ShapeDtypeStruct((t, D), BF16),
                   jax.ShapeDtypeStruct((t, LANES), F32)],
        scratch_shapes=[pltpu.VMEM((D, D), BF16)],
        compiler_params=_params(1, "arbitrary"),
        name="out_proj",
    )(*ins)


GROUP_ROWS = 512
CUM_BLK = 256


def _route_kernel(n, cap, nseq, aff_ref, h2_ref, slot_ref, xe_ref, gs_ref, cum_ref, slot_t_ref):
    lane_row = _iota((1, LANES), 1)
    a = aff_ref[0:n, :]
    for s in range(1, nseq):
        a = a + pltpu.roll(aff_ref[s * n:(s + 1) * n, :], N_EXPERTS * s, 1)
    capf = jnp.float32(cap)

    def bisect(_, lohi):
        lo, hi = lohi
        mid = lo + lax.shift_right_logical(hi - lo, 1)
        cnt = jnp.sum(jnp.where(a >= pltpu.bitcast(mid, F32), 1.0, 0.0), axis=0, keepdims=True)
        ok = cnt >= capf
        return jnp.where(ok, mid, lo), jnp.where(ok, hi, mid)

    lo0 = jnp.zeros((1, LANES), jnp.int32)
    hi0 = jnp.full((1, LANES), 0x3F800001, jnp.int32)
    lo, hi = lax.fori_loop(0, 31, bisect, (lo0, hi0))
    gt = a >= pltpu.bitcast(hi, F32)
    eq = (a >= pltpu.bitcast(lo, F32)) & jnp.logical_not(gt)
    need = capf - jnp.sum(jnp.where(gt, 1.0, 0.0), axis=0, keepdims=True)
    tri = jnp.where(_iota((CUM_BLK, CUM_BLK), 0) >= _iota((CUM_BLK, CUM_BLK), 1), 1.0, 0.0).astype(BF16)

    def cumsum_rows(flags):
        carry = jnp.zeros((1, LANES), F32)
        for rb in range(n // CUM_BLK):
            rows = slice(rb * CUM_BLK, (rb + 1) * CUM_BLK)
            part = _dot(tri, flags[rows, :].astype(BF16)) + carry
            cum_ref[rows, :] = part
            carry = part[CUM_BLK - 1:CUM_BLK, :]
        return cum_ref[...]

    eq_rank = cumsum_rows(jnp.where(eq, 1.0, 0.0))
    sel = gt | (eq & (eq_rank <= need))
    pos = cumsum_rows(jnp.where(sel, 1.0, 0.0))
    slot = jnp.where(sel, pos - 1.0, -1.0)
    for s in range(nseq):
        own = slot if s == 0 else pltpu.roll(slot, LANES - N_EXPERTS * s, 1)
        slot_ref[s * n:(s + 1) * n, :] = jnp.where(lane_row < N_EXPERTS, own, -1.0)
    slot_t_ref[...] = slot.T
    a_hi, a_lo = _split_bf16(a)
    a_hilo = jnp.concatenate([a_hi, a_lo], axis=1)
    ones = jnp.ones((LANES, LANES), BF16)
    gexp = GROUP_ROWS // cap
    shift = int(np.log2(cap))
    row_e = lax.shift_right_logical(_iota((GROUP_ROWS, LANES), 0), shift)
    slot_id = _iota((cap, n), 0).astype(F32)
    lane = _iota((GROUP_ROWS, LANES), 1)
    for s in range(nseq):
        h2 = h2_ref[s * n:(s + 1) * n, :]
        for g in range(N_EXPERTS * cap // GROUP_ROWS):
            pieces = []
            for j in range(gexp):
                e_lane = N_EXPERTS * s + g * gexp + j
                mine_row = jnp.broadcast_to(slot_t_ref[e_lane:e_lane + 1, :], (cap, n))
                pieces.append(jnp.where(mine_row == slot_id, 1.0, 0.0).astype(BF16))
            onehot = pieces[0] if gexp == 1 else jnp.concatenate(pieces, axis=0)
            xe = _dot(onehot, h2).astype(BF16)
            gboth = _dot(onehot, a_hilo)
            mine = lane == row_e + (g * gexp + N_EXPERTS * s)
            g_hi, g_lo = _split_bf16(jnp.where(mine, gboth[:, 0:LANES] + gboth[:, LANES:2 * LANES], 0.0))
            gsb = _dot(g_hi, ones) + _dot(g_lo, ones)
            for j in range(gexp):
                xe_ref[g * gexp + j, s * cap:(s + 1) * cap, :] = xe[j * cap:(j + 1) * cap, :]
                gs_ref[g * gexp + j, s * cap:(s + 1) * cap, :] = gsb[j * cap:(j + 1) * cap, :]


def _route_call(aff, h2, n_seq_tokens):
    t = aff.shape[0]
    n = n_seq_tokens
    b = t // n
    cap = EC_CAPACITY * n // N_EXPERTS
    nseq = min(b, LANES // N_EXPERTS, max(1, 2048 // n))
    return pl.pallas_call(
        functools.partial(_route_kernel, n, cap, nseq),
        grid=(b // nseq,),
        in_specs=[pl.BlockSpec((nseq * n, LANES), lambda i: (i, 0)), pl.BlockSpec((nseq * n, D), lambda i: (i, 0))],
        out_specs=[pl.BlockSpec((nseq * n, LANES), lambda i: (i, 0)),
                   pl.BlockSpec((N_EXPERTS, nseq * cap, D), lambda i: (0, i, 0)),
                   pl.BlockSpec((N_EXPERTS, nseq * cap, LANES), lambda i: (0, i, 0))],
        out_shape=[jax.ShapeDtypeStruct((t, LANES), F32),
                   jax.ShapeDtypeStruct((N_EXPERTS, b * cap, D), BF16),
                   jax.ShapeDtypeStruct((N_EXPERTS, b * cap, LANES), F32)],
        scratch_shapes=[pltpu.VMEM((n, LANES), F32), pltpu.VMEM((LANES, n), F32)],
        compiler_params=_params(1),
        name="route",
    )(aff, h2)


FF_TILE = 512


def _ffn_kernel(xc_ref, xl_ref, gc_ref, gl_ref, wg_ref, wu_ref, wd_ref, yc_ref, yl_ref, accc_ref, accl_ref):
    f = pl.program_id(1)
    wg = wg_ref[...].astype(BF16)
    wu = wu_ref[...].astype(BF16)
    wd = wd_ref[...].astype(BF16)
    for x_ref, g_ref, y_ref, acc_ref in ((xc_ref, gc_ref, yc_ref, accc_ref), (xl_ref, gl_ref, yl_ref, accl_ref)):
        x = x_ref[...]
        a = _dot(x, wg)
        hmid = (_silu(a) * _dot(x, wu)).astype(BF16)
        contrib = _dot(hmid, wd)

        @pl.when(f == 0)
        def _():
            acc_ref[...] = contrib

        @pl.when(f == FF // FF_TILE - 1)
        def _():
            gate = jnp.concatenate([g_ref[...]] * (D // LANES), axis=1)
            y_ref[...] = ((acc_ref[...] + contrib) * gate).astype(BF16)


def _ffn_call(l, xe_c, xe_l, gs_c, gs_l, w_gate, w_up, w_down):
    rc = xe_c.shape[1]
    rl = xe_l.shape[1]
    ex = lambda r, w: pl.BlockSpec((None, r, w), lambda e, f: (e, 0, 0))
    return pl.pallas_call(
        _ffn_kernel,
        grid=(N_EXPERTS, FF // FF_TILE),
        in_specs=[
            ex(rc, D), ex(rl, D), ex(rc, LANES), ex(rl, LANES),
            pl.BlockSpec((None, None, D, FF_TILE), lambda e, f: (l, e, 0, f)),
            pl.BlockSpec((None, None, D, FF_TILE), lambda e, f: (l, e, 0, f)),
            pl.BlockSpec((None, None, FF_TILE, D), lambda e, f: (l, e, f, 0)),
        ],
        out_specs=[ex(rc, D), ex(rl, D)],
        out_shape=[jax.ShapeDtypeStruct((N_EXPERTS, rc, D), BF16), jax.ShapeDtypeStruct((N_EXPERTS, rl, D), BF16)],
        scratch_shapes=[pltpu.VMEM((rc, D), F32), pltpu.VMEM((rl, D), F32)],
        compiler_params=pltpu.CompilerParams(dimension_semantics=("parallel", "arbitrary"),
                                             vmem_limit_bytes=VMEM_LIMIT),
        name="ffn",
    )(xe_c, xe_l, gs_c, gs_l, w_gate, w_up, w_down)


COMB_ROWS = 256


def _combine_kernel(cap, final, ye_ref, slot_ref, x1_ref, mod_ref, *rest):
    if final:
        fg_ref, x2_ref, yf_ref = rest
    else:
        (x2_ref,) = rest
    gexp = GROUP_ROWS // cap
    shift = int(np.log2(cap))
    col_e = lax.shift_right_logical(_iota((LANES, GROUP_ROWS), 1), shift)
    col_s = (_iota((1, GROUP_ROWS), 1) & (cap - 1)).astype(F32)
    lane_e = _iota((LANES, GROUP_ROWS), 0)
    sb = slot_ref[...].astype(BF16)
    acc = jnp.zeros((COMB_ROWS, D), F32)
    for g in range(N_EXPERTS * cap // GROUP_ROWS):
        expand_m = jnp.where(lane_e == col_e + g * gexp, 1.0, 0.0).astype(BF16)
        onehot_t = jnp.where(_dot(sb, expand_m) == col_s, 1.0, 0.0).astype(BF16)
        if cap % LANES == 0:
            for j in range(gexp):
                acc = acc + _dot(onehot_t[:, j * cap:(j + 1) * cap], ye_ref[g * gexp + j])
        else:
            ye = jnp.concatenate([ye_ref[g * gexp + j] for j in range(gexp)], axis=0)
            acc = acc + _dot(onehot_t, ye)
    mod = mod_ref[...]
    x2 = x1_ref[...] + mod[:, 5 * D:6 * D] * acc
    x2_ref[...] = x2
    if final:
        yf_ref[...] = _rms(x2) * fg_ref[...]


def _combine_call(ye, slot, x1, mod_p, n_seq_tokens, final_g):
    t = x1.shape[0]
    n = n_seq_tokens
    b = t // n
    nr = n // COMB_ROWS
    cap = EC_CAPACITY * n // N_EXPERTS
    bm = mod_p.shape[0]
    final = final_g is not None
    row = lambda i, j: (i * nr + j, 0)
    ins = [ye, slot, x1, mod_p]
    in_specs = [
        pl.BlockSpec((N_EXPERTS, cap, D), lambda i, j: (0, i, 0)),
        pl.BlockSpec((COMB_ROWS, LANES), row),
        pl.BlockSpec((COMB_ROWS, D), row),
        pl.BlockSpec((None, 1, 6 * D), (lambda i, j: (i, 0, 0)) if bm > 1 else (lambda i, j: (0, 0, 0))),
    ]
    out_shape = [jax.ShapeDtypeStruct((t, D), F32)]
    out_specs = [pl.BlockSpec((COMB_ROWS, D), row)]
    if final:
        ins.append(final_g)
        in_specs.append(pl.BlockSpec((1, D), lambda i, j: (0, 0)))
        out_shape.append(jax.ShapeDtypeStruct((t, D), F32))
        out_specs.append(pl.BlockSpec((COMB_ROWS, D), row))
    return pl.pallas_call(
        functools.partial(_combine_kernel, cap, final),
        grid=(b, nr),
        in_specs=in_specs,
        out_specs=out_specs,
        out_shape=out_shape,
        compiler_params=_params(2),
        name="combine",
    )(*ins)


def _rot_cols(w):
    a, b, c, d = (w[..., 8 * i:8 * (i + 1)] for i in range(4))
    return jnp.concatenate([-b, a, -d, c], axis=-1)


def _rope_tables(n):
    rows = n // GRID_W
    row = jnp.repeat(jnp.arange(rows, dtype=F32), GRID_W)
    col = jnp.tile(jnp.arange(GRID_W, dtype=F32), rows)
    inv = ROPE_THETA ** (-jnp.arange(0, ROPE // 2, 2, dtype=F32) / (ROPE // 2))
    ra = row[:, None] * inv
    ca = col[:, None] * inv
    ang = jnp.concatenate([ra, ra, ca, ca], axis=-1)
    cos, sin = jnp.cos(ang), jnp.sin(ang)
    z32, z64, z96 = (jnp.zeros((n, w), F32) for w in (32, 64, 96))
    cq = jnp.concatenate([jnp.ones((n, 64), F32), cos, z32], axis=1)
    sq = jnp.concatenate([z64, sin, z32], axis=1)
    ck = jnp.concatenate([z64, cos, z32], axis=1)
    sk = jnp.concatenate([z96, sin], axis=1)
    return cq, sq, ck, sk


def _retention_tables(p_f, p_b):
    pos = jnp.arange(CHUNK, dtype=F32)
    diff = pos[:, None] - pos[None, :]

    def one(p, backward):
        lg = jnp.log1p(-jnp.exp2(p.astype(F32)))[:, :, None]
        dd = -diff if backward else diff
        intra = jnp.where(dd >= 0, jnp.exp(jnp.maximum(dd, 0.0) * lg[..., None]), 0.0)
        qexp = (CHUNK - pos) if backward else (pos + 1.0)
        kexp = pos if backward else (CHUNK - 1.0 - pos)
        qd = jnp.exp(qexp * lg)
        kd = jnp.exp(kexp * lg)
        cd = jnp.exp(CHUNK * lg)
        nl = p.shape[0]
        intra_w = jnp.swapaxes(intra, 1, 2).reshape(nl, CHUNK, HEADS * CHUNK)
        qd_w = jnp.repeat(jnp.swapaxes(qd, 1, 2), HEAD_W, axis=2)
        kd_t = jnp.repeat(kd, HEAD_W, axis=1)
        cdw = jnp.broadcast_to(jnp.repeat(cd, HEAD_W, axis=1), (nl, 256, 256))
        return intra_w, qd_w, kd_t, cdw

    f = one(p_f, False)
    b = one(p_b, True)
    return tuple(jnp.stack([x, y], axis=1) for x, y in zip(f, b))


def _prepare(norm1_g, w_in, sg_norm_g, sg_w, sg_b, conv_w, conv_b, conv_ln_g, conv_ln_b, conv_pw, ret_decay_f,
             ret_decay_b, ret_gn_g, q_norm_g, w_uq, kv_norm_g, w_ukv, norm2_g, router):
    L = DEPTH
    row = lambda a: a.reshape(L, 1, -1)
    tail_src = w_in[:, :, MAIN_W:]
    kpe_cols = tail_src[:, :, Q_LORA + KV_LORA:]
    w_tail = jnp.concatenate([tail_src[:, :, Q_LORA:Q_LORA + KV_LORA], tail_src[:, :, :Q_LORA], kpe_cols,
                              _rot_cols(kpe_cols)], axis=2)
    uq = w_uq.reshape(L, Q_LORA, HEADS, HEAD_W + ROPE)
    pad_rows = lambda a: jnp.pad(a, ((0, 0), (0, 256 - Q_LORA), (0, 0)))
    wq = pad_rows(jnp.pad(uq, ((0, 0), (0, 0), (0, 0), (0, LANES - HEAD_W - ROPE))).reshape(L, Q_LORA, 512))
    uq_rot = jnp.pad(_rot_cols(uq[..., HEAD_W:]), ((0, 0), (0, 0), (0, 0), (HEAD_W, LANES - HEAD_W - ROPE)))
    wqr = pad_rows(uq_rot.reshape(L, Q_LORA, 512))
    ukv = w_ukv.reshape(L, KV_LORA, HEADS, 2 * HEAD_W)
    wk = jnp.pad(ukv[..., :HEAD_W], ((0, 0), (0, 0), (0, 0), (0, LANES - HEAD_W))).reshape(L, KV_LORA, 512)
    wv = ukv[..., HEAD_W:].reshape(L, KV_LORA, 256)
    intra, qd, kd, cd = _retention_tables(ret_decay_f, ret_decay_b)
    r_pad = jnp.pad(router, ((0, 0), (0, 0), (0, LANES - N_EXPERTS)))
    r_hi = r_pad.astype(BF16)
    r_lo = (r_pad - r_hi.astype(F32)).astype(BF16)
    return dict(
        norm1_g=row(norm1_g), w_tail=w_tail.astype(BF16), kv_g=row(kv_norm_g),
        q_g=jnp.pad(row(q_norm_g), ((0, 0), (0, 0), (0, 256 - Q_LORA))),
        wq=wq.astype(BF16), wqr=wqr.astype(BF16), wk=wk.astype(BF16), wv=wv.astype(BF16),
        sg_g=row(sg_norm_g), sg_w=sg_w.astype(BF16), sg_bm=jnp.repeat(jnp.swapaxes(sg_b, 1, 2), HEAD_W, axis=2),
        conv_w=jnp.pad(conv_w, ((0, 0), (0, 1), (0, 0))), conv_b=row(conv_b), conv_ln_g=row(conv_ln_g),
        conv_ln_b=row(conv_ln_b), conv_pw=conv_pw.astype(BF16),
        ret_intra=intra, ret_qd=qd, ret_kdt=kd, ret_cd=cd, ret_gn_g=row(ret_gn_g),
        norm2_g=row(norm2_g), r_hi=r_hi, r_lo=r_lo,
    )


def _block_diag_states(state):
    eye = jnp.eye(HEADS, dtype=state.dtype)
    bd = state[:, :, :, :, None, :] * eye[None, None, :, None, :, None]
    return bd.reshape(state.shape[0], 2, HEADS * HEAD_W, HEADS * HEAD_W)


def _mixers(l, x, mod_p, n, W, w_in, w_out, rope_tabs, s0_bd, cache):
    outs = _inproj_call(l, x, mod_p, n, W, w_in, rope_tabs)
    ya, cv, rqkv, rg, qt, k, vt = outs[:7]
    yb = _conv_call(l, cv, n, W)
    yc, s_f, s_b = _ret_call(l, rqkv, rg, n, W, s0_bd)
    yd = _attn_call(l, qt, k, vt, cache, n)
    x1, h2, aff = _outproj_call(l, (ya, yb, yc, yd), x, mod_p, n, W, w_out)
    slot, xe, gs = _route_call(aff, h2, n)
    return x1, slot, xe, gs, outs[7:], (s_f, s_b)


def kernel(x_prompt, x_sample, cache_mla_ckv, cache_mla_kpe, state_ret, c, c_ctx, w_mod, b_mod, norm1_g, w_in, sg_norm_g, sg_w, sg_b, conv_w, conv_b, conv_ln_g, conv_ln_b, conv_pw, ret_decay_f, ret_decay_b, ret_gn_g, q_norm_g, w_uq, kv_norm_g, w_ukv, w_out, norm2_g, router, w_gate, w_up, w_down, final_norm_g):
    bc, nc_tok, _ = x_prompt.shape
    bl, nl_tok, _ = x_sample.shape
    W = _prepare(norm1_g, w_in, sg_norm_g, sg_w, sg_b, conv_w, conv_b, conv_ln_g, conv_ln_b, conv_pw, ret_decay_f,
                 ret_decay_b, ret_gn_g, q_norm_g, w_uq, kv_norm_g, w_ukv, norm2_g, router)
    cvec = jnp.zeros((8, D), F32).at[0].set(c_ctx).at[1:1 + bl].set(c)
    mod_all = _mod_call(cvec, w_mod, b_mod)
    rope_tabs = _rope_tables(nl_tok)
    kpe128 = jnp.pad(cache_mla_kpe, ((0, 0), (0, 0), (0, 0), (HEAD_W, LANES - HEAD_W - ROPE)))
    cache = _kvcache_call(cache_mla_ckv, kpe128, W["wk"], W["wv"])
    s0_all = _block_diag_states(jnp.swapaxes(state_ret, 0, 1).reshape(DEPTH * bl, 2, HEADS, HEAD_W, HEAD_W))
    s0_all = s0_all.reshape(DEPTH, bl, 2, 256, 256)
    fg = final_norm_g.reshape(1, D)
    xp = x_prompt.reshape(bc * nc_tok, D)
    xs = x_sample.reshape(bl * nl_tok, D)
    ckvs, kpes, rets = [], [], []
    yp = ys = None
    for l in range(DEPTH):
        mod_c = mod_all[l, 0:1].reshape(1, 1, 6 * D)
        mod_l = mod_all[l, 1:1 + bl].reshape(bl, 1, 6 * D)
        x1c, slot_c, xe_c, gs_c, (ckv, slab), (s_f, s_b) = _mixers(l, xp, mod_c, nc_tok, W, w_in, w_out, None, None,
                                                                 None)
        x1l, slot_l, xe_l, gs_l, _, _ = _mixers(l, xs, mod_l, nl_tok, W, w_in, w_out, rope_tabs, s0_all[l], cache)
        ye_c, ye_l = _ffn_call(l, xe_c, xe_l, gs_c, gs_l, w_gate, w_up, w_down)
        last = l == DEPTH - 1
        res_c = _combine_call(ye_c, slot_c, x1c, mod_c, nc_tok, fg if last else None)
        res_l = _combine_call(ye_l, slot_l, x1l, mod_l, nl_tok, fg if last else None)
        xp, xs = res_c[0], res_l[0]
        if last:
            yp, ys = res_c[1], res_l[1]
        ckvs.append(ckv.reshape(bc, nc_tok, KV_LORA))
        kpes.append(slab[:, HEAD_W:HEAD_W + ROPE].reshape(bc, nc_tok, ROPE))
        rets.append(jnp.stack([s_f, s_b], axis=1).reshape(bc, 2, HEADS, HEAD_W, HEAD_W))
    return (yp.reshape(bc, nc_tok, D), ys.reshape(bl, nl_tok, D), jnp.stack(ckvs, axis=1), jnp.stack(kpes, axis=1),
            jnp.stack(rets, axis=1))
```

```python
import functools

import jax
import jax.numpy as jnp
import numpy as np
from jax import lax
from jax.experimental import pallas as pl
from jax.experimental.pallas import tpu as pltpu

F32 = jnp.float32
BF16 = jnp.bfloat16

D = 1024
DEPTH = 4
CHUNK = 128
EPS = 1e-6
GRID_W = 64
CONV_K = 31
HEADS = 4
HEAD_W = 64
Q_LORA = 192
KV_LORA = 128
ROPE = 32
ROPE_THETA = 10000.0
N_EXPERTS = 16
FF = 1024
EC_CAPACITY = 2
MAIN_W = 2304
TAIL_W = 384
ATT_SCALE = (HEAD_W + ROPE) ** -0.5
LANES = 128
SUBLANES = 8
VMEM_LIMIT = 56 * 1024 * 1024


def _iota(shape, dim):
    return lax.broadcasted_iota(jnp.int32, shape, dim)


def _dot(a, b):
    return jnp.dot(a, b, preferred_element_type=F32)


def _split_bf16(x):
    hi = x.astype(BF16)
    lo = (x - hi.astype(F32)).astype(BF16)
    return hi, lo


def _sigmoid(x):
    return 1.0 / (1.0 + jnp.exp(-x))


def _silu(x):
    return x * _sigmoid(x)


def _rms(x):
    return x * lax.rsqrt(jnp.mean(x * x, axis=-1, keepdims=True) + EPS)


def _wspec(arr, l):
    nd = arr.ndim
    return pl.BlockSpec((None,) + tuple(arr.shape[1:]), lambda *_: (l,) + (0,) * (nd - 1))


def _params(n_axes, sem="parallel"):
    return pltpu.CompilerParams(dimension_semantics=(sem,) * n_axes, vmem_limit_bytes=VMEM_LIMIT)


def _mod_kernel(c_ref, w_ref, b_ref, o_ref):
    cv = c_ref[...]
    s_hi, s_lo = _split_bf16(_silu(cv))
    w_hi, w_lo = _split_bf16(w_ref[...])
    o_ref[...] = _dot(s_hi, w_hi) + _dot(s_lo, w_hi) + _dot(s_hi, w_lo) + b_ref[...]


def _mod_call(cvec, w_mod, b_mod):
    nt = 1536
    return pl.pallas_call(
        _mod_kernel,
        grid=(DEPTH, 6 * D // nt),
        in_specs=[
            pl.BlockSpec((8, D), lambda l, j: (0, 0)),
            pl.BlockSpec((None, D, nt), lambda l, j: (l, 0, j)),
            pl.BlockSpec((None, 1, nt), lambda l, j: (l, 0, j)),
        ],
        out_specs=pl.BlockSpec((None, 8, nt), lambda l, j: (l, 0, j)),
        out_shape=jax.ShapeDtypeStruct((DEPTH, 8, 6 * D), F32),
        compiler_params=_params(2),
        name="mod",
    )(cvec, w_mod, b_mod.reshape(DEPTH, 1, 6 * D))


N_MAIN_BLOCKS = MAIN_W // 256
RET_K_BLOCK = 5


def _inproj_kernel(rope, x_ref, mod_ref, g1_ref, w_ref, wt_ref, kvg_ref, qg_ref, wq_ref, wqr_ref, wk_ref, wv_ref,
                   sgg_ref, sgw_ref, sgb_ref, *rest):
    if rope:
        cq_ref, sq_ref, ck_ref, sk_ref, ya_ref, cv_ref, rqkv_ref, rg_ref, qt_ref, k_ref, vt_ref, wb_ref = rest
    else:
        ya_ref, cv_ref, rqkv_ref, rg_ref, qt_ref, k_ref, vt_ref, ckv_ref, slab_ref, wb_ref = rest

    @pl.when(pl.program_id(0) == 0)
    def _():
        for cb in range(N_MAIN_BLOCKS):
            blk = w_ref[:, cb * 256:(cb + 1) * 256]
            if cb == RET_K_BLOCK:
                blk = blk * (HEAD_W ** -0.5)
            wb_ref[:, cb * 256:(cb + 1) * 256] = blk.astype(BF16)

    x = x_ref[...]
    mod = mod_ref[...]
    h = _rms(x) * g1_ref[...]
    h = h * (1.0 + mod[:, D:2 * D]) + mod[:, 0:D]
    hb = h.astype(BF16)
    proj = lambda cb: _dot(hb, wb_ref[:, cb * 256:(cb + 1) * 256])
    v_gate = proj(1)
    tail = _dot(hb, wt_ref[...])
    u = proj(0)
    cv_ref[:, 0:256] = proj(2)
    cv_ref[:, 256:512] = proj(3)
    for j in range(3):
        rqkv_ref[:, j * 256:(j + 1) * 256] = proj(4 + j).astype(BF16)
    rg_ref[:, 0:256] = proj(7)
    rg_ref[:, 256:512] = proj(8)
    vn = _rms(v_gate) * sgg_ref[...]
    group = lax.shift_right_logical(_iota((1, 256), 1), 6)
    for cidx in range(x.shape[0] // CHUNK):
        rows = slice(cidx * CHUNK, (cidx + 1) * CHUNK)
        acc = jnp.zeros((CHUNK, 256), F32)
        for g in range(HEADS):
            acc = acc + _dot(sgw_ref[g], jnp.where(group == g, vn[rows, :], 0.0).astype(BF16))
        ya_ref[rows, :] = u[rows, :] * (acc + sgb_ref[...])
    ckv_n = _rms(tail[:, 0:128]) * kvg_ref[...]
    c256 = tail[:, 128:384]
    lane256 = _iota((1, 256), 1)
    ms = jnp.sum(jnp.where(lane256 < Q_LORA, c256 * c256, 0.0), axis=-1, keepdims=True) * (1.0 / Q_LORA)
    cqn = (c256 * lax.rsqrt(ms + EPS) * qg_ref[...]).astype(BF16)
    ckvb = ckv_n.astype(BF16)
    q = _dot(cqn, wq_ref[...])
    kn = _dot(ckvb, wk_ref[...])
    vt_ref[...] = _dot(ckvb, wv_ref[...]).T.astype(BF16)
    slab = tail[:, 256:384]
    if rope:
        qr = _dot(cqn, wqr_ref[...])
        kx = slab * ck_ref[...] + pltpu.roll(slab * sk_ref[...], LANES - ROPE, 1)
        q = q * jnp.concatenate([cq_ref[...]] * HEADS, axis=1) + qr * jnp.concatenate([sq_ref[...]] * HEADS, axis=1)
    else:
        lane128 = _iota((1, LANES), 1)
        kx = jnp.where((lane128 >= HEAD_W) & (lane128 < HEAD_W + ROPE), slab, 0.0)
        ckv_ref[...] = ckv_n
        slab_ref[...] = slab
    qt_ref[...] = (q * ATT_SCALE).T.astype(BF16)
    for hh in range(HEADS):
        sl = slice(hh * LANES, (hh + 1) * LANES)
        k_ref[:, sl] = (kn[:, sl] + kx).astype(BF16)


def _inproj_call(l, x, mod_p, n_seq_tokens, W, w_in, rope_tabs):
    t = x.shape[0]
    tm = 256
    tps = n_seq_tokens // tm
    bm = mod_p.shape[0]
    rope = rope_tabs is not None
    row = lambda i: (i, 0)
    col = lambda i: (0, i)
    rest = [W["w_tail"], W["kv_g"], W["q_g"], W["wq"], W["wqr"], W["wk"], W["wv"], W["sg_g"], W["sg_w"], W["sg_bm"]]
    ins = [x, mod_p, W["norm1_g"], w_in] + rest
    in_specs = [
        pl.BlockSpec((tm, D), row),
        pl.BlockSpec((None, 1, 6 * D), (lambda i: (i // tps, 0, 0)) if bm > 1 else (lambda i: (0, 0, 0))),
        _wspec(W["norm1_g"], l),
        pl.BlockSpec((None, D, MAIN_W), lambda i: (l, 0, 0)),
    ] + [_wspec(a, l) for a in rest]
    out_shape = [
        jax.ShapeDtypeStruct((t, 256), F32), jax.ShapeDtypeStruct((t, 512), F32),
        jax.ShapeDtypeStruct((t, 768), BF16), jax.ShapeDtypeStruct((t, 512), F32),
        jax.ShapeDtypeStruct((512, t), BF16), jax.ShapeDtypeStruct((t, 512), BF16),
        jax.ShapeDtypeStruct((256, t), BF16),
    ]
    out_specs = [pl.BlockSpec((tm, 256), row), pl.BlockSpec((tm, 512), row), pl.BlockSpec((tm, 768), row),
                 pl.BlockSpec((tm, 512), row), pl.BlockSpec((512, tm), col), pl.BlockSpec((tm, 512), row),
                 pl.BlockSpec((256, tm), col)]
    if rope:
        ins += list(rope_tabs)
        in_specs += [pl.BlockSpec((tm, LANES), lambda i: (i % tps, 0))] * 4
    else:
        out_shape += [jax.ShapeDtypeStruct((t, 128), F32), jax.ShapeDtypeStruct((t, 128), F32)]
        out_specs += [pl.BlockSpec((tm, 128), row), pl.BlockSpec((tm, 128), row)]
    return pl.pallas_call(
        functools.partial(_inproj_kernel, rope),
        grid=(t // tm,),
        in_specs=in_specs,
        out_specs=out_specs,
        out_shape=out_shape,
        scratch_shapes=[pltpu.VMEM((D, MAIN_W), BF16)],
        compiler_params=_params(1, "arbitrary"),
        name="in_proj",
    )(*ins)


def _kvcache_kernel(ckv_ref, kpe_ref, wk_ref, wv_ref, k_ref, vt_ref):
    cb = ckv_ref[...].astype(BF16)
    kn = _dot(cb, wk_ref[...])
    kx = kpe_ref[...]
    for hh in range(HEADS):
        sl = slice(hh * LANES, (hh + 1) * LANES)
        k_ref[:, sl] = (kn[:, sl] + kx).astype(BF16)
    vt_ref[...] = _dot(cb, wv_ref[...]).T.astype(BF16)


def _kvcache_call(cache_ckv, kpe128, wk, wv):
    b, depth, m, _ = cache_ckv.shape
    blk = lambda w: pl.BlockSpec((None, None, m, w), lambda l, i: (i, l, 0, 0))
    wblk = lambda a: pl.BlockSpec((None,) + tuple(a.shape[1:]), lambda l, i: (l, 0, 0))
    return pl.pallas_call(
        _kvcache_kernel,
        grid=(depth, b),
        in_specs=[blk(128), blk(128), wblk(wk), wblk(wv)],
        out_specs=[pl.BlockSpec((None, None, m, 512), lambda l, i: (l, i, 0, 0)),
                   pl.BlockSpec((None, None, 256, m), lambda l, i: (l, i, 0, 0))],
        out_shape=[jax.ShapeDtypeStruct((depth, b, m, 512), BF16), jax.ShapeDtypeStruct((depth, b, 256, m), BF16)],
        compiler_params=_params(2),
        name="kv_cache",
    )(cache_ckv, kpe128, wk, wv)


HALO = 16


def _conv_kernel(rb, nblk, cur_ref, prev_ref, next_ref, wdw_ref, bdw_ref, lng_ref, lnb_ref, wpw_ref, o_ref, pad_ref,
                 sh_ref):
    i = pl.program_id(0)
    keep_prev = (i % nblk != 0).astype(F32)
    keep_next = (i % nblk != nblk - 1).astype(F32)

    def glu(blk):
        return blk[:, 0:256] * _sigmoid(blk[:, 256:512])

    pad_ref[0:HALO, :] = glu(prev_ref[...]) * keep_prev
    pad_ref[HALO:HALO + rb, :] = glu(cur_ref[...])
    pad_ref[HALO + rb:2 * HALO + rb, :] = glu(next_ref[...]) * keep_next
    span = rb + 2 * HALO - SUBLANES
    for s in range(SUBLANES):
        sh_ref[s] = pad_ref[s:s + span, :]
    off = HALO - CONV_K // 2
    for j in range(rb // CHUNK):
        acc = jnp.zeros((CHUNK, 256), F32)
        for k in range(CONV_K):
            start = j * CHUNK + (off + k) // SUBLANES * SUBLANES
            acc = acc + sh_ref[(off + k) % SUBLANES, start:start + CHUNK, :] * wdw_ref[k:k + 1, :]
        y = acc + bdw_ref[...]
        mu = jnp.mean(y, axis=-1, keepdims=True)
        dlt = y - mu
        var = jnp.mean(dlt * dlt, axis=-1, keepdims=True)
        z = _silu(dlt * lax.rsqrt(var + EPS) * lng_ref[...] + lnb_ref[...])
        o_ref[j * CHUNK:(j + 1) * CHUNK, :] = _dot(z.astype(BF16), wpw_ref[...])


def _conv_call(l, cv, n_seq_tokens, W):
    t = cv.shape[0]
    rb = min(n_seq_tokens, 512)
    nblk = n_seq_tokens // rb
    per = rb // HALO
    last = t // HALO - 1
    ins = [cv, cv, cv, W["conv_w"], W["conv_b"], W["conv_ln_g"], W["conv_ln_b"], W["conv_pw"]]
    return pl.pallas_call(
        functools.partial(_conv_kernel, rb, nblk),
        grid=(t // rb,),
        in_specs=[
            pl.BlockSpec((rb, 512), lambda i: (i, 0)),
            pl.BlockSpec((HALO, 512), lambda i: (jnp.maximum(i * per - 1, 0), 0)),
            pl.BlockSpec((HALO, 512), lambda i: (jnp.minimum((i + 1) * per, last), 0)),
        ] + [_wspec(a, l) for a in ins[3:]],
        out_specs=pl.BlockSpec((rb, 256), lambda i: (i, 0)),
        out_shape=jax.ShapeDtypeStruct((t, 256), F32),
        scratch_shapes=[pltpu.VMEM((rb + 2 * HALO, 256), F32),
                        pltpu.VMEM((SUBLANES, rb + 2 * HALO - SUBLANES, 256), F32)],
        compiler_params=_params(1),
        name="conv",
    )(*ins)


def _ret_kernel(nc, has_s0, rq_ref, rk_ref, rv_ref, gf_ref, gb_ref, intra_ref, qd_ref, kdt_ref, cd_ref, gng_ref, *rest):
    if has_s0:
        s0_ref, y_ref, sf_ref, sb_ref, s_ref, kvb_ref, of_ref, ob_ref = rest
    else:
        y_ref, sf_ref, sb_ref, s_ref, kvb_ref, of_ref, ob_ref = rest
    same_head = lax.shift_right_logical(_iota((256, 256), 0), 6) == lax.shift_right_logical(_iota((256, 256), 1), 6)
    avg = jnp.where(same_head, 1.0 / HEAD_W, 0.0).astype(BF16)
    row_head = lax.shift_right_logical(_iota((256, CHUNK), 0), 6)
    lane_head = lax.shift_right_logical(_iota((CHUNK, 256), 1), 6)

    def head_mean(x):
        hi, lo = _split_bf16(x)
        return _dot(hi, avg) + _dot(lo, avg)

    def rows_of(cidx):
        return pl.ds(pl.multiple_of(cidx * CHUNK, CHUNK), CHUNK)

    def compact(st):
        return st[:, 0:64] + st[:, 64:128] + st[:, 128:192] + st[:, 192:256]

    s_ref[...] = s0_ref[0] if has_s0 else jnp.zeros((256, 256), F32)

    def fwd_body(cidx, carry):
        rows = rows_of(cidx)
        qb = rq_ref[rows, :]
        vb = rv_ref[rows, :]
        kt = rk_ref[rows, :].astype(F32).T
        kbd = jnp.concatenate([jnp.where(row_head == hh, kt, 0.0).astype(BF16) for hh in range(HEADS)], axis=1)
        s = _dot(qb, kbd)
        p = jnp.concatenate([(s * intra_ref[0]).astype(BF16), (s * intra_ref[1]).astype(BF16)], axis=0)
        zero = jnp.zeros_like(vb)
        vbd = jnp.concatenate([jnp.where(lane_head == hh, vb, zero) for hh in range(HEADS)], axis=0)
        inner = _dot(p, vbd)
        kts = jnp.concatenate([(kt * kdt_ref[0]).astype(BF16), (kt * kdt_ref[1]).astype(BF16)], axis=0)
        kv = _dot(kts, vb)
        kvb_ref[cidx] = jnp.where(same_head, kv[256:512, :], 0.0)
        ob_ref[rows, :] = inner[CHUNK:2 * CHUNK, :]
        st = s_ref[...]
        of_ref[rows, :] = inner[0:CHUNK, :] + _dot(qb, st.astype(BF16)) * qd_ref[0]
        s_ref[...] = cd_ref[0] * st + jnp.where(same_head, kv[0:256, :], 0.0)
        return carry

    lax.fori_loop(0, nc, fwd_body, 0, unroll=2)
    sf_ref[...] = compact(s_ref[...])
    s_ref[...] = s0_ref[1] if has_s0 else jnp.zeros((256, 256), F32)

    def bwd_body(it, carry):
        cidx = nc - 1 - it
        rows = rows_of(cidx)
        st = s_ref[...]
        ob_ref[rows, :] = ob_ref[rows, :] + _dot(rq_ref[rows, :], st.astype(BF16)) * qd_ref[1]
        s_ref[...] = cd_ref[1] * st + kvb_ref[cidx]
        return carry

    lax.fori_loop(0, nc, bwd_body, 0, unroll=2)
    sb_ref[...] = compact(s_ref[...])

    def norm_body(blk, carry):
        rows = pl.ds(pl.multiple_of(blk * 2 * CHUNK, 2 * CHUNK), 2 * CHUNK)
        o = jnp.concatenate([of_ref[rows, :], ob_ref[rows, :]], axis=0)
        dlt = o - head_mean(o)
        nrm = dlt * lax.rsqrt(head_mean(dlt * dlt) + EPS) * gng_ref[...]
        y_ref[rows, :] = (_silu(gf_ref[rows, :]) * nrm[0:2 * CHUNK, :]
                          + _silu(gb_ref[rows, :]) * nrm[2 * CHUNK:4 * CHUNK, :])
        return carry

    lax.fori_loop(0, nc // 2, norm_body, 0)


def _ret_call(l, rqkv, rg, n_seq_tokens, W, s0_bd):
    t = rqkv.shape[0]
    b = t // n_seq_tokens
    nc = n_seq_tokens // CHUNK
    has_s0 = s0_bd is not None
    col = lambda j: pl.BlockSpec((n_seq_tokens, 256), lambda i: (i, j))
    ins = [rqkv] * 3 + [rg] * 2 + [W["ret_intra"], W["ret_qd"], W["ret_kdt"], W["ret_cd"], W["ret_gn_g"]]
    in_specs = [col(0), col(1), col(2), col(0), col(1)] + [_wspec(a, l) for a in ins[5:]]
    if has_s0:
        ins.append(s0_bd)
        in_specs.append(pl.BlockSpec((None, 2, 256, 256), lambda i: (i, 0, 0, 0)))
    st = pl.BlockSpec((None, 256, HEAD_W), lambda i: (i, 0, 0))
    return pl.pallas_call(
        functools.partial(_ret_kernel, nc, has_s0),
        grid=(b,),
        in_specs=in_specs,
        out_specs=[pl.BlockSpec((n_seq_tokens, 256), lambda i: (i, 0)), st, st],
        out_shape=[jax.ShapeDtypeStruct((t, 256), F32), jax.ShapeDtypeStruct((b, 256, HEAD_W), F32),
                   jax.ShapeDtypeStruct((b, 256, HEAD_W), F32)],
        scratch_shapes=[pltpu.VMEM((256, 256), F32), pltpu.VMEM((nc, 256, 256), F32),
                        pltpu.VMEM((n_seq_tokens, 256), F32), pltpu.VMEM((n_seq_tokens, 256), F32)],
        compiler_params=_params(1),
        name="ret",
    )(*ins)


def _attn_kernel(nparts, qt_ref, *refs):
    k_refs = refs[0:2 * nparts:2]
    vt_refs = refs[1:2 * nparts:2]
    o_ref = refs[2 * nparts]
    def scores(hh):
        sl = slice(hh * LANES, (hh + 1) * LANES)
        qth = qt_ref[sl, :]
        return [_dot(k_ref[:, sl], qth) for k_ref in k_refs]

    outs = []
    ss_all = [scores(hh) for hh in range(HEADS)]
    for hh in range(HEADS):
        ss = ss_all[hh]
        m = ss[0].max(axis=0, keepdims=True)
        for s in ss[1:]:
            m = jnp.maximum(m, s.max(axis=0, keepdims=True))
        es = [jnp.exp(s - m) for s in ss]
        den = es[0].sum(axis=0, keepdims=True)
        for e in es[1:]:
            den = den + e.sum(axis=0, keepdims=True)
        vsl = slice(hh * HEAD_W, (hh + 1) * HEAD_W)
        oh = _dot(vt_refs[0][vsl, :], es[0].astype(BF16))
        for e, vt_ref in zip(es[1:], vt_refs[1:]):
            oh = oh + _dot(vt_ref[vsl, :], e.astype(BF16))
        outs.append(oh * (1.0 / den))
    o_ref[...] = jnp.concatenate(outs, axis=0).T


def _attn_call(l, qt, k, vt, cache, n_seq_tokens):
    t = k.shape[0]
    n = n_seq_tokens
    tq = 256
    nq = n // tq
    b = t // n
    ins = [qt]
    in_specs = [pl.BlockSpec((512, tq), lambda i, j: (0, i * nq + j))]
    if cache is not None:
        kc, vtc = cache
        m = kc.shape[2]
        ins += [kc, vtc]
        in_specs += [pl.BlockSpec((None, None, m, 512), lambda i, j: (l, i, 0, 0)),
                     pl.BlockSpec((None, None, 256, m), lambda i, j: (l, i, 0, 0))]
    ins += [k, vt]
    in_specs += [pl.BlockSpec((n, 512), lambda i, j: (i, 0)), pl.BlockSpec((256, n), lambda i, j: (0, i))]
    return pl.pallas_call(
        functools.partial(_attn_kernel, len(ins) // 2),
        grid=(b, nq),
        in_specs=in_specs,
        out_specs=pl.BlockSpec((tq, 256), lambda i, j: (i * nq + j, 0)),
        out_shape=jax.ShapeDtypeStruct((t, 256), F32),
        compiler_params=_params(2),
        name="attn",
    )(*ins)


def _outproj_kernel(ya_ref, yb_ref, yc_ref, yd_ref, x_ref, mod_ref, wo_ref, g2_ref, rhi_ref, rlo_ref,
                    x1_ref, h2_ref, aff_ref, wb_ref):
    @pl.when(pl.program_id(0) == 0)
    def _():
        for rb in range(4):
            wb_ref[rb * 256:(rb + 1) * 256, :] = wo_ref[rb * 256:(rb + 1) * 256, :].astype(BF16)

    mix = _dot(ya_ref[...].astype(BF16), wb_ref[0:256, :])
    mix = mix + _dot(yb_ref[...].astype(BF16), wb_ref[256:512, :])
    mix = mix + _dot(yc_ref[...].astype(BF16), wb_ref[512:768, :])
    mix = mix + _dot(yd_ref[...].astype(BF16), wb_ref[768:1024, :])
    mod = mod_ref[...]
    x1 = x_ref[...] + mod[:, 2 * D:3 * D] * mix
    x1_ref[...] = x1
    h2 = _rms(x1) * g2_ref[...]
    h2 = h2 * (1.0 + mod[:, 4 * D:5 * D]) + mod[:, 3 * D:4 * D]
    h_hi, h_lo = _split_bf16(h2)
    h2_ref[...] = h_hi
    logits = _dot(h_hi, rhi_ref[...]) + _dot(h_lo, rhi_ref[...]) + _dot(h_hi, rlo_ref[...])
    lane = _iota((1, LANES), 1)
    logits = jnp.where(lane < N_EXPERTS, logits, -1e30)
    e = jnp.exp(logits - logits.max(axis=-1, keepdims=True))
    aff_ref[...] = e / e.sum(axis=-1, keepdims=True)


def _outproj_call(l, ys, x, mod_p, n_seq_tokens, W, w_out):
    t = x.shape[0]
    tm = 256
    tps = n_seq_tokens // tm
    bm = mod_p.shape[0]
    row = lambda i: (i, 0)
    ins = list(ys) + [x, mod_p, w_out, W["norm2_g"], W["r_hi"], W["r_lo"]]
    in_specs = [pl.BlockSpec((tm, 256), row)] * 4 + [
        pl.BlockSpec((tm, D), row),
        pl.BlockSpec((None, 1, 6 * D), (lambda i: (i // tps, 0, 0)) if bm > 1 else (lambda i: (0, 0, 0))),
    ] + [_wspec(a, l) for a in ins[6:]]
    return pl.pallas_call(
        _outproj_kernel,
        grid=(t // tm,),
        in_specs=in_specs,
        out_specs=[pl.BlockSpec((tm, D), row), pl.BlockSpec((tm, D), row), pl.BlockSpec((tm, LANES), row)],
        out_shape=[jax.ShapeDtypeStruct((t, D), F32), jax.ShapeDtypeStruct((t, D), BF16),
                   jax.ShapeDtypeStruct((t, LANES), F32)],
        scratch_shapes=[pltpu.VMEM((D, D), BF16)],
        compiler_params=_params(1, "arbitrary"),
        name="out_proj",
    )(*ins)


GROUP_ROWS = 512
CUM_BLK = 256


def _route_kernel(n, cap, nseq, aff_ref, h2_ref, slot_ref, xe_ref, gs_ref, cum_ref, slot_t_ref):
    lane_row = _iota((1, LANES), 1)
    a = aff_ref[0:n, :]
    for s in range(1, nseq):
        a = a + pltpu.roll(aff_ref[s * n:(s + 1) * n, :], N_EXPERTS * s, 1)
    capf = jnp.float32(cap)

    def bisect(_, lohi):
        lo, hi = lohi
        mid = lo + lax.shift_right_logical(hi - lo, 1)
        cnt = jnp.sum(jnp.where(a >= pltpu.bitcast(mid, F32), 1.0, 0.0), axis=0, keepdims=True)
        ok = cnt >= capf
        return jnp.where(ok, mid, lo), jnp.where(ok, hi, mid)

    lo0 = jnp.zeros((1, LANES), jnp.int32)
    hi0 = jnp.full((1, LANES), 0x3F800001, jnp.int32)
    lo, hi = lax.fori_loop(0, 31, bisect, (lo0, hi0))
    gt = a >= pltpu.bitcast(hi, F32)
    eq = (a >= pltpu.bitcast(lo, F32)) & jnp.logical_not(gt)
    need = capf - jnp.sum(jnp.where(gt, 1.0, 0.0), axis=0, keepdims=True)
    tri = jnp.where(_iota((CUM_BLK, CUM_BLK), 0) >= _iota((CUM_BLK, CUM_BLK), 1), 1.0, 0.0).astype(BF16)

    def cumsum_rows(flags):
        carry = jnp.zeros((1, LANES), F32)
        for rb in range(n // CUM_BLK):
            rows = slice(rb * CUM_BLK, (rb + 1) * CUM_BLK)
            part = _dot(tri, flags[rows, :].astype(BF16)) + carry
            cum_ref[rows, :] = part
            carry = part[CUM_BLK - 1:CUM_BLK, :]
        return cum_ref[...]

    eq_rank = cumsum_rows(jnp.where(eq, 1.0, 0.0))
    sel = gt | (eq & (eq_rank <= need))
    pos = cumsum_rows(jnp.where(sel, 1.0, 0.0))
    slot = jnp.where(sel, pos - 1.0, -1.0)
    for s in range(nseq):
        own = slot if s == 0 else pltpu.roll(slot, LANES - N_EXPERTS * s, 1)
        slot_ref[s * n:(s + 1) * n, :] = jnp.where(lane_row < N_EXPERTS, own, -1.0)
    slot_t_ref[...] = slot.T
    a_hi, a_lo = _split_bf16(a)
    a_hilo = jnp.concatenate([a_hi, a_lo], axis=1)
    ones = jnp.ones((LANES, LANES), BF16)
    gexp = GROUP_ROWS // cap
    shift = int(np.log2(cap))
    row_e = lax.shift_right_logical(_iota((GROUP_ROWS, LANES), 0), shift)
    slot_id = _iota((cap, n), 0).astype(F32)
    lane = _iota((GROUP_ROWS, LANES), 1)
    def build_onehot(s, g):
        pieces = []
        for j in range(gexp):
            e_lane = N_EXPERTS * s + g * gexp + j
            mine_row = jnp.broadcast_to(slot_t_ref[e_lane:e_lane + 1, :], (cap, n))
            pieces.append(jnp.where(mine_row == slot_id, 1.0, 0.0).astype(BF16))
        return pieces[0] if gexp == 1 else jnp.concatenate(pieces, axis=0)

    items = [(s, g) for s in range(nseq) for g in range(N_EXPERTS * cap // GROUP_ROWS)]
    onehot_next = build_onehot(*items[0])
    for idx, (s, g) in enumerate(items):
        onehot = onehot_next
        if idx + 1 < len(items):
            onehot_next = build_onehot(*items[idx + 1])
        xe = _dot(onehot, h2_ref[s * n:(s + 1) * n, :]).astype(BF16)
        gboth = _dot(onehot, a_hilo)
        mine = lane == row_e + (g * gexp + N_EXPERTS * s)
        g_hi, g_lo = _split_bf16(jnp.where(mine, gboth[:, 0:LANES] + gboth[:, LANES:2 * LANES], 0.0))
        gsb = _dot(g_hi, ones) + _dot(g_lo, ones)
        for j in range(gexp):
            xe_ref[g * gexp + j, s * cap:(s + 1) * cap, :] = xe[j * cap:(j + 1) * cap, :]
            gs_ref[g * gexp + j, s * cap:(s + 1) * cap, :] = gsb[j * cap:(j + 1) * cap, :]


def _route_call(aff, h2, n_seq_tokens):
    t = aff.shape[0]
    n = n_seq_tokens
    b = t // n
    cap = EC_CAPACITY * n // N_EXPERTS
    nseq = min(b, LANES // N_EXPERTS, max(1, 2048 // n))
    return pl.pallas_call(
        functools.partial(_route_kernel, n, cap, nseq),
        grid=(b // nseq,),
        in_specs=[pl.BlockSpec((nseq * n, LANES), lambda i: (i, 0)), pl.BlockSpec((nseq * n, D), lambda i: (i, 0))],
        out_specs=[pl.BlockSpec((nseq * n, LANES), lambda i: (i, 0)),
                   pl.BlockSpec((N_EXPERTS, nseq * cap, D), lambda i: (0, i, 0)),
                   pl.BlockSpec((N_EXPERTS, nseq * cap, LANES), lambda i: (0, i, 0))],
        out_shape=[jax.ShapeDtypeStruct((t, LANES), F32),
                   jax.ShapeDtypeStruct((N_EXPERTS, b * cap, D), BF16),
                   jax.ShapeDtypeStruct((N_EXPERTS, b * cap, LANES), F32)],
        scratch_shapes=[pltpu.VMEM((n, LANES), F32), pltpu.VMEM((LANES, n), F32)],
        compiler_params=_params(1),
        name="route",
    )(aff, h2)


FF_TILE = 512


def _ffn_kernel(xc_ref, xl_ref, gc_ref, gl_ref, wg_ref, wu_ref, wd_ref, yc_ref, yl_ref, accc_ref, accl_ref):
    f = pl.program_id(1)
    wg = wg_ref[...].astype(BF16)
    wu = wu_ref[...].astype(BF16)
    wd = wd_ref[...].astype(BF16)
    paths = ((xc_ref, gc_ref, yc_ref, accc_ref), (xl_ref, gl_ref, yl_ref, accl_ref))
    gate_up = [(_dot(x_ref[...], wg), _dot(x_ref[...], wu)) for x_ref, _, _, _ in paths]
    for (x_ref, g_ref, y_ref, acc_ref), (a, up) in zip(paths, gate_up):
        hmid = (_silu(a) * up).astype(BF16)
        contrib = _dot(hmid, wd)

        @pl.when(f == 0)
        def _():
            acc_ref[...] = contrib

        @pl.when(f == FF // FF_TILE - 1)
        def _():
            gate = jnp.concatenate([g_ref[...]] * (D // LANES), axis=1)
            y_ref[...] = ((acc_ref[...] + contrib) * gate).astype(BF16)


def _ffn_call(l, xe_c, xe_l, gs_c, gs_l, w_gate, w_up, w_down):
    rc = xe_c.shape[1]
    rl = xe_l.shape[1]
    ex = lambda r, w: pl.BlockSpec((None, r, w), lambda e, f: (e, 0, 0))
    return pl.pallas_call(
        _ffn_kernel,
        grid=(N_EXPERTS, FF // FF_TILE),
        in_specs=[
            ex(rc, D), ex(rl, D), ex(rc, LANES), ex(rl, LANES),
            pl.BlockSpec((None, None, D, FF_TILE), lambda e, f: (l, e, 0, f)),
            pl.BlockSpec((None, None, D, FF_TILE), lambda e, f: (l, e, 0, f)),
            pl.BlockSpec((None, None, FF_TILE, D), lambda e, f: (l, e, f, 0)),
        ],
        out_specs=[ex(rc, D), ex(rl, D)],
        out_shape=[jax.ShapeDtypeStruct((N_EXPERTS, rc, D), BF16), jax.ShapeDtypeStruct((N_EXPERTS, rl, D), BF16)],
        scratch_shapes=[pltpu.VMEM((rc, D), F32), pltpu.VMEM((rl, D), F32)],
        compiler_params=pltpu.CompilerParams(dimension_semantics=("parallel", "arbitrary"),
                                             vmem_limit_bytes=VMEM_LIMIT),
        name="ffn",
    )(xe_c, xe_l, gs_c, gs_l, w_gate, w_up, w_down)


COMB_ROWS = 256


def _combine_kernel(cap, final, ye_ref, slot_ref, x1_ref, mod_ref, *rest):
    if final:
        fg_ref, x2_ref, yf_ref = rest
    else:
        (x2_ref,) = rest
    gexp = GROUP_ROWS // cap
    shift = int(np.log2(cap))
    col_e = lax.shift_right_logical(_iota((LANES, GROUP_ROWS), 1), shift)
    col_s = (_iota((1, GROUP_ROWS), 1) & (cap - 1)).astype(F32)
    lane_e = _iota((LANES, GROUP_ROWS), 0)
    sb = slot_ref[...].astype(BF16)

    def build_onehot_t(g):
        expand_m = jnp.where(lane_e == col_e + g * gexp, 1.0, 0.0).astype(BF16)
        return jnp.where(_dot(sb, expand_m) == col_s, 1.0, 0.0).astype(BF16)

    ngroups = N_EXPERTS * cap // GROUP_ROWS
    acc = jnp.zeros((COMB_ROWS, D), F32)
    onehot_next = build_onehot_t(0)
    for g in range(ngroups):
        onehot_t = onehot_next
        if g + 1 < ngroups:
            onehot_next = build_onehot_t(g + 1)
        if cap % LANES == 0:
            for j in range(gexp):
                acc = acc + _dot(onehot_t[:, j * cap:(j + 1) * cap], ye_ref[g * gexp + j])
        else:
            ye = jnp.concatenate([ye_ref[g * gexp + j] for j in range(gexp)], axis=0)
            acc = acc + _dot(onehot_t, ye)
    mod = mod_ref[...]
    x2 = x1_ref[...] + mod[:, 5 * D:6 * D] * acc
    x2_ref[...] = x2
    if final:
        yf_ref[...] = _rms(x2) * fg_ref[...]


def _combine_call(ye, slot, x1, mod_p, n_seq_tokens, final_g):
    t = x1.shape[0]
    n = n_seq_tokens
    b = t // n
    nr = n // COMB_ROWS
    cap = EC_CAPACITY * n // N_EXPERTS
    bm = mod_p.shape[0]
    final = final_g is not None
    row = lambda i, j: (i * nr + j, 0)
    ins = [ye, slot, x1, mod_p]
    in_specs = [
        pl.BlockSpec((N_EXPERTS, cap, D), lambda i, j: (0, i, 0)),
        pl.BlockSpec((COMB_ROWS, LANES), row),
        pl.BlockSpec((COMB_ROWS, D), row),
        pl.BlockSpec((None, 1, 6 * D), (lambda i, j: (i, 0, 0)) if bm > 1 else (lambda i, j: (0, 0, 0))),
    ]
    out_shape = [jax.ShapeDtypeStruct((t, D), F32)]
    out_specs = [pl.BlockSpec((COMB_ROWS, D), row)]
    if final:
        ins.append(final_g)
        in_specs.append(pl.BlockSpec((1, D), lambda i, j: (0, 0)))
        out_shape.append(jax.ShapeDtypeStruct((t, D), F32))
        out_specs.append(pl.BlockSpec((COMB_ROWS, D), row))
    return pl.pallas_call(
        functools.partial(_combine_kernel, cap, final),
        grid=(b, nr),
        in_specs=in_specs,
        out_specs=out_specs,
        out_shape=out_shape,
        compiler_params=_params(2),
        name="combine",
    )(*ins)


def _rot_cols(w):
    a, b, c, d = (w[..., 8 * i:8 * (i + 1)] for i in range(4))
    return jnp.concatenate([-b, a, -d, c], axis=-1)


def _rope_tables(n):
    rows = n // GRID_W
    row = jnp.repeat(jnp.arange(rows, dtype=F32), GRID_W)
    col = jnp.tile(jnp.arange(GRID_W, dtype=F32), rows)
    inv = ROPE_THETA ** (-jnp.arange(0, ROPE // 2, 2, dtype=F32) / (ROPE // 2))
    ra = row[:, None] * inv
    ca = col[:, None] * inv
    ang = jnp.concatenate([ra, ra, ca, ca], axis=-1)
    cos, sin = jnp.cos(ang), jnp.sin(ang)
    z32, z64, z96 = (jnp.zeros((n, w), F32) for w in (32, 64, 96))
    cq = jnp.concatenate([jnp.ones((n, 64), F32), cos, z32], axis=1)
    sq = jnp.concatenate([z64, sin, z32], axis=1)
    ck = jnp.concatenate([z64, cos, z32], axis=1)
    sk = jnp.concatenate([z96, sin], axis=1)
    return cq, sq, ck, sk


def _retention_tables(p_f, p_b):
    pos = jnp.arange(CHUNK, dtype=F32)
    diff = pos[:, None] - pos[None, :]

    def one(p, backward):
        lg = jnp.log1p(-jnp.exp2(p.astype(F32)))[:, :, None]
        dd = -diff if backward else diff
        intra = jnp.where(dd >= 0, jnp.exp(jnp.maximum(dd, 0.0) * lg[..., None]), 0.0)
        qexp = (CHUNK - pos) if backward else (pos + 1.0)
        kexp = pos if backward else (CHUNK - 1.0 - pos)
        qd = jnp.exp(qexp * lg)
        kd = jnp.exp(kexp * lg)
        cd = jnp.exp(CHUNK * lg)
        nl = p.shape[0]
        intra_w = jnp.swapaxes(intra, 1, 2).reshape(nl, CHUNK, HEADS * CHUNK)
        qd_w = jnp.repeat(jnp.swapaxes(qd, 1, 2), HEAD_W, axis=2)
        kd_t = jnp.repeat(kd, HEAD_W, axis=1)
        cdw = jnp.broadcast_to(jnp.repeat(cd, HEAD_W, axis=1), (nl, 256, 256))
        return intra_w, qd_w, kd_t, cdw

    f = one(p_f, False)
    b = one(p_b, True)
    return tuple(jnp.stack([x, y], axis=1) for x, y in zip(f, b))


def _prepare(norm1_g, w_in, sg_norm_g, sg_w, sg_b, conv_w, conv_b, conv_ln_g, conv_ln_b, conv_pw, ret_decay_f,
             ret_decay_b, ret_gn_g, q_norm_g, w_uq, kv_norm_g, w_ukv, norm2_g, router):
    L = DEPTH
    row = lambda a: a.reshape(L, 1, -1)
    tail_src = w_in[:, :, MAIN_W:]
    kpe_cols = tail_src[:, :, Q_LORA + KV_LORA:]
    w_tail = jnp.concatenate([tail_src[:, :, Q_LORA:Q_LORA + KV_LORA], tail_src[:, :, :Q_LORA], kpe_cols,
                              _rot_cols(kpe_cols)], axis=2)
    uq = w_uq.reshape(L, Q_LORA, HEADS, HEAD_W + ROPE)
    pad_rows = lambda a: jnp.pad(a, ((0, 0), (0, 256 - Q_LORA), (0, 0)))
    wq = pad_rows(jnp.pad(uq, ((0, 0), (0, 0), (0, 0), (0, LANES - HEAD_W - ROPE))).reshape(L, Q_LORA, 512))
    uq_rot = jnp.pad(_rot_cols(uq[..., HEAD_W:]), ((0, 0), (0, 0), (0, 0), (HEAD_W, LANES - HEAD_W - ROPE)))
    wqr = pad_rows(uq_rot.reshape(L, Q_LORA, 512))
    ukv = w_ukv.reshape(L, KV_LORA, HEADS, 2 * HEAD_W)
    wk = jnp.pad(ukv[..., :HEAD_W], ((0, 0), (0, 0), (0, 0), (0, LANES - HEAD_W))).reshape(L, KV_LORA, 512)
    wv = ukv[..., HEAD_W:].reshape(L, KV_LORA, 256)
    intra, qd, kd, cd = _retention_tables(ret_decay_f, ret_decay_b)
    r_pad = jnp.pad(router, ((0, 0), (0, 0), (0, LANES - N_EXPERTS)))
    r_hi = r_pad.astype(BF16)
    r_lo = (r_pad - r_hi.astype(F32)).astype(BF16)
    return dict(
        norm1_g=row(norm1_g), w_tail=w_tail.astype(BF16), kv_g=row(kv_norm_g),
        q_g=jnp.pad(row(q_norm_g), ((0, 0), (0, 0), (0, 256 - Q_LORA))),
        wq=wq.astype(BF16), wqr=wqr.astype(BF16), wk=wk.astype(BF16), wv=wv.astype(BF16),
        sg_g=row(sg_norm_g), sg_w=sg_w.astype(BF16), sg_bm=jnp.repeat(jnp.swapaxes(sg_b, 1, 2), HEAD_W, axis=2),
        conv_w=jnp.pad(conv_w, ((0, 0), (0, 1), (0, 0))), conv_b=row(conv_b), conv_ln_g=row(conv_ln_g),
        conv_ln_b=row(conv_ln_b), conv_pw=conv_pw.astype(BF16),
        ret_intra=intra, ret_qd=qd, ret_kdt=kd, ret_cd=cd, ret_gn_g=row(ret_gn_g),
        norm2_g=row(norm2_g), r_hi=r_hi, r_lo=r_lo,
    )


def _block_diag_states(state):
    eye = jnp.eye(HEADS, dtype=state.dtype)
    bd = state[:, :, :, :, None, :] * eye[None, None, :, None, :, None]
    return bd.reshape(state.shape[0], 2, HEADS * HEAD_W, HEADS * HEAD_W)


def _mixers(l, x, mod_p, n, W, w_in, w_out, rope_tabs, s0_bd, cache):
    outs = _inproj_call(l, x, mod_p, n, W, w_in, rope_tabs)
    ya, cv, rqkv, rg, qt, k, vt = outs[:7]
    yb = _conv_call(l, cv, n, W)
    yc, s_f, s_b = _ret_call(l, rqkv, rg, n, W, s0_bd)
    yd = _attn_call(l, qt, k, vt, cache, n)
    x1, h2, aff = _outproj_call(l, (ya, yb, yc, yd), x, mod_p, n, W, w_out)
    slot, xe, gs = _route_call(aff, h2, n)
    return x1, slot, xe, gs, outs[7:], (s_f, s_b)


def kernel(x_prompt, x_sample, cache_mla_ckv, cache_mla_kpe, state_ret, c, c_ctx, w_mod, b_mod, norm1_g, w_in, sg_norm_g, sg_w, sg_b, conv_w, conv_b, conv_ln_g, conv_ln_b, conv_pw, ret_decay_f, ret_decay_b, ret_gn_g, q_norm_g, w_uq, kv_norm_g, w_ukv, w_out, norm2_g, router, w_gate, w_up, w_down, final_norm_g):
    bc, nc_tok, _ = x_prompt.shape
    bl, nl_tok, _ = x_sample.shape
    W = _prepare(norm1_g, w_in, sg_norm_g, sg_w, sg_b, conv_w, conv_b, conv_ln_g, conv_ln_b, conv_pw, ret_decay_f,
                 ret_decay_b, ret_gn_g, q_norm_g, w_uq, kv_norm_g, w_ukv, norm2_g, router)
    cvec = jnp.zeros((8, D), F32).at[0].set(c_ctx).at[1:1 + bl].set(c)
    mod_all = _mod_call(cvec, w_mod, b_mod)
    rope_tabs = _rope_tables(nl_tok)
    kpe128 = jnp.pad(cache_mla_kpe, ((0, 0), (0, 0), (0, 0), (HEAD_W, LANES - HEAD_W - ROPE)))
    cache = _kvcache_call(cache_mla_ckv, kpe128, W["wk"], W["wv"])
    s0_all = _block_diag_states(jnp.swapaxes(state_ret, 0, 1).reshape(DEPTH * bl, 2, HEADS, HEAD_W, HEAD_W))
    s0_all = s0_all.reshape(DEPTH, bl, 2, 256, 256)
    fg = final_norm_g.reshape(1, D)
    xp = x_prompt.reshape(bc * nc_tok, D)
    xs = x_sample.reshape(bl * nl_tok, D)
    ckvs, kpes, rets = [], [], []
    yp = ys = None
    for l in range(DEPTH):
        mod_c = mod_all[l, 0:1].reshape(1, 1, 6 * D)
        mod_l = mod_all[l, 1:1 + bl].reshape(bl, 1, 6 * D)
        x1c, slot_c, xe_c, gs_c, (ckv, slab), (s_f, s_b) = _mixers(l, xp, mod_c, nc_tok, W, w_in, w_out, None, None,
                                                                 None)
        x1l, slot_l, xe_l, gs_l, _, _ = _mixers(l, xs, mod_l, nl_tok, W, w_in, w_out, rope_tabs, s0_all[l], cache)
        ye_c, ye_l = _ffn_call(l, xe_c, xe_l, gs_c, gs_l, w_gate, w_up, w_down)
        last = l == DEPTH - 1
        res_c = _combine_call(ye_c, slot_c, x1c, mod_c, nc_tok, fg if last else None)
        res_l = _combine_call(ye_l, slot_l, x1l, mod_l, nl_tok, fg if last else None)
        xp, xs = res_c[0], res_l[0]
        if last:
            yp, ys = res_c[1], res_l[1]
        ckvs.append(ckv.reshape(bc, nc_tok, KV_LORA))
        kpes.append(slab[:, HEAD_W:HEAD_W + ROPE].reshape(bc, nc_tok, ROPE))
        rets.append(jnp.stack([s_f, s_b], axis=1).reshape(bc, 2, HEADS, HEAD_W, HEAD_W))
    return (yp.reshape(bc, nc_tok, D), ys.reshape(bl, nl_tok, D), jnp.stack(ckvs, axis=1), jnp.stack(kpes, axis=1),
            jnp.stack(rets, axis=1))
```

```python
import functools

import jax
import jax.numpy as jnp
import numpy as np
from jax import lax
from jax.experimental import pallas as pl
from jax.experimental.pallas import tpu as pltpu

F32 = jnp.float32
BF16 = jnp.bfloat16

D = 1024
DEPTH = 4
CHUNK = 128
EPS = 1e-6
GRID_W = 64
CONV_K = 31
HEADS = 4
HEAD_W = 64
Q_LORA = 192
KV_LORA = 128
ROPE = 32
ROPE_THETA = 10000.0
N_EXPERTS = 16
FF = 1024
EC_CAPACITY = 2
MAIN_W = 2304
TAIL_W = 384
ATT_SCALE = (HEAD_W + ROPE) ** -0.5
LANES = 128
SUBLANES = 8
VMEM_LIMIT = 56 * 1024 * 1024


def _iota(shape, dim):
    return lax.broadcasted_iota(jnp.int32, shape, dim)


def _dot(a, b):
    return jnp.dot(a, b, preferred_element_type=F32)


def _split_bf16(x):
    hi = x.astype(BF16)
    lo = (x - hi.astype(F32)).astype(BF16)
    return hi, lo


def _sigmoid(x):
    return 1.0 / (1.0 + jnp.exp(-x))


def _silu(x):
    return x * _sigmoid(x)


def _rms(x):
    return x * lax.rsqrt(jnp.mean(x * x, axis=-1, keepdims=True) + EPS)


def _wspec(arr, l):
    nd = arr.ndim
    return pl.BlockSpec((None,) + tuple(arr.shape[1:]), lambda *_: (l,) + (0,) * (nd - 1))


def _params(n_axes, sem="parallel"):
    return pltpu.CompilerParams(dimension_semantics=(sem,) * n_axes, vmem_limit_bytes=VMEM_LIMIT)


def _mod_kernel(c_ref, w_ref, b_ref, o_ref):
    cv = c_ref[...]
    s_hi, s_lo = _split_bf16(_silu(cv))
    w_hi, w_lo = _split_bf16(w_ref[...])
    o_ref[...] = _dot(s_hi, w_hi) + _dot(s_lo, w_hi) + _dot(s_hi, w_lo) + b_ref[...]


def _mod_call(cvec, w_mod, b_mod):
    nt = 1536
    return pl.pallas_call(
        _mod_kernel,
        grid=(DEPTH, 6 * D // nt),
        in_specs=[
            pl.BlockSpec((8, D), lambda l, j: (0, 0)),
            pl.BlockSpec((None, D, nt), lambda l, j: (l, 0, j)),
            pl.BlockSpec((None, 1, nt), lambda l, j: (l, 0, j)),
        ],
        out_specs=pl.BlockSpec((None, 8, nt), lambda l, j: (l, 0, j)),
        out_shape=jax.ShapeDtypeStruct((DEPTH, 8, 6 * D), F32),
        compiler_params=_params(2),
        name="mod",
    )(cvec, w_mod, b_mod.reshape(DEPTH, 1, 6 * D))


N_MAIN_BLOCKS = MAIN_W // 256
RET_K_BLOCK = 5


def _inproj_kernel(rope, x_ref, mod_ref, g1_ref, w_ref, wt_ref, kvg_ref, qg_ref, wq_ref, wqr_ref, wk_ref, wv_ref,
                   sgg_ref, sgw_ref, sgb_ref, *rest):
    if rope:
        cq_ref, sq_ref, ck_ref, sk_ref, ya_ref, cv_ref, rqkv_ref, rg_ref, qt_ref, k_ref, vt_ref, wb_ref = rest
    else:
        ya_ref, cv_ref, rqkv_ref, rg_ref, qt_ref, k_ref, vt_ref, ckv_ref, slab_ref, wb_ref = rest

    @pl.when(pl.program_id(0) == 0)
    def _():
        for cb in range(N_MAIN_BLOCKS):
            blk = w_ref[:, cb * 256:(cb + 1) * 256]
            if cb == RET_K_BLOCK:
                blk = blk * (HEAD_W ** -0.5)
            wb_ref[:, cb * 256:(cb + 1) * 256] = blk.astype(BF16)

    x = x_ref[...]
    mod = mod_ref[...]
    h = _rms(x) * g1_ref[...]
    h = h * (1.0 + mod[:, D:2 * D]) + mod[:, 0:D]
    hb = h.astype(BF16)
    proj = lambda cb: _dot(hb, wb_ref[:, cb * 256:(cb + 1) * 256])
    v_gate = proj(1)
    tail = _dot(hb, wt_ref[...])
    u = proj(0)
    cv_ref[:, 0:256] = proj(2)
    cv_ref[:, 256:512] = proj(3)
    for j in range(3):
        rqkv_ref[:, j * 256:(j + 1) * 256] = proj(4 + j).astype(BF16)
    rg_ref[:, 0:256] = proj(7)
    rg_ref[:, 256:512] = proj(8)
    vn = _rms(v_gate) * sgg_ref[...]
    group = lax.shift_right_logical(_iota((1, 256), 1), 6)
    for cidx in range(x.shape[0] // CHUNK):
        rows = slice(cidx * CHUNK, (cidx + 1) * CHUNK)
        acc = jnp.zeros((CHUNK, 256), F32)
        for g in range(HEADS):
            acc = acc + _dot(sgw_ref[g], jnp.where(group == g, vn[rows, :], 0.0).astype(BF16))
        ya_ref[rows, :] = u[rows, :] * (acc + sgb_ref[...])
    ckv_n = _rms(tail[:, 0:128]) * kvg_ref[...]
    c256 = tail[:, 128:384]
    lane256 = _iota((1, 256), 1)
    ms = jnp.sum(jnp.where(lane256 < Q_LORA, c256 * c256, 0.0), axis=-1, keepdims=True) * (1.0 / Q_LORA)
    cqn = (c256 * lax.rsqrt(ms + EPS) * qg_ref[...]).astype(BF16)
    ckvb = ckv_n.astype(BF16)
    q = _dot(cqn, wq_ref[...])
    kn = _dot(ckvb, wk_ref[...])
    vt_ref[...] = _dot(ckvb, wv_ref[...]).T.astype(BF16)
    slab = tail[:, 256:384]
    if rope:
        qr = _dot(cqn, wqr_ref[...])
        kx = slab * ck_ref[...] + pltpu.roll(slab * sk_ref[...], LANES - ROPE, 1)
        q = q * jnp.concatenate([cq_ref[...]] * HEADS, axis=1) + qr * jnp.concatenate([sq_ref[...]] * HEADS, axis=1)
    else:
        lane128 = _iota((1, LANES), 1)
        kx = jnp.where((lane128 >= HEAD_W) & (lane128 < HEAD_W + ROPE), slab, 0.0)
        ckv_ref[...] = ckv_n
        slab_ref[...] = slab
    qt_ref[...] = (q * ATT_SCALE).T.astype(BF16)
    for hh in range(HEADS):
        sl = slice(hh * LANES, (hh + 1) * LANES)
        k_ref[:, sl] = (kn[:, sl] + kx).astype(BF16)


def _inproj_call(l, x, mod_p, n_seq_tokens, W, w_in, rope_tabs):
    t = x.shape[0]
    tm = 256
    tps = n_seq_tokens // tm
    bm = mod_p.shape[0]
    rope = rope_tabs is not None
    row = lambda i: (i, 0)
    col = lambda i: (0, i)
    rest = [W["w_tail"], W["kv_g"], W["q_g"], W["wq"], W["wqr"], W["wk"], W["wv"], W["sg_g"], W["sg_w"], W["sg_bm"]]
    ins = [x, mod_p, W["norm1_g"], w_in] + rest
    in_specs = [
        pl.BlockSpec((tm, D), row),
        pl.BlockSpec((None, 1, 6 * D), (lambda i: (i // tps, 0, 0)) if bm > 1 else (lambda i: (0, 0, 0))),
        _wspec(W["norm1_g"], l),
        pl.BlockSpec((None, D, MAIN_W), lambda i: (l, 0, 0)),
    ] + [_wspec(a, l) for a in rest]
    out_shape = [
        jax.ShapeDtypeStruct((t, 256), F32), jax.ShapeDtypeStruct((t, 512), F32),
        jax.ShapeDtypeStruct((t, 768), BF16), jax.ShapeDtypeStruct((t, 512), F32),
        jax.ShapeDtypeStruct((512, t), BF16), jax.ShapeDtypeStruct((t, 512), BF16),
        jax.ShapeDtypeStruct((256, t), BF16),
    ]
    out_specs = [pl.BlockSpec((tm, 256), row), pl.BlockSpec((tm, 512), row), pl.BlockSpec((tm, 768), row),
                 pl.BlockSpec((tm, 512), row), pl.BlockSpec((512, tm), col), pl.BlockSpec((tm, 512), row),
                 pl.BlockSpec((256, tm), col)]
    if rope:
        ins += list(rope_tabs)
        in_specs += [pl.BlockSpec((tm, LANES), lambda i: (i % tps, 0))] * 4
    else:
        out_shape += [jax.ShapeDtypeStruct((t, 128), F32), jax.ShapeDtypeStruct((t, 128), F32)]
        out_specs += [pl.BlockSpec((tm, 128), row), pl.BlockSpec((tm, 128), row)]
    return pl.pallas_call(
        functools.partial(_inproj_kernel, rope),
        grid=(t // tm,),
        in_specs=in_specs,
        out_specs=out_specs,
        out_shape=out_shape,
        scratch_shapes=[pltpu.VMEM((D, MAIN_W), BF16)],
        compiler_params=_params(1, "arbitrary"),
        name="in_proj",
    )(*ins)


def _kvcache_kernel(ckv_ref, kpe_ref, wk_ref, wv_ref, k_ref, vt_ref):
    cb = ckv_ref[...].astype(BF16)
    kn = _dot(cb, wk_ref[...])
    kx = kpe_ref[...]
    for hh in range(HEADS):
        sl = slice(hh * LANES, (hh + 1) * LANES)
        k_ref[:, sl] = (kn[:, sl] + kx).astype(BF16)
    vt_ref[...] = _dot(cb, wv_ref[...]).T.astype(BF16)


def _kvcache_call(cache_ckv, kpe128, wk, wv):
    b, depth, m, _ = cache_ckv.shape
    blk = lambda w: pl.BlockSpec((None, None, m, w), lambda l, i: (i, l, 0, 0))
    wblk = lambda a: pl.BlockSpec((None,) + tuple(a.shape[1:]), lambda l, i: (l, 0, 0))
    return pl.pallas_call(
        _kvcache_kernel,
        grid=(depth, b),
        in_specs=[blk(128), blk(128), wblk(wk), wblk(wv)],
        out_specs=[pl.BlockSpec((None, None, m, 512), lambda l, i: (l, i, 0, 0)),
                   pl.BlockSpec((None, None, 256, m), lambda l, i: (l, i, 0, 0))],
        out_shape=[jax.ShapeDtypeStruct((depth, b, m, 512), BF16), jax.ShapeDtypeStruct((depth, b, 256, m), BF16)],
        compiler_params=_params(2),
        name="kv_cache",
    )(cache_ckv, kpe128, wk, wv)


HALO = 16


def _conv_kernel(rb, nblk, cur_ref, prev_ref, next_ref, wdw_ref, bdw_ref, lng_ref, lnb_ref, wpw_ref, o_ref, pad_ref,
                 sh_ref):
    i = pl.program_id(0)
    keep_prev = (i % nblk != 0).astype(F32)
    keep_next = (i % nblk != nblk - 1).astype(F32)

    def glu(blk):
        return blk[:, 0:256] * _sigmoid(blk[:, 256:512])

    pad_ref[0:HALO, :] = glu(prev_ref[...]) * keep_prev
    pad_ref[HALO:HALO + rb, :] = glu(cur_ref[...])
    pad_ref[HALO + rb:2 * HALO + rb, :] = glu(next_ref[...]) * keep_next
    span = rb + 2 * HALO - SUBLANES
    for s in range(SUBLANES):
        sh_ref[s] = pad_ref[s:s + span, :]
    off = HALO - CONV_K // 2
    for j in range(rb // CHUNK):
        acc = jnp.zeros((CHUNK, 256), F32)
        for k in range(CONV_K):
            start = j * CHUNK + (off + k) // SUBLANES * SUBLANES
            acc = acc + sh_ref[(off + k) % SUBLANES, start:start + CHUNK, :] * wdw_ref[k:k + 1, :]
        y = acc + bdw_ref[...]
        mu = jnp.mean(y, axis=-1, keepdims=True)
        dlt = y - mu
        var = jnp.mean(dlt * dlt, axis=-1, keepdims=True)
        z = _silu(dlt * lax.rsqrt(var + EPS) * lng_ref[...] + lnb_ref[...])
        o_ref[j * CHUNK:(j + 1) * CHUNK, :] = _dot(z.astype(BF16), wpw_ref[...])


def _conv_call(l, cv, n_seq_tokens, W):
    t = cv.shape[0]
    rb = min(n_seq_tokens, 512)
    nblk = n_seq_tokens // rb
    per = rb // HALO
    last = t // HALO - 1
    ins = [cv, cv, cv, W["conv_w"], W["conv_b"], W["conv_ln_g"], W["conv_ln_b"], W["conv_pw"]]
    return pl.pallas_call(
        functools.partial(_conv_kernel, rb, nblk),
        grid=(t // rb,),
        in_specs=[
            pl.BlockSpec((rb, 512), lambda i: (i, 0)),
            pl.BlockSpec((HALO, 512), lambda i: (jnp.maximum(i * per - 1, 0), 0)),
            pl.BlockSpec((HALO, 512), lambda i: (jnp.minimum((i + 1) * per, last), 0)),
        ] + [_wspec(a, l) for a in ins[3:]],
        out_specs=pl.BlockSpec((rb, 256), lambda i: (i, 0)),
        out_shape=jax.ShapeDtypeStruct((t, 256), F32),
        scratch_shapes=[pltpu.VMEM((rb + 2 * HALO, 256), F32),
                        pltpu.VMEM((SUBLANES, rb + 2 * HALO - SUBLANES, 256), F32)],
        compiler_params=_params(1),
        name="conv",
    )(*ins)


def _ret_kernel(nb, nc, has_s0, rq_ref, rk_ref, rv_ref, gf_ref, gb_ref, intra_ref, qd_ref, kdt_ref, cd_ref, gng_ref,
                *rest):
    if has_s0:
        s0_ref, y_ref, sf_ref, sb_ref, s_ref, kvb_ref, of_ref, ob_ref = rest
    else:
        y_ref, sf_ref, sb_ref, s_ref, kvb_ref, of_ref, ob_ref = rest
    same_head = lax.shift_right_logical(_iota((256, 256), 0), 6) == lax.shift_right_logical(_iota((256, 256), 1), 6)
    avg = jnp.where(same_head, 1.0 / HEAD_W, 0.0).astype(BF16)
    row_head = lax.shift_right_logical(_iota((256, CHUNK), 0), 6)
    lane_head = lax.shift_right_logical(_iota((CHUNK, 256), 1), 6)

    def head_mean(x):
        hi, lo = _split_bf16(x)
        return _dot(hi, avg) + _dot(lo, avg)

    def rows_of(sq, cidx):
        return pl.ds(pl.multiple_of(sq * nc * CHUNK + cidx * CHUNK, CHUNK), CHUNK)

    def compact(st):
        return st[:, 0:64] + st[:, 64:128] + st[:, 128:192] + st[:, 192:256]

    for sq in range(nb):
        s_ref[sq] = s0_ref[sq, 0] if has_s0 else jnp.zeros((256, 256), F32)

    def fwd_one(sq, cidx):
        rows = rows_of(sq, cidx)
        qb = rq_ref[rows, :]
        vb = rv_ref[rows, :]
        kt = rk_ref[rows, :].astype(F32).T
        kbd = jnp.concatenate([jnp.where(row_head == hh, kt, 0.0).astype(BF16) for hh in range(HEADS)], axis=1)
        s = _dot(qb, kbd)
        p = jnp.concatenate([(s * intra_ref[0]).astype(BF16), (s * intra_ref[1]).astype(BF16)], axis=0)
        zero = jnp.zeros_like(vb)
        vbd = jnp.concatenate([jnp.where(lane_head == hh, vb, zero) for hh in range(HEADS)], axis=0)
        inner = _dot(p, vbd)
        kts = jnp.concatenate([(kt * kdt_ref[0]).astype(BF16), (kt * kdt_ref[1]).astype(BF16)], axis=0)
        kv = _dot(kts, vb)
        kvb_ref[sq * nc + cidx] = jnp.where(same_head, kv[256:512, :], 0.0)
        ob_ref[rows, :] = inner[CHUNK:2 * CHUNK, :]
        st = s_ref[sq]
        of_ref[rows, :] = inner[0:CHUNK, :] + _dot(qb, st.astype(BF16)) * qd_ref[0]
        s_ref[sq] = cd_ref[0] * st + jnp.where(same_head, kv[0:256, :], 0.0)

    def fwd_body(cidx, carry):
        for sq in range(nb):
            fwd_one(sq, cidx)
        return carry

    lax.fori_loop(0, nc, fwd_body, 0, unroll=2)
    for sq in range(nb):
        sf_ref[sq] = compact(s_ref[sq])
        s_ref[sq] = s0_ref[sq, 1] if has_s0 else jnp.zeros((256, 256), F32)

    def bwd_body(it, carry):
        cidx = nc - 1 - it
        for sq in range(nb):
            rows = rows_of(sq, cidx)
            st = s_ref[sq]
            ob_ref[rows, :] = ob_ref[rows, :] + _dot(rq_ref[rows, :], st.astype(BF16)) * qd_ref[1]
            s_ref[sq] = cd_ref[1] * st + kvb_ref[sq * nc + cidx]
        return carry

    lax.fori_loop(0, nc, bwd_body, 0, unroll=2)
    for sq in range(nb):
        sb_ref[sq] = compact(s_ref[sq])

    def norm_body(blk, carry):
        rows = pl.ds(pl.multiple_of(blk * 2 * CHUNK, 2 * CHUNK), 2 * CHUNK)
        o = jnp.concatenate([of_ref[rows, :], ob_ref[rows, :]], axis=0)
        dlt = o - head_mean(o)
        nrm = dlt * lax.rsqrt(head_mean(dlt * dlt) + EPS) * gng_ref[...]
        y_ref[rows, :] = (_silu(gf_ref[rows, :]) * nrm[0:2 * CHUNK, :]
                          + _silu(gb_ref[rows, :]) * nrm[2 * CHUNK:4 * CHUNK, :])
        return carry

    lax.fori_loop(0, nb * nc // 2, norm_body, 0, unroll=2)


RET_BLOCK_ROWS = 4096


def _ret_call(l, rqkv, rg, n_seq_tokens, W, s0_bd):
    t = rqkv.shape[0]
    n = n_seq_tokens
    b = t // n
    nc = n // CHUNK
    nb = min(b, max(1, RET_BLOCK_ROWS // n), 4)
    has_s0 = s0_bd is not None
    col = lambda j: pl.BlockSpec((nb * n, 256), lambda i: (i, j))
    ins = [rqkv] * 3 + [rg] * 2 + [W["ret_intra"], W["ret_qd"], W["ret_kdt"], W["ret_cd"], W["ret_gn_g"]]
    in_specs = [col(0), col(1), col(2), col(0), col(1)] + [_wspec(a, l) for a in ins[5:]]
    if has_s0:
        ins.append(s0_bd)
        in_specs.append(pl.BlockSpec((nb, 2, 256, 256), lambda i: (i, 0, 0, 0)))
    st = pl.BlockSpec((nb, 256, HEAD_W), lambda i: (i, 0, 0))
    return pl.pallas_call(
        functools.partial(_ret_kernel, nb, nc, has_s0),
        grid=(b // nb,),
        in_specs=in_specs,
        out_specs=[pl.BlockSpec((nb * n, 256), lambda i: (i, 0)), st, st],
        out_shape=[jax.ShapeDtypeStruct((t, 256), F32), jax.ShapeDtypeStruct((b, 256, HEAD_W), F32),
                   jax.ShapeDtypeStruct((b, 256, HEAD_W), F32)],
        scratch_shapes=[pltpu.VMEM((nb, 256, 256), F32), pltpu.VMEM((nb * nc, 256, 256), F32),
                        pltpu.VMEM((nb * n, 256), F32), pltpu.VMEM((nb * n, 256), F32)],
        compiler_params=_params(1),
        name="ret",
    )(*ins)


ATT_TQ = 256


def _attn_kernel(nparts, nb, qt_ref, *refs):
    k_refs = refs[0:2 * nparts:2]
    vt_refs = refs[1:2 * nparts:2]
    o_ref = refs[2 * nparts]
    n_own = k_refs[-1].shape[0] // nb

    def operands(sq, hh):
        sl = slice(hh * LANES, (hh + 1) * LANES)
        vsl = slice(hh * HEAD_W, (hh + 1) * HEAD_W)
        own = slice(sq * n_own, (sq + 1) * n_own)
        parts = [(k_ref[:, sl], vt_ref[vsl, :]) for k_ref, vt_ref in zip(k_refs[:-1], vt_refs[:-1])]
        return parts + [(k_refs[-1][own, sl], vt_refs[-1][vsl, own])]

    work = [(sq, hh) for sq in range(nb) for hh in range(HEADS)]
    ss_all = [[_dot(kk, qt_ref[hh * LANES:(hh + 1) * LANES, sq * ATT_TQ:(sq + 1) * ATT_TQ])
               for kk, _ in operands(sq, hh)] for sq, hh in work]
    outs = []
    for (sq, hh), ss in zip(work, ss_all):
        m = ss[0].max(axis=0, keepdims=True)
        for s in ss[1:]:
            m = jnp.maximum(m, s.max(axis=0, keepdims=True))
        es = [jnp.exp(s - m) for s in ss]
        den = es[0].sum(axis=0, keepdims=True)
        for e in es[1:]:
            den = den + e.sum(axis=0, keepdims=True)
        oh = None
        for e, (_, vv) in zip(es, operands(sq, hh)):
            part = _dot(vv, e.astype(BF16))
            oh = part if oh is None else oh + part
        outs.append(oh * (1.0 / den))
    for sq in range(nb):
        o_ref[sq * ATT_TQ:(sq + 1) * ATT_TQ, :] = jnp.concatenate(outs[sq * HEADS:(sq + 1) * HEADS], axis=0).T


def _attn_call(l, qt, k, vt, cache, n_seq_tokens):
    t = k.shape[0]
    n = n_seq_tokens
    tq = ATT_TQ
    nq = n // tq
    b = t // n
    nb = min(b, 4) if (cache is None and nq == 1) else 1
    ins = [qt]
    in_specs = [pl.BlockSpec((512, nb * tq), lambda i, j: (0, i * nq + j))]
    if cache is not None:
        kc, vtc = cache
        m = kc.shape[2]
        ins += [kc, vtc]
        in_specs += [pl.BlockSpec((None, None, m, 512), lambda i, j: (l, i, 0, 0)),
                     pl.BlockSpec((None, None, 256, m), lambda i, j: (l, i, 0, 0))]
    ins += [k, vt]
    in_specs += [pl.BlockSpec((nb * n, 512), lambda i, j: (i, 0)), pl.BlockSpec((256, nb * n), lambda i, j: (0, i))]
    return pl.pallas_call(
        functools.partial(_attn_kernel, len(ins) // 2, nb),
        grid=(b // nb, nq),
        in_specs=in_specs,
        out_specs=pl.BlockSpec((nb * tq, 256), lambda i, j: (i * nq + j, 0)),
        out_shape=jax.ShapeDtypeStruct((t, 256), F32),
        compiler_params=_params(2),
        name="attn",
    )(*ins)


def _outproj_kernel(ya_ref, yb_ref, yc_ref, yd_ref, x_ref, mod_ref, wo_ref, g2_ref, rw_ref,
                    x1_ref, h2_ref, aff_ref, wb_ref):
    @pl.when(pl.program_id(0) == 0)
    def _():
        for rb in range(4):
            wb_ref[rb * 256:(rb + 1) * 256, :] = wo_ref[rb * 256:(rb + 1) * 256, :].astype(BF16)

    mix = _dot(ya_ref[...].astype(BF16), wb_ref[0:256, :])
    mix = mix + _dot(yb_ref[...].astype(BF16), wb_ref[256:512, :])
    mix = mix + _dot(yc_ref[...].astype(BF16), wb_ref[512:768, :])
    mix = mix + _dot(yd_ref[...].astype(BF16), wb_ref[768:1024, :])
    mod = mod_ref[...]
    x1 = x_ref[...] + mod[:, 2 * D:3 * D] * mix
    x1_ref[...] = x1
    h2 = _rms(x1) * g2_ref[...]
    h2 = h2 * (1.0 + mod[:, 4 * D:5 * D]) + mod[:, 3 * D:4 * D]
    h_hi, h_lo = _split_bf16(h2)
    h2_ref[...] = h_hi
    both = _dot(h_hi, rw_ref[...])
    logits = both[:, 0:LANES] + both[:, LANES:2 * LANES] + _dot(h_lo, rw_ref[:, 0:LANES])
    lane = _iota((1, LANES), 1)
    logits = jnp.where(lane < N_EXPERTS, logits, -1e30)
    e = jnp.exp(logits - logits.max(axis=-1, keepdims=True))
    aff_ref[...] = e / e.sum(axis=-1, keepdims=True)


def _outproj_call(l, ys, x, mod_p, n_seq_tokens, W, w_out):
    t = x.shape[0]
    tm = 256
    tps = n_seq_tokens // tm
    bm = mod_p.shape[0]
    row = lambda i: (i, 0)
    ins = list(ys) + [x, mod_p, w_out, W["norm2_g"], W["r_hilo"]]
    in_specs = [pl.BlockSpec((tm, 256), row)] * 4 + [
        pl.BlockSpec((tm, D), row),
        pl.BlockSpec((None, 1, 6 * D), (lambda i: (i // tps, 0, 0)) if bm > 1 else (lambda i: (0, 0, 0))),
    ] + [_wspec(a, l) for a in ins[6:]]
    return pl.pallas_call(
        _outproj_kernel,
        grid=(t // tm,),
        in_specs=in_specs,
        out_specs=[pl.BlockSpec((tm, D), row), pl.BlockSpec((tm, D), row), pl.BlockSpec((tm, LANES), row)],
        out_shape=[jax.ShapeDtypeStruct((t, D), F32), jax.ShapeDtypeStruct((t, D), BF16),
                   jax.ShapeDtypeStruct((t, LANES), F32)],
        scratch_shapes=[pltpu.VMEM((D, D), BF16)],
        compiler_params=_params(1, "arbitrary"),
        name="out_proj",
    )(*ins)


GROUP_ROWS = 512
CUM_BLK = 256


def _route_kernel(n, cap, nseq, aff_ref, h2_ref, slot_ref, xe_ref, gs_ref, cum_ref, slot_t_ref):
    lane_row = _iota((1, LANES), 1)
    a = aff_ref[0:n, :]
    for s in range(1, nseq):
        a = a + pltpu.roll(aff_ref[s * n:(s + 1) * n, :], N_EXPERTS * s, 1)
    capf = jnp.float32(cap)
    used = N_EXPERTS * nseq
    fold = LANES // used
    folded = a[0:n // fold, :]
    for k in range(1, fold):
        folded = folded + pltpu.roll(a[k * (n // fold):(k + 1) * (n // fold), :], used * k, 1)

    def bisect(_, lohi):
        lo, hi = lohi
        mid = lo + lax.shift_right_logical(hi - lo, 1)
        cnt = jnp.sum(jnp.where(folded >= pltpu.bitcast(mid, F32), 1.0, 0.0), axis=0, keepdims=True)
        width = used
        while width < LANES:
            cnt = cnt + pltpu.roll(cnt, width, 1)
            width *= 2
        ok = cnt >= capf
        return jnp.where(ok, mid, lo), jnp.where(ok, hi, mid)

    lo0 = jnp.zeros((1, LANES), jnp.int32)
    hi0 = jnp.full((1, LANES), 0x3F800001, jnp.int32)
    lo, hi = lax.fori_loop(0, 31, bisect, (lo0, hi0))
    gt = a >= pltpu.bitcast(hi, F32)
    eq = (a >= pltpu.bitcast(lo, F32)) & jnp.logical_not(gt)
    need = capf - jnp.sum(jnp.where(gt, 1.0, 0.0), axis=0, keepdims=True)
    tri = jnp.where(_iota((CUM_BLK, CUM_BLK), 0) >= _iota((CUM_BLK, CUM_BLK), 1), 1.0, 0.0).astype(BF16)

    def cumsum_rows(flags):
        carry = jnp.zeros((1, LANES), F32)
        for rb in range(n // CUM_BLK):
            rows = slice(rb * CUM_BLK, (rb + 1) * CUM_BLK)
            part = _dot(tri, flags[rows, :].astype(BF16)) + carry
            cum_ref[rows, :] = part
            carry = part[CUM_BLK - 1:CUM_BLK, :]
        return cum_ref[...]

    eq_rank = cumsum_rows(jnp.where(eq, 1.0, 0.0))
    sel = gt | (eq & (eq_rank <= need))
    pos = cumsum_rows(jnp.where(sel, 1.0, 0.0))
    slot = jnp.where(sel, pos - 1.0, -1.0)
    for s in range(nseq):
        own = slot if s == 0 else pltpu.roll(slot, LANES - N_EXPERTS * s, 1)
        slot_ref[s * n:(s + 1) * n, :] = jnp.where(lane_row < N_EXPERTS, own, -1.0)
    slot_t_ref[...] = slot.T
    a_hi, a_lo = _split_bf16(a)
    a_hilo = jnp.concatenate([a_hi, a_lo], axis=1)
    ones = jnp.ones((LANES, LANES), BF16)
    gexp = GROUP_ROWS // cap
    shift = int(np.log2(cap))
    row_e = lax.shift_right_logical(_iota((GROUP_ROWS, LANES), 0), shift)
    slot_id = _iota((cap, n), 0).astype(F32)
    lane = _iota((GROUP_ROWS, LANES), 1)
    def build_onehot(s, g):
        pieces = []
        for j in range(gexp):
            e_lane = N_EXPERTS * s + g * gexp + j
            mine_row = jnp.broadcast_to(slot_t_ref[e_lane:e_lane + 1, :], (cap, n))
            pieces.append(jnp.where(mine_row == slot_id, 1.0, 0.0).astype(BF16))
        return pieces[0] if gexp == 1 else jnp.concatenate(pieces, axis=0)

    items = [(s, g) for s in range(nseq) for g in range(N_EXPERTS * cap // GROUP_ROWS)]
    onehot_next = build_onehot(*items[0])
    for idx, (s, g) in enumerate(items):
        onehot = onehot_next
        if idx + 1 < len(items):
            onehot_next = build_onehot(*items[idx + 1])
        xe = _dot(onehot, h2_ref[s * n:(s + 1) * n, :]).astype(BF16)
        gboth = _dot(onehot, a_hilo)
        mine = lane == row_e + (g * gexp + N_EXPERTS * s)
        g_hi, g_lo = _split_bf16(jnp.where(mine, gboth[:, 0:LANES] + gboth[:, LANES:2 * LANES], 0.0))
        gsb = _dot(g_hi, ones) + _dot(g_lo, ones)
        for j in range(gexp):
            xe_ref[g * gexp + j, s * cap:(s + 1) * cap, :] = xe[j * cap:(j + 1) * cap, :]
            gs_ref[g * gexp + j, s * cap:(s + 1) * cap, :] = gsb[j * cap:(j + 1) * cap, :]


def _route_call(aff, h2, n_seq_tokens):
    t = aff.shape[0]
    n = n_seq_tokens
    b = t // n
    cap = EC_CAPACITY * n // N_EXPERTS
    nseq = min(b, LANES // N_EXPERTS, max(1, 2048 // n))
    return pl.pallas_call(
        functools.partial(_route_kernel, n, cap, nseq),
        grid=(b // nseq,),
        in_specs=[pl.BlockSpec((nseq * n, LANES), lambda i: (i, 0)), pl.BlockSpec((nseq * n, D), lambda i: (i, 0))],
        out_specs=[pl.BlockSpec((nseq * n, LANES), lambda i: (i, 0)),
                   pl.BlockSpec((N_EXPERTS, nseq * cap, D), lambda i: (0, i, 0)),
                   pl.BlockSpec((N_EXPERTS, nseq * cap, LANES), lambda i: (0, i, 0))],
        out_shape=[jax.ShapeDtypeStruct((t, LANES), F32),
                   jax.ShapeDtypeStruct((N_EXPERTS, b * cap, D), BF16),
                   jax.ShapeDtypeStruct((N_EXPERTS, b * cap, LANES), F32)],
        scratch_shapes=[pltpu.VMEM((n, LANES), F32), pltpu.VMEM((LANES, n), F32)],
        compiler_params=_params(1),
        name="route",
    )(aff, h2)


FF_TILE = 512
FFN_ROWS = 512


def _ffn_kernel(xc_ref, xl_ref, gc_ref, gl_ref, wg_ref, wu_ref, wd_ref, yc_ref, yl_ref, accc_ref, accl_ref):
    @pl.when(pl.program_id(1) == 0)
    def _():
        accc_ref[...] = jnp.zeros(accc_ref.shape, F32)
        accl_ref[...] = jnp.zeros(accl_ref.shape, F32)

    wg = wg_ref[...].astype(BF16)
    wu = wu_ref[...].astype(BF16)
    chunks = ([(xc_ref, gc_ref, yc_ref, accc_ref, r0) for r0 in range(0, xc_ref.shape[0], FFN_ROWS)]
              + [(xl_ref, gl_ref, yl_ref, accl_ref, r0) for r0 in range(0, xl_ref.shape[0], FFN_ROWS)])

    def gate_up(idx):
        x_ref, _, _, _, r0 = chunks[idx]
        x = x_ref[r0:r0 + FFN_ROWS, :]
        return _dot(x, wg), _dot(x, wu)

    nxt = gate_up(0)
    wd = wd_ref[...].astype(BF16)
    for idx, (x_ref, g_ref, y_ref, acc_ref, r0) in enumerate(chunks):
        a, up = nxt
        if idx + 1 < len(chunks):
            nxt = gate_up(idx + 1)
        rows = slice(r0, r0 + FFN_ROWS)
        contrib = _dot((_silu(a) * up).astype(BF16), wd)
        total = acc_ref[rows, :] + contrib
        acc_ref[rows, :] = total
        gate = jnp.concatenate([g_ref[rows, :]] * (D // LANES), axis=1)
        y_ref[rows, :] = (total * gate).astype(BF16)


def _ffn_call(l, xe_c, xe_l, gs_c, gs_l, w_gate, w_up, w_down):
    rc = xe_c.shape[1]
    rl = xe_l.shape[1]
    ex = lambda r, w: pl.BlockSpec((None, r, w), lambda e, f: (e, 0, 0))
    return pl.pallas_call(
        _ffn_kernel,
        grid=(N_EXPERTS, FF // FF_TILE),
        in_specs=[
            ex(rc, D), ex(rl, D), ex(rc, LANES), ex(rl, LANES),
            pl.BlockSpec((None, None, D, FF_TILE), lambda e, f: (l, e, 0, f)),
            pl.BlockSpec((None, None, D, FF_TILE), lambda e, f: (l, e, 0, f)),
            pl.BlockSpec((None, None, FF_TILE, D), lambda e, f: (l, e, f, 0)),
        ],
        out_specs=[ex(rc, D), ex(rl, D)],
        out_shape=[jax.ShapeDtypeStruct((N_EXPERTS, rc, D), BF16), jax.ShapeDtypeStruct((N_EXPERTS, rl, D), BF16)],
        scratch_shapes=[pltpu.VMEM((rc, D), F32), pltpu.VMEM((rl, D), F32)],
        compiler_params=pltpu.CompilerParams(dimension_semantics=("parallel", "arbitrary"),
                                             vmem_limit_bytes=VMEM_LIMIT),
        name="ffn",
    )(xe_c, xe_l, gs_c, gs_l, w_gate, w_up, w_down)


COMB_ROWS = 256


def _combine_kernel(cap, final, ye_ref, slot_ref, x1_ref, mod_ref, *rest):
    if final:
        fg_ref, x2_ref, yf_ref = rest
    else:
        (x2_ref,) = rest
    gexp = GROUP_ROWS // cap
    shift = int(np.log2(cap))
    col_e = lax.shift_right_logical(_iota((LANES, GROUP_ROWS), 1), shift)
    col_s = (_iota((1, GROUP_ROWS), 1) & (cap - 1)).astype(F32)
    lane_e = _iota((LANES, GROUP_ROWS), 0)
    sb = slot_ref[...].astype(BF16)

    def build_onehot_t(g):
        expand_m = jnp.where(lane_e == col_e + g * gexp, 1.0, 0.0).astype(BF16)
        return jnp.where(_dot(sb, expand_m) == col_s, 1.0, 0.0).astype(BF16)

    ngroups = N_EXPERTS * cap // GROUP_ROWS
    acc = jnp.zeros((COMB_ROWS, D), F32)
    onehot_next = build_onehot_t(0)
    for g in range(ngroups):
        onehot_t = onehot_next
        if g + 1 < ngroups:
            onehot_next = build_onehot_t(g + 1)
        if cap % LANES == 0:
            for j in range(gexp):
                acc = acc + _dot(onehot_t[:, j * cap:(j + 1) * cap], ye_ref[g * gexp + j])
        else:
            ye = jnp.concatenate([ye_ref[g * gexp + j] for j in range(gexp)], axis=0)
            acc = acc + _dot(onehot_t, ye)
    mod = mod_ref[...]
    x2 = x1_ref[...] + mod[:, 5 * D:6 * D] * acc
    x2_ref[...] = x2
    if final:
        yf_ref[...] = _rms(x2) * fg_ref[...]


def _combine_call(ye, slot, x1, mod_p, n_seq_tokens, final_g):
    t = x1.shape[0]
    n = n_seq_tokens
    b = t // n
    nr = n // COMB_ROWS
    cap = EC_CAPACITY * n // N_EXPERTS
    bm = mod_p.shape[0]
    final = final_g is not None
    row = lambda i, j: (i * nr + j, 0)
    ins = [ye, slot, x1, mod_p]
    in_specs = [
        pl.BlockSpec((N_EXPERTS, cap, D), lambda i, j: (0, i, 0)),
        pl.BlockSpec((COMB_ROWS, LANES), row),
        pl.BlockSpec((COMB_ROWS, D), row),
        pl.BlockSpec((None, 1, 6 * D), (lambda i, j: (i, 0, 0)) if bm > 1 else (lambda i, j: (0, 0, 0))),
    ]
    out_shape = [jax.ShapeDtypeStruct((t, D), F32)]
    out_specs = [pl.BlockSpec((COMB_ROWS, D), row)]
    if final:
        ins.append(final_g)
        in_specs.append(pl.BlockSpec((1, D), lambda i, j: (0, 0)))
        out_shape.append(jax.ShapeDtypeStruct((t, D), F32))
        out_specs.append(pl.BlockSpec((COMB_ROWS, D), row))
    return pl.pallas_call(
        functools.partial(_combine_kernel, cap, final),
        grid=(b, nr),
        in_specs=in_specs,
        out_specs=out_specs,
        out_shape=out_shape,
        compiler_params=_params(2),
        name="combine",
    )(*ins)


def _rot_cols(w):
    a, b, c, d = (w[..., 8 * i:8 * (i + 1)] for i in range(4))
    return jnp.concatenate([-b, a, -d, c], axis=-1)


def _rope_tables(n):
    rows = n // GRID_W
    row = jnp.repeat(jnp.arange(rows, dtype=F32), GRID_W)
    col = jnp.tile(jnp.arange(GRID_W, dtype=F32), rows)
    inv = ROPE_THETA ** (-jnp.arange(0, ROPE // 2, 2, dtype=F32) / (ROPE // 2))
    ra = row[:, None] * inv
    ca = col[:, None] * inv
    ang = jnp.concatenate([ra, ra, ca, ca], axis=-1)
    cos, sin = jnp.cos(ang), jnp.sin(ang)
    z32, z64, z96 = (jnp.zeros((n, w), F32) for w in (32, 64, 96))
    cq = jnp.concatenate([jnp.ones((n, 64), F32), cos, z32], axis=1)
    sq = jnp.concatenate([z64, sin, z32], axis=1)
    ck = jnp.concatenate([z64, cos, z32], axis=1)
    sk = jnp.concatenate([z96, sin], axis=1)
    return cq, sq, ck, sk


def _retention_tables(p_f, p_b):
    pos = jnp.arange(CHUNK, dtype=F32)
    diff = pos[:, None] - pos[None, :]

    def one(p, backward):
        lg = jnp.log1p(-jnp.exp2(p.astype(F32)))[:, :, None]
        dd = -diff if backward else diff
        intra = jnp.where(dd >= 0, jnp.exp(jnp.maximum(dd, 0.0) * lg[..., None]), 0.0)
        qexp = (CHUNK - pos) if backward else (pos + 1.0)
        kexp = pos if backward else (CHUNK - 1.0 - pos)
        qd = jnp.exp(qexp * lg)
        kd = jnp.exp(kexp * lg)
        cd = jnp.exp(CHUNK * lg)
        nl = p.shape[0]
        intra_w = jnp.swapaxes(intra, 1, 2).reshape(nl, CHUNK, HEADS * CHUNK)
        qd_w = jnp.repeat(jnp.swapaxes(qd, 1, 2), HEAD_W, axis=2)
        kd_t = jnp.repeat(kd, HEAD_W, axis=1)
        cdw = jnp.broadcast_to(jnp.repeat(cd, HEAD_W, axis=1), (nl, 256, 256))
        return intra_w, qd_w, kd_t, cdw

    f = one(p_f, False)
    b = one(p_b, True)
    return tuple(jnp.stack([x, y], axis=1) for x, y in zip(f, b))


def _prepare(norm1_g, w_in, sg_norm_g, sg_w, sg_b, conv_w, conv_b, conv_ln_g, conv_ln_b, conv_pw, ret_decay_f,
             ret_decay_b, ret_gn_g, q_norm_g, w_uq, kv_norm_g, w_ukv, norm2_g, router):
    L = DEPTH
    row = lambda a: a.reshape(L, 1, -1)
    tail_src = w_in[:, :, MAIN_W:]
    kpe_cols = tail_src[:, :, Q_LORA + KV_LORA:]
    w_tail = jnp.concatenate([tail_src[:, :, Q_LORA:Q_LORA + KV_LORA], tail_src[:, :, :Q_LORA], kpe_cols,
                              _rot_cols(kpe_cols)], axis=2)
    uq = w_uq.reshape(L, Q_LORA, HEADS, HEAD_W + ROPE)
    pad_rows = lambda a: jnp.pad(a, ((0, 0), (0, 256 - Q_LORA), (0, 0)))
    wq = pad_rows(jnp.pad(uq, ((0, 0), (0, 0), (0, 0), (0, LANES - HEAD_W - ROPE))).reshape(L, Q_LORA, 512))
    uq_rot = jnp.pad(_rot_cols(uq[..., HEAD_W:]), ((0, 0), (0, 0), (0, 0), (HEAD_W, LANES - HEAD_W - ROPE)))
    wqr = pad_rows(uq_rot.reshape(L, Q_LORA, 512))
    ukv = w_ukv.reshape(L, KV_LORA, HEADS, 2 * HEAD_W)
    wk = jnp.pad(ukv[..., :HEAD_W], ((0, 0), (0, 0), (0, 0), (0, LANES - HEAD_W))).reshape(L, KV_LORA, 512)
    wv = ukv[..., HEAD_W:].reshape(L, KV_LORA, 256)
    intra, qd, kd, cd = _retention_tables(ret_decay_f, ret_decay_b)
    r_pad = jnp.pad(router, ((0, 0), (0, 0), (0, LANES - N_EXPERTS)))
    r_hi = r_pad.astype(BF16)
    r_lo = (r_pad - r_hi.astype(F32)).astype(BF16)
    return dict(
        norm1_g=row(norm1_g), w_tail=w_tail.astype(BF16), kv_g=row(kv_norm_g),
        q_g=jnp.pad(row(q_norm_g), ((0, 0), (0, 0), (0, 256 - Q_LORA))),
        wq=wq.astype(BF16), wqr=wqr.astype(BF16), wk=wk.astype(BF16), wv=wv.astype(BF16),
        sg_g=row(sg_norm_g), sg_w=sg_w.astype(BF16), sg_bm=jnp.repeat(jnp.swapaxes(sg_b, 1, 2), HEAD_W, axis=2),
        conv_w=jnp.pad(conv_w, ((0, 0), (0, 1), (0, 0))), conv_b=row(conv_b), conv_ln_g=row(conv_ln_g),
        conv_ln_b=row(conv_ln_b), conv_pw=conv_pw.astype(BF16),
        ret_intra=intra, ret_qd=qd, ret_kdt=kd, ret_cd=cd, ret_gn_g=row(ret_gn_g),
        norm2_g=row(norm2_g), r_hilo=jnp.concatenate([r_hi, r_lo], axis=2),
    )


def _block_diag_states(state):
    eye = jnp.eye(HEADS, dtype=state.dtype)
    bd = state[:, :, :, :, None, :] * eye[None, None, :, None, :, None]
    return bd.reshape(state.shape[0], 2, HEADS * HEAD_W, HEADS * HEAD_W)


def _mixers(l, x, mod_p, n, W, w_in, w_out, rope_tabs, s0_bd, cache):
    outs = _inproj_call(l, x, mod_p, n, W, w_in, rope_tabs)
    ya, cv, rqkv, rg, qt, k, vt = outs[:7]
    yb = _conv_call(l, cv, n, W)
    yc, s_f, s_b = _ret_call(l, rqkv, rg, n, W, s0_bd)
    yd = _attn_call(l, qt, k, vt, cache, n)
    x1, h2, aff = _outproj_call(l, (ya, yb, yc, yd), x, mod_p, n, W, w_out)
    slot, xe, gs = _route_call(aff, h2, n)
    return x1, slot, xe, gs, outs[7:], (s_f, s_b)


def kernel(x_prompt, x_sample, cache_mla_ckv, cache_mla_kpe, state_ret, c, c_ctx, w_mod, b_mod, norm1_g, w_in, sg_norm_g, sg_w, sg_b, conv_w, conv_b, conv_ln_g, conv_ln_b, conv_pw, ret_decay_f, ret_decay_b, ret_gn_g, q_norm_g, w_uq, kv_norm_g, w_ukv, w_out, norm2_g, router, w_gate, w_up, w_down, final_norm_g):
    bc, nc_tok, _ = x_prompt.shape
    bl, nl_tok, _ = x_sample.shape
    W = _prepare(norm1_g, w_in, sg_norm_g, sg_w, sg_b, conv_w, conv_b, conv_ln_g, conv_ln_b, conv_pw, ret_decay_f,
                 ret_decay_b, ret_gn_g, q_norm_g, w_uq, kv_norm_g, w_ukv, norm2_g, router)
    cvec = jnp.zeros((8, D), F32).at[0].set(c_ctx).at[1:1 + bl].set(c)
    mod_all = _mod_call(cvec, w_mod, b_mod)
    rope_tabs = _rope_tables(nl_tok)
    kpe128 = jnp.pad(cache_mla_kpe, ((0, 0), (0, 0), (0, 0), (HEAD_W, LANES - HEAD_W - ROPE)))
    cache = _kvcache_call(cache_mla_ckv, kpe128, W["wk"], W["wv"])
    s0_all = _block_diag_states(jnp.swapaxes(state_ret, 0, 1).reshape(DEPTH * bl, 2, HEADS, HEAD_W, HEAD_W))
    s0_all = s0_all.reshape(DEPTH, bl, 2, 256, 256)
    fg = final_norm_g.reshape(1, D)
    xp = x_prompt.reshape(bc * nc_tok, D)
    xs = x_sample.reshape(bl * nl_tok, D)
    ckvs, kpes, rets = [], [], []
    yp = ys = None
    for l in range(DEPTH):
        mod_c = mod_all[l, 0:1].reshape(1, 1, 6 * D)
        mod_l = mod_all[l, 1:1 + bl].reshape(bl, 1, 6 * D)
        x1c, slot_c, xe_c, gs_c, (ckv, slab), (s_f, s_b) = _mixers(l, xp, mod_c, nc_tok, W, w_in, w_out, None, None,
                                                                 None)
        x1l, slot_l, xe_l, gs_l, _, _ = _mixers(l, xs, mod_l, nl_tok, W, w_in, w_out, rope_tabs, s0_all[l], cache)
        ye_c, ye_l = _ffn_call(l, xe_c, xe_l, gs_c, gs_l, w_gate, w_up, w_down)
        last = l == DEPTH - 1
        res_c = _combine_call(ye_c, slot_c, x1c, mod_c, nc_tok, fg if last else None)
        res_l = _combine_call(ye_l, slot_l, x1l, mod_l, nl_tok, fg if last else None)
        xp, xs = res_c[0], res_l[0]
        if last:
            yp, ys = res_c[1], res_l[1]
        ckvs.append(ckv.reshape(bc, nc_tok, KV_LORA))
        kpes.append(slab[:, HEAD_W:HEAD_W + ROPE].reshape(bc, nc_tok, ROPE))
        rets.append(jnp.stack([s_f, s_b], axis=1).reshape(bc, 2, HEADS, HEAD_W, HEAD_W))
    return (yp.reshape(bc, nc_tok, D), ys.reshape(bl, nl_tok, D), jnp.stack(ckvs, axis=1), jnp.stack(kpes, axis=1),
            jnp.stack(rets, axis=1))
```

```python
import functools

import jax
import jax.numpy as jnp
import numpy as np
from jax import lax
from jax.experimental import pallas as pl
from jax.experimental.pallas import tpu as pltpu

F32 = jnp.float32
BF16 = jnp.bfloat16

D = 1024
DEPTH = 4
CHUNK = 128
EPS = 1e-6
GRID_W = 64
CONV_K = 31
HEADS = 4
HEAD_W = 64
Q_LORA = 192
KV_LORA = 128
ROPE = 32
ROPE_THETA = 10000.0
N_EXPERTS = 16
FF = 1024
EC_CAPACITY = 2
MAIN_W = 2304
TAIL_W = 384
ATT_SCALE = (HEAD_W + ROPE) ** -0.5
LANES = 128
SUBLANES = 8
VMEM_LIMIT = 56 * 1024 * 1024


def _iota(shape, dim):
    return lax.broadcasted_iota(jnp.int32, shape, dim)


def _dot(a, b):
    return jnp.dot(a, b, preferred_element_type=F32)


def _dot_nt(a, b):
    return lax.dot_general(a, b, (((1,), (1,)), ((), ())), preferred_element_type=F32)


def _split_bf16(x):
    hi = x.astype(BF16)
    lo = (x - hi.astype(F32)).astype(BF16)
    return hi, lo


def _sigmoid(x):
    return 1.0 / (1.0 + jnp.exp(-x))


def _silu(x):
    return x * _sigmoid(x)


def _rms(x):
    return x * lax.rsqrt(jnp.mean(x * x, axis=-1, keepdims=True) + EPS)


def _wspec(arr, l):
    nd = arr.ndim
    return pl.BlockSpec((None,) + tuple(arr.shape[1:]), lambda *_: (l,) + (0,) * (nd - 1))


def _params(n_axes, sem="parallel"):
    return pltpu.CompilerParams(dimension_semantics=(sem,) * n_axes, vmem_limit_bytes=VMEM_LIMIT)


def _mod_kernel(c_ref, w_ref, b_ref, o_ref):
    cv = c_ref[...]
    s_hi, s_lo = _split_bf16(_silu(cv))
    w_hi, w_lo = _split_bf16(w_ref[...])
    o_ref[...] = _dot(s_hi, w_hi) + _dot(s_lo, w_hi) + _dot(s_hi, w_lo) + b_ref[...]


def _mod_call(cvec, w_mod, b_mod):
    nt = 1536
    return pl.pallas_call(
        _mod_kernel,
        grid=(DEPTH, 6 * D // nt),
        in_specs=[
            pl.BlockSpec((8, D), lambda l, j: (0, 0)),
            pl.BlockSpec((None, D, nt), lambda l, j: (l, 0, j)),
            pl.BlockSpec((None, 1, nt), lambda l, j: (l, 0, j)),
        ],
        out_specs=pl.BlockSpec((None, 8, nt), lambda l, j: (l, 0, j)),
        out_shape=jax.ShapeDtypeStruct((DEPTH, 8, 6 * D), F32),
        compiler_params=_params(2),
        name="mod",
    )(cvec, w_mod, b_mod.reshape(DEPTH, 1, 6 * D))


N_MAIN_BLOCKS = MAIN_W // 256
RET_K_BLOCK = 5


def _inproj_kernel(rope, x_ref, mod_ref, g1_ref, w_ref, wt_ref, kvg_ref, qg_ref, wq_ref, wqr_ref, wk_ref, wv_ref,
                   sgg_ref, sgw_ref, sgb_ref, *rest):
    if rope:
        cq_ref, sq_ref, ck_ref, sk_ref, ya_ref, cv_ref, rqkv_ref, rg_ref, qt_ref, k_ref, vt_ref, wb_ref = rest
    else:
        _, _, ya_ref, cv_ref, rqkv_ref, rg_ref, qt_ref, k_ref, vt_ref, ckv_ref, kpe_ref, wb_ref = rest

    @pl.when(pl.program_id(0) == 0)
    def _():
        for cb in range(N_MAIN_BLOCKS):
            blk = w_ref[cb * 256:(cb + 1) * 256, :]
            if cb == RET_K_BLOCK:
                blk = blk * (HEAD_W ** -0.5)
            wb_ref[cb * 256:(cb + 1) * 256, :] = blk.astype(BF16)
        wb_ref[MAIN_W:MAIN_W + TAIL_W, :] = wt_ref[...].astype(BF16)

    x = x_ref[...]
    mod = mod_ref[...]
    h = _rms(x) * g1_ref[...]
    h = h * (1.0 + mod[:, D:2 * D]) + mod[:, 0:D]
    hb = h.astype(BF16)
    proj = lambda cb: _dot_nt(hb, wb_ref[cb * 256:(cb + 1) * 256, :])
    v_gate = proj(1)
    tail = _dot_nt(hb, wb_ref[MAIN_W:MAIN_W + TAIL_W, :])
    u = proj(0)
    cv_ref[:, 0:256] = proj(2)
    cv_ref[:, 256:512] = proj(3)
    for j in range(3):
        rqkv_ref[:, j * 256:(j + 1) * 256] = proj(4 + j).astype(BF16)
    rg_ref[:, 0:256] = proj(7)
    rg_ref[:, 256:512] = proj(8)
    vn = _rms(v_gate) * sgg_ref[...]
    group = lax.shift_right_logical(_iota((1, 256), 1), 6)
    for cidx in range(x.shape[0] // CHUNK):
        rows = slice(cidx * CHUNK, (cidx + 1) * CHUNK)
        acc = jnp.zeros((CHUNK, 256), F32)
        for g in range(HEADS):
            acc = acc + _dot(sgw_ref[g], jnp.where(group == g, vn[rows, :], 0.0).astype(BF16))
        ya_ref[rows, :] = u[rows, :] * (acc + sgb_ref[...])
    ckv_n = _rms(tail[:, 0:128]) * kvg_ref[...]
    c256 = tail[:, 128:384]
    lane256 = _iota((1, 256), 1)
    ms = jnp.sum(jnp.where(lane256 < Q_LORA, c256 * c256, 0.0), axis=-1, keepdims=True) * (1.0 / Q_LORA)
    cqn = (c256 * lax.rsqrt(ms + EPS) * qg_ref[...]).astype(BF16)
    ckvb = ckv_n.astype(BF16)
    q = _dot(cqn, wq_ref[...])
    kn = _dot(ckvb, wk_ref[...])
    vt_ref[...] = _dot(ckvb, wv_ref[...]).T.astype(BF16)
    slab = tail[:, 256:384]
    if rope:
        qr = _dot(cqn, wqr_ref[...])
        kx = slab * ck_ref[...] + pltpu.roll(slab * sk_ref[...], LANES - ROPE, 1)
        q = q * jnp.concatenate([cq_ref[...]] * HEADS, axis=1) + qr * jnp.concatenate([sq_ref[...]] * HEADS, axis=1)
    else:
        lane128 = _iota((1, LANES), 1)
        kx = jnp.where((lane128 >= HEAD_W) & (lane128 < HEAD_W + ROPE), slab, 0.0)
        ckv_ref[...] = ckv_n
        kpe_ref[...] = pltpu.roll(slab, LANES - HEAD_W, 1)[:, 0:ROPE]
    qt_ref[...] = (q * ATT_SCALE).T.astype(BF16)
    for hh in range(HEADS):
        sl = slice(hh * LANES, (hh + 1) * LANES)
        k_ref[:, sl] = (kn[:, sl] + kx).astype(BF16)


def _inproj_call(l, x, mod_p, n_seq_tokens, W, w_in, rope_tabs, cache_out):
    t = x.shape[0]
    tm = 256
    tps = n_seq_tokens // tm
    bm = mod_p.shape[0]
    rope = rope_tabs is not None
    row = lambda i: (i, 0)
    col = lambda i: (0, i)
    rest = [W["w_tail"], W["kv_g"], W["q_g"], W["wq"], W["wqr"], W["wk"], W["wv"], W["sg_g"], W["sg_w"], W["sg_bm"]]
    ins = [x, mod_p, W["norm1_g"], w_in] + rest
    in_specs = [
        pl.BlockSpec((tm, D), row),
        pl.BlockSpec((None, 1, 6 * D), (lambda i: (i // tps, 0, 0)) if bm > 1 else (lambda i: (0, 0, 0))),
        _wspec(W["norm1_g"], l),
        pl.BlockSpec((None, MAIN_W, D), lambda i: (l, 0, 0)),
    ] + [_wspec(a, l) for a in rest]
    out_shape = [
        jax.ShapeDtypeStruct((t, 256), F32), jax.ShapeDtypeStruct((t, 512), F32),
        jax.ShapeDtypeStruct((t, 768), BF16), jax.ShapeDtypeStruct((t, 512), F32),
        jax.ShapeDtypeStruct((512, t), BF16), jax.ShapeDtypeStruct((t, 512), BF16),
        jax.ShapeDtypeStruct((256, t), BF16),
    ]
    out_specs = [pl.BlockSpec((tm, 256), row), pl.BlockSpec((tm, 512), row), pl.BlockSpec((tm, 768), row),
                 pl.BlockSpec((tm, 512), row), pl.BlockSpec((512, tm), col), pl.BlockSpec((tm, 512), row),
                 pl.BlockSpec((256, tm), col)]
    aliases = {}
    if rope:
        ins += list(rope_tabs)
        in_specs += [pl.BlockSpec((tm, LANES), lambda i: (i % tps, 0))] * 4
    else:
        for acc in cache_out:
            aliases[len(ins)] = len(out_shape)
            ins.append(acc)
            in_specs.append(pl.BlockSpec(memory_space=pl.ANY))
            out_shape.append(jax.ShapeDtypeStruct(acc.shape, acc.dtype))
            out_specs.append(pl.BlockSpec((None, None, tm, acc.shape[-1]), lambda i: (i // tps, l, i % tps, 0)))
    return pl.pallas_call(
        functools.partial(_inproj_kernel, rope),
        grid=(t // tm,),
        in_specs=in_specs,
        out_specs=out_specs,
        out_shape=out_shape,
        input_output_aliases=aliases,
        scratch_shapes=[pltpu.VMEM((MAIN_W + TAIL_W, D), BF16)],
        compiler_params=_params(1, "arbitrary"),
        name="in_proj",
    )(*ins)


def _kvcache_kernel(ckv_ref, kpe_ref, wk_ref, wv_ref, k_ref, vt_ref):
    cb = ckv_ref[...].astype(BF16)
    kn = _dot(cb, wk_ref[...])
    kx = kpe_ref[...]
    for hh in range(HEADS):
        sl = slice(hh * LANES, (hh + 1) * LANES)
        k_ref[:, sl] = (kn[:, sl] + kx).astype(BF16)
    vt_ref[...] = _dot(cb, wv_ref[...]).T.astype(BF16)


def _kvcache_call(cache_ckv, kpe128, wk, wv):
    b, depth, m, _ = cache_ckv.shape
    blk = lambda w: pl.BlockSpec((None, None, m, w), lambda l, i: (i, l, 0, 0))
    wblk = lambda a: pl.BlockSpec((None,) + tuple(a.shape[1:]), lambda l, i: (l, 0, 0))
    return pl.pallas_call(
        _kvcache_kernel,
        grid=(depth, b),
        in_specs=[blk(128), blk(128), wblk(wk), wblk(wv)],
        out_specs=[pl.BlockSpec((None, None, m, 512), lambda l, i: (l, i, 0, 0)),
                   pl.BlockSpec((None, None, 256, m), lambda l, i: (l, i, 0, 0))],
        out_shape=[jax.ShapeDtypeStruct((depth, b, m, 512), BF16), jax.ShapeDtypeStruct((depth, b, 256, m), BF16)],
        compiler_params=_params(2),
        name="kv_cache",
    )(cache_ckv, kpe128, wk, wv)


HALO = 16


def _conv_kernel(rb, nblk, cur_ref, prev_ref, next_ref, wdw_ref, bdw_ref, lng_ref, lnb_ref, wpw_ref, o_ref, pad_ref,
                 sh_ref):
    i = pl.program_id(0)
    keep_prev = (i % nblk != 0).astype(F32)
    keep_next = (i % nblk != nblk - 1).astype(F32)

    def glu(blk):
        return blk[:, 0:256] * _sigmoid(blk[:, 256:512])

    pad_ref[0:HALO, :] = glu(prev_ref[...]) * keep_prev
    pad_ref[HALO:HALO + rb, :] = glu(cur_ref[...])
    pad_ref[HALO + rb:2 * HALO + rb, :] = glu(next_ref[...]) * keep_next
    span = rb + 2 * HALO - SUBLANES
    for s in range(SUBLANES):
        sh_ref[s] = pad_ref[s:s + span, :]
    off = HALO - CONV_K // 2
    for j in range(rb // CHUNK):
        acc = jnp.zeros((CHUNK, 256), F32)
        for k in range(CONV_K):
            start = j * CHUNK + (off + k) // SUBLANES * SUBLANES
            acc = acc + sh_ref[(off + k) % SUBLANES, start:start + CHUNK, :] * wdw_ref[k:k + 1, :]
        y = acc + bdw_ref[...]
        mu = jnp.mean(y, axis=-1, keepdims=True)
        dlt = y - mu
        var = jnp.mean(dlt * dlt, axis=-1, keepdims=True)
        z = _silu(dlt * lax.rsqrt(var + EPS) * lng_ref[...] + lnb_ref[...])
        o_ref[j * CHUNK:(j + 1) * CHUNK, :] = _dot(z.astype(BF16), wpw_ref[...])


def _conv_call(l, cv, n_seq_tokens, W):
    t = cv.shape[0]
    rb = min(n_seq_tokens, 512)
    nblk = n_seq_tokens // rb
    per = rb // HALO
    last = t // HALO - 1
    ins = [cv, cv, cv, W["conv_w"], W["conv_b"], W["conv_ln_g"], W["conv_ln_b"], W["conv_pw"]]
    return pl.pallas_call(
        functools.partial(_conv_kernel, rb, nblk),
        grid=(t // rb,),
        in_specs=[
            pl.BlockSpec((rb, 512), lambda i: (i, 0)),
            pl.BlockSpec((HALO, 512), lambda i: (jnp.maximum(i * per - 1, 0), 0)),
            pl.BlockSpec((HALO, 512), lambda i: (jnp.minimum((i + 1) * per, last), 0)),
        ] + [_wspec(a, l) for a in ins[3:]],
        out_specs=pl.BlockSpec((rb, 256), lambda i: (i, 0)),
        out_shape=jax.ShapeDtypeStruct((t, 256), F32),
        scratch_shapes=[pltpu.VMEM((rb + 2 * HALO, 256), F32),
                        pltpu.VMEM((SUBLANES, rb + 2 * HALO - SUBLANES, 256), F32)],
        compiler_params=_params(1),
        name="conv",
    )(*ins)


def _ret_kernel(nb, nc, has_s0, rq_ref, rk_ref, rv_ref, gf_ref, gb_ref, intra_ref, qd_ref, kdt_ref, cd_ref, gng_ref,
                *rest):
    if has_s0:
        s0_ref, y_ref, s_ref, kvb_ref, of_ref, ob_ref = rest
        st_ref = None
    else:
        _, y_ref, st_ref, s_ref, kvb_ref, of_ref, ob_ref = rest
    same_head = lax.shift_right_logical(_iota((256, 256), 0), 6) == lax.shift_right_logical(_iota((256, 256), 1), 6)
    avg = jnp.where(same_head, 1.0 / HEAD_W, 0.0).astype(BF16)
    row_head = lax.shift_right_logical(_iota((256, CHUNK), 0), 6)
    lane_head = lax.shift_right_logical(_iota((CHUNK, 256), 1), 6)

    def head_mean(x):
        hi, lo = _split_bf16(x)
        return _dot(hi, avg) + _dot(lo, avg)

    def rows_of(sq, cidx):
        return pl.ds(pl.multiple_of(sq * nc * CHUNK + cidx * CHUNK, CHUNK), CHUNK)

    def compact(st):
        return st[:, 0:64] + st[:, 64:128] + st[:, 128:192] + st[:, 192:256]

    for sq in range(nb):
        s_ref[sq] = s0_ref[sq, 0] if has_s0 else jnp.zeros((256, 256), F32)

    def fwd_one(sq, cidx):
        rows = rows_of(sq, cidx)
        qb = rq_ref[rows, :]
        vb = rv_ref[rows, :]
        kt = rk_ref[rows, :].astype(F32).T
        kbd = jnp.concatenate([jnp.where(row_head == hh, kt, 0.0).astype(BF16) for hh in range(HEADS)], axis=1)
        s = _dot(qb, kbd)
        p = jnp.concatenate([(s * intra_ref[0]).astype(BF16), (s * intra_ref[1]).astype(BF16)], axis=0)
        zero = jnp.zeros_like(vb)
        vbd = jnp.concatenate([jnp.where(lane_head == hh, vb, zero) for hh in range(HEADS)], axis=0)
        inner = _dot(p, vbd)
        kts = jnp.concatenate([(kt * kdt_ref[0]).astype(BF16), (kt * kdt_ref[1]).astype(BF16)], axis=0)
        kv = _dot(kts, vb)
        kvb_ref[sq * nc + cidx] = jnp.where(same_head, kv[256:512, :], 0.0)
        ob_ref[rows, :] = inner[CHUNK:2 * CHUNK, :]
        st = s_ref[sq]
        of_ref[rows, :] = inner[0:CHUNK, :] + _dot(qb, st.astype(BF16)) * qd_ref[0]
        s_ref[sq] = cd_ref[0] * st + jnp.where(same_head, kv[0:256, :], 0.0)

    def fwd_body(cidx, carry):
        for sq in range(nb):
            fwd_one(sq, cidx)
        return carry

    lax.fori_loop(0, nc, fwd_body, 0, unroll=2)
    for sq in range(nb):
        if st_ref is not None:
            st_ref[sq, 0] = compact(s_ref[sq])
        s_ref[sq] = s0_ref[sq, 1] if has_s0 else jnp.zeros((256, 256), F32)

    def bwd_body(it, carry):
        cidx = nc - 1 - it
        for sq in range(nb):
            rows = rows_of(sq, cidx)
            st = s_ref[sq]
            ob_ref[rows, :] = ob_ref[rows, :] + _dot(rq_ref[rows, :], st.astype(BF16)) * qd_ref[1]
            s_ref[sq] = cd_ref[1] * st + kvb_ref[sq * nc + cidx]
        return carry

    lax.fori_loop(0, nc, bwd_body, 0, unroll=2)
    if st_ref is not None:
        for sq in range(nb):
            st_ref[sq, 1] = compact(s_ref[sq])

    def norm_body(blk, carry):
        rows = pl.ds(pl.multiple_of(blk * 2 * CHUNK, 2 * CHUNK), 2 * CHUNK)
        o = jnp.concatenate([of_ref[rows, :], ob_ref[rows, :]], axis=0)
        dlt = o - head_mean(o)
        nrm = dlt * lax.rsqrt(head_mean(dlt * dlt) + EPS) * gng_ref[...]
        y_ref[rows, :] = (_silu(gf_ref[rows, :]) * nrm[0:2 * CHUNK, :]
                          + _silu(gb_ref[rows, :]) * nrm[2 * CHUNK:4 * CHUNK, :])
        return carry

    lax.fori_loop(0, nb * nc // 2, norm_body, 0, unroll=2)


RET_BLOCK_ROWS = 4096


def _ret_call(l, rqkv, rg, n_seq_tokens, W, s0_bd, state_out):
    t = rqkv.shape[0]
    n = n_seq_tokens
    b = t // n
    nc = n // CHUNK
    nb = min(b, max(1, RET_BLOCK_ROWS // n), 4)
    has_s0 = s0_bd is not None
    col = lambda j: pl.BlockSpec((nb * n, 256), lambda i: (i, j))
    ins = [rqkv] * 3 + [rg] * 2 + [W["ret_intra"], W["ret_qd"], W["ret_kdt"], W["ret_cd"], W["ret_gn_g"]]
    in_specs = [col(0), col(1), col(2), col(0), col(1)] + [_wspec(a, l) for a in ins[5:]]
    out_specs = [pl.BlockSpec((nb * n, 256), lambda i: (i, 0))]
    out_shape = [jax.ShapeDtypeStruct((t, 256), F32)]
    aliases = {}
    if has_s0:
        ins.append(s0_bd)
        in_specs.append(pl.BlockSpec((nb, 2, 256, 256), lambda i: (i, 0, 0, 0)))
    else:
        aliases[len(ins)] = 1
        ins.append(state_out)
        in_specs.append(pl.BlockSpec(memory_space=pl.ANY))
        out_specs.append(pl.BlockSpec((nb, None, 2, 256, HEAD_W), lambda i: (i, l, 0, 0, 0)))
        out_shape.append(jax.ShapeDtypeStruct(state_out.shape, state_out.dtype))
    return pl.pallas_call(
        functools.partial(_ret_kernel, nb, nc, has_s0),
        grid=(b // nb,),
        in_specs=in_specs,
        out_specs=out_specs,
        out_shape=out_shape,
        input_output_aliases=aliases,
        scratch_shapes=[pltpu.VMEM((nb, 256, 256), F32), pltpu.VMEM((nb * nc, 256, 256), F32),
                        pltpu.VMEM((nb * n, 256), F32), pltpu.VMEM((nb * n, 256), F32)],
        compiler_params=_params(1),
        name="ret",
    )(*ins)


ATT_TQ = 256


def _attn_kernel(nparts, nb, qt_ref, *refs):
    k_refs = refs[0:2 * nparts:2]
    vt_refs = refs[1:2 * nparts:2]
    o_ref = refs[2 * nparts]
    n_own = k_refs[-1].shape[0] // nb

    def operands(sq, hh):
        sl = slice(hh * LANES, (hh + 1) * LANES)
        vsl = slice(hh * HEAD_W, (hh + 1) * HEAD_W)
        own = slice(sq * n_own, (sq + 1) * n_own)
        parts = [(k_ref[:, sl], vt_ref[vsl, :]) for k_ref, vt_ref in zip(k_refs[:-1], vt_refs[:-1])]
        return parts + [(k_refs[-1][own, sl], vt_refs[-1][vsl, own])]

    work = [(sq, hh) for sq in range(nb) for hh in range(HEADS)]
    ss_all = [[_dot(kk, qt_ref[hh * LANES:(hh + 1) * LANES, sq * ATT_TQ:(sq + 1) * ATT_TQ])
               for kk, _ in operands(sq, hh)] for sq, hh in work]
    outs = []
    for (sq, hh), ss in zip(work, ss_all):
        m = ss[0].max(axis=0, keepdims=True)
        for s in ss[1:]:
            m = jnp.maximum(m, s.max(axis=0, keepdims=True))
        es = [jnp.exp(s - m) for s in ss]
        den = es[0].sum(axis=0, keepdims=True)
        for e in es[1:]:
            den = den + e.sum(axis=0, keepdims=True)
        oh = None
        for e, (_, vv) in zip(es, operands(sq, hh)):
            part = _dot(vv, e.astype(BF16))
            oh = part if oh is None else oh + part
        outs.append(oh * (1.0 / den))
    for sq in range(nb):
        o_ref[sq * ATT_TQ:(sq + 1) * ATT_TQ, :] = jnp.concatenate(outs[sq * HEADS:(sq + 1) * HEADS], axis=0).T


def _attn_call(l, qt, k, vt, cache, n_seq_tokens):
    t = k.shape[0]
    n = n_seq_tokens
    tq = ATT_TQ
    nq = n // tq
    b = t // n
    nb = min(b, 4) if (cache is None and nq == 1) else 1
    ins = [qt]
    in_specs = [pl.BlockSpec((512, nb * tq), lambda i, j: (0, i * nq + j))]
    if cache is not None:
        kc, vtc = cache
        m = kc.shape[2]
        ins += [kc, vtc]
        in_specs += [pl.BlockSpec((None, None, m, 512), lambda i, j: (l, i, 0, 0)),
                     pl.BlockSpec((None, None, 256, m), lambda i, j: (l, i, 0, 0))]
    ins += [k, vt]
    in_specs += [pl.BlockSpec((nb * n, 512), lambda i, j: (i, 0)), pl.BlockSpec((256, nb * n), lambda i, j: (0, i))]
    return pl.pallas_call(
        functools.partial(_attn_kernel, len(ins) // 2, nb),
        grid=(b // nb, nq),
        in_specs=in_specs,
        out_specs=pl.BlockSpec((nb * tq, 256), lambda i, j: (i * nq + j, 0)),
        out_shape=jax.ShapeDtypeStruct((t, 256), F32),
        compiler_params=_params(2),
        name="attn",
    )(*ins)


def _outproj_kernel(ya_ref, yb_ref, yc_ref, yd_ref, x_ref, mod_ref, wo_ref, g2_ref, rw_ref,
                    x1_ref, h2_ref, aff_ref, wb_ref):
    @pl.when(pl.program_id(0) == 0)
    def _():
        for rb in range(4):
            wb_ref[rb * 256:(rb + 1) * 256, :] = wo_ref[rb * 256:(rb + 1) * 256, :].astype(BF16)

    mix = _dot(ya_ref[...].astype(BF16), wb_ref[0:256, :])
    mix = mix + _dot(yb_ref[...].astype(BF16), wb_ref[256:512, :])
    mix = mix + _dot(yc_ref[...].astype(BF16), wb_ref[512:768, :])
    mix = mix + _dot(yd_ref[...].astype(BF16), wb_ref[768:1024, :])
    mod = mod_ref[...]
    x1 = x_ref[...] + mod[:, 2 * D:3 * D] * mix
    x1_ref[...] = x1
    h2 = _rms(x1) * g2_ref[...]
    h2 = h2 * (1.0 + mod[:, 4 * D:5 * D]) + mod[:, 3 * D:4 * D]
    h_hi, h_lo = _split_bf16(h2)
    h2_ref[...] = h_hi
    both = _dot(h_hi, rw_ref[...])
    logits = both[:, 0:LANES] + both[:, LANES:2 * LANES] + _dot(h_lo, rw_ref[:, 0:LANES])
    lane = _iota((1, LANES), 1)
    logits = jnp.where(lane < N_EXPERTS, logits, -1e30)
    e = jnp.exp(logits - logits.max(axis=-1, keepdims=True))
    aff_ref[...] = e / e.sum(axis=-1, keepdims=True)


def _outproj_call(l, ys, x, mod_p, n_seq_tokens, W, w_out):
    t = x.shape[0]
    tm = 256
    tps = n_seq_tokens // tm
    bm = mod_p.shape[0]
    row = lambda i: (i, 0)
    ins = list(ys) + [x, mod_p, w_out, W["norm2_g"], W["r_hilo"]]
    in_specs = [pl.BlockSpec((tm, 256), row)] * 4 + [
        pl.BlockSpec((tm, D), row),
        pl.BlockSpec((None, 1, 6 * D), (lambda i: (i // tps, 0, 0)) if bm > 1 else (lambda i: (0, 0, 0))),
    ] + [_wspec(a, l) for a in ins[6:]]
    return pl.pallas_call(
        _outproj_kernel,
        grid=(t // tm,),
        in_specs=in_specs,
        out_specs=[pl.BlockSpec((tm, D), row), pl.BlockSpec((tm, D), row), pl.BlockSpec((tm, LANES), row)],
        out_shape=[jax.ShapeDtypeStruct((t, D), F32), jax.ShapeDtypeStruct((t, D), BF16),
                   jax.ShapeDtypeStruct((t, LANES), F32)],
        scratch_shapes=[pltpu.VMEM((D, D), BF16)],
        compiler_params=_params(1, "arbitrary"),
        name="out_proj",
    )(*ins)


GROUP_ROWS = 512
CUM_BLK = 256


def _route_kernel(n, cap, nseq, aff_ref, h2_ref, slot_ref, xe_ref, gs_ref, cum_ref, slot_t_ref):
    lane_row = _iota((1, LANES), 1)
    a = aff_ref[0:n, :]
    for s in range(1, nseq):
        a = a + pltpu.roll(aff_ref[s * n:(s + 1) * n, :], N_EXPERTS * s, 1)
    capf = jnp.float32(cap)
    used = N_EXPERTS * nseq
    fold = LANES // used
    folded = a[0:n // fold, :]
    for k in range(1, fold):
        folded = folded + pltpu.roll(a[k * (n // fold):(k + 1) * (n // fold), :], used * k, 1)

    def bisect(_, lohi):
        lo, hi = lohi
        mid = lo + lax.shift_right_logical(hi - lo, 1)
        cnt = jnp.sum(jnp.where(folded >= pltpu.bitcast(mid, F32), 1.0, 0.0), axis=0, keepdims=True)
        width = used
        while width < LANES:
            cnt = cnt + pltpu.roll(cnt, width, 1)
            width *= 2
        ok = cnt >= capf
        return jnp.where(ok, mid, lo), jnp.where(ok, hi, mid)

    lo0 = jnp.zeros((1, LANES), jnp.int32)
    hi0 = jnp.full((1, LANES), 0x3F800001, jnp.int32)
    lo, hi = lax.fori_loop(0, 31, bisect, (lo0, hi0))
    gt = a >= pltpu.bitcast(hi, F32)
    eq = (a >= pltpu.bitcast(lo, F32)) & jnp.logical_not(gt)
    need = capf - jnp.sum(jnp.where(gt, 1.0, 0.0), axis=0, keepdims=True)
    tri = jnp.where(_iota((CUM_BLK, CUM_BLK), 0) >= _iota((CUM_BLK, CUM_BLK), 1), 1.0, 0.0).astype(BF16)

    def cumsum_rows(flags):
        carry = jnp.zeros((1, LANES), F32)
        for rb in range(n // CUM_BLK):
            rows = slice(rb * CUM_BLK, (rb + 1) * CUM_BLK)
            part = _dot(tri, flags[rows, :].astype(BF16)) + carry
            cum_ref[rows, :] = part
            carry = part[CUM_BLK - 1:CUM_BLK, :]
        return cum_ref[...]

    eq_rank = cumsum_rows(jnp.where(eq, 1.0, 0.0))
    sel = gt | (eq & (eq_rank <= need))
    pos = cumsum_rows(jnp.where(sel, 1.0, 0.0))
    slot = jnp.where(sel, pos - 1.0, -1.0)
    for s in range(nseq):
        own = slot if s == 0 else pltpu.roll(slot, LANES - N_EXPERTS * s, 1)
        slot_ref[s * n:(s + 1) * n, :] = jnp.where(lane_row < N_EXPERTS, own, -1.0)
    slot_t_ref[...] = slot.T
    a_hi, a_lo = _split_bf16(a)
    a_hilo = jnp.concatenate([a_hi, a_lo], axis=1)
    ones = jnp.ones((LANES, LANES), BF16)
    gexp = GROUP_ROWS // cap
    shift = int(np.log2(cap))
    row_e = lax.shift_right_logical(_iota((GROUP_ROWS, LANES), 0), shift)
    slot_id = _iota((cap, n), 0).astype(F32)
    lane = _iota((GROUP_ROWS, LANES), 1)
    def build_onehot(s, g):
        pieces = []
        for j in range(gexp):
            e_lane = N_EXPERTS * s + g * gexp + j
            mine_row = jnp.broadcast_to(slot_t_ref[e_lane:e_lane + 1, :], (cap, n))
            pieces.append(jnp.where(mine_row == slot_id, 1.0, 0.0).astype(BF16))
        return pieces[0] if gexp == 1 else jnp.concatenate(pieces, axis=0)

    items = [(s, g) for s in range(nseq) for g in range(N_EXPERTS * cap // GROUP_ROWS)]
    onehot_next = build_onehot(*items[0])
    for idx, (s, g) in enumerate(items):
        onehot = onehot_next
        if idx + 1 < len(items):
            onehot_next = build_onehot(*items[idx + 1])
        xe = _dot(onehot, h2_ref[s * n:(s + 1) * n, :]).astype(BF16)
        gboth = _dot(onehot, a_hilo)
        mine = lane == row_e + (g * gexp + N_EXPERTS * s)
        g_hi, g_lo = _split_bf16(jnp.where(mine, gboth[:, 0:LANES] + gboth[:, LANES:2 * LANES], 0.0))
        gsb = _dot(g_hi, ones) + _dot(g_lo, ones)
        for j in range(gexp):
            xe_ref[g * gexp + j, s * cap:(s + 1) * cap, :] = xe[j * cap:(j + 1) * cap, :]
            gs_ref[g * gexp + j, s * cap:(s + 1) * cap, :] = gsb[j * cap:(j + 1) * cap, :]


def _route_call(aff, h2, n_seq_tokens):
    t = aff.shape[0]
    n = n_seq_tokens
    b = t // n
    cap = EC_CAPACITY * n // N_EXPERTS
    nseq = min(b, LANES // N_EXPERTS, max(1, 2048 // n))
    return pl.pallas_call(
        functools.partial(_route_kernel, n, cap, nseq),
        grid=(b // nseq,),
        in_specs=[pl.BlockSpec((nseq * n, LANES), lambda i: (i, 0)), pl.BlockSpec((nseq * n, D), lambda i: (i, 0))],
        out_specs=[pl.BlockSpec((nseq * n, LANES), lambda i: (i, 0)),
                   pl.BlockSpec((N_EXPERTS, nseq * cap, D), lambda i: (0, i, 0)),
                   pl.BlockSpec((N_EXPERTS, nseq * cap, LANES), lambda i: (0, i, 0))],
        out_shape=[jax.ShapeDtypeStruct((t, LANES), F32),
                   jax.ShapeDtypeStruct((N_EXPERTS, b * cap, D), BF16),
                   jax.ShapeDtypeStruct((N_EXPERTS, b * cap, LANES), F32)],
        scratch_shapes=[pltpu.VMEM((n, LANES), F32), pltpu.VMEM((LANES, n), F32)],
        compiler_params=_params(1),
        name="route",
    )(aff, h2)


FF_TILE = 512
FFN_ROWS = 512


def _ffn_kernel(xc_ref, xl_ref, gc_ref, gl_ref, wg_ref, wu_ref, wd_ref, yc_ref, yl_ref, accc_ref, accl_ref):
    @pl.when(pl.program_id(1) == 0)
    def _():
        accc_ref[...] = jnp.zeros(accc_ref.shape, F32)
        accl_ref[...] = jnp.zeros(accl_ref.shape, F32)

    wg = wg_ref[...].astype(BF16)
    wu = wu_ref[...].astype(BF16)
    chunks = ([(xc_ref, gc_ref, yc_ref, accc_ref, r0) for r0 in range(0, xc_ref.shape[0], FFN_ROWS)]
              + [(xl_ref, gl_ref, yl_ref, accl_ref, r0) for r0 in range(0, xl_ref.shape[0], FFN_ROWS)])

    def gate_up(idx):
        x_ref, _, _, _, r0 = chunks[idx]
        x = x_ref[r0:r0 + FFN_ROWS, :]
        return _dot(x, wg), _dot(x, wu)

    nxt = gate_up(0)
    wd = wd_ref[...].astype(BF16)
    for idx, (x_ref, g_ref, y_ref, acc_ref, r0) in enumerate(chunks):
        a, up = nxt
        if idx + 1 < len(chunks):
            nxt = gate_up(idx + 1)
        rows = slice(r0, r0 + FFN_ROWS)
        contrib = _dot((_silu(a) * up).astype(BF16), wd)
        total = acc_ref[rows, :] + contrib
        acc_ref[rows, :] = total
        gate = jnp.concatenate([g_ref[rows, :]] * (D // LANES), axis=1)
        y_ref[rows, :] = (total * gate).astype(BF16)


def _ffn_call(l, xe_c, xe_l, gs_c, gs_l, w_gate, w_up, w_down):
    rc = xe_c.shape[1]
    rl = xe_l.shape[1]
    ex = lambda r, w: pl.BlockSpec((None, r, w), lambda e, f: (e, 0, 0))
    return pl.pallas_call(
        _ffn_kernel,
        grid=(N_EXPERTS, FF // FF_TILE),
        in_specs=[
            ex(rc, D), ex(rl, D), ex(rc, LANES), ex(rl, LANES),
            pl.BlockSpec((None, None, D, FF_TILE), lambda e, f: (l, e, 0, f)),
            pl.BlockSpec((None, None, D, FF_TILE), lambda e, f: (l, e, 0, f)),
            pl.BlockSpec((None, None, FF_TILE, D), lambda e, f: (l, e, f, 0)),
        ],
        out_specs=[ex(rc, D), ex(rl, D)],
        out_shape=[jax.ShapeDtypeStruct((N_EXPERTS, rc, D), BF16), jax.ShapeDtypeStruct((N_EXPERTS, rl, D), BF16)],
        scratch_shapes=[pltpu.VMEM((rc, D), F32), pltpu.VMEM((rl, D), F32)],
        compiler_params=pltpu.CompilerParams(dimension_semantics=("parallel", "arbitrary"),
                                             vmem_limit_bytes=VMEM_LIMIT),
        name="ffn",
    )(xe_c, xe_l, gs_c, gs_l, w_gate, w_up, w_down)


COMB_ROWS = 256


def _combine_kernel(cap, final, ye_ref, slot_ref, x1_ref, mod_ref, *rest):
    if final:
        fg_ref, x2_ref, yf_ref = rest
    else:
        (x2_ref,) = rest
    gexp = GROUP_ROWS // cap
    shift = int(np.log2(cap))
    col_e = lax.shift_right_logical(_iota((LANES, GROUP_ROWS), 1), shift)
    col_s = (_iota((1, GROUP_ROWS), 1) & (cap - 1)).astype(F32)
    lane_e = _iota((LANES, GROUP_ROWS), 0)
    sb = slot_ref[...].astype(BF16)

    def build_onehot_t(g):
        expand_m = jnp.where(lane_e == col_e + g * gexp, 1.0, 0.0).astype(BF16)
        return jnp.where(_dot(sb, expand_m) == col_s, 1.0, 0.0).astype(BF16)

    ngroups = N_EXPERTS * cap // GROUP_ROWS
    acc = jnp.zeros((COMB_ROWS, D), F32)
    onehot_next = build_onehot_t(0)
    for g in range(ngroups):
        onehot_t = onehot_next
        if g + 1 < ngroups:
            onehot_next = build_onehot_t(g + 1)
        if cap % LANES == 0:
            for j in range(gexp):
                acc = acc + _dot(onehot_t[:, j * cap:(j + 1) * cap], ye_ref[g * gexp + j])
        else:
            ye = jnp.concatenate([ye_ref[g * gexp + j] for j in range(gexp)], axis=0)
            acc = acc + _dot(onehot_t, ye)
    mod = mod_ref[...]
    x2 = x1_ref[...] + mod[:, 5 * D:6 * D] * acc
    x2_ref[...] = x2
    if final:
        yf_ref[...] = _rms(x2) * fg_ref[...]


def _combine_call(ye, slot, x1, mod_p, n_seq_tokens, final_g):
    t = x1.shape[0]
    n = n_seq_tokens
    b = t // n
    nr = n // COMB_ROWS
    cap = EC_CAPACITY * n // N_EXPERTS
    bm = mod_p.shape[0]
    final = final_g is not None
    row = lambda i, j: (i * nr + j, 0)
    ins = [ye, slot, x1, mod_p]
    in_specs = [
        pl.BlockSpec((N_EXPERTS, cap, D), lambda i, j: (0, i, 0)),
        pl.BlockSpec((COMB_ROWS, LANES), row),
        pl.BlockSpec((COMB_ROWS, D), row),
        pl.BlockSpec((None, 1, 6 * D), (lambda i, j: (i, 0, 0)) if bm > 1 else (lambda i, j: (0, 0, 0))),
    ]
    out_shape = [jax.ShapeDtypeStruct((t, D), F32)]
    out_specs = [pl.BlockSpec((COMB_ROWS, D), row)]
    if final:
        ins.append(final_g)
        in_specs.append(pl.BlockSpec((1, D), lambda i, j: (0, 0)))
        out_shape.append(jax.ShapeDtypeStruct((t, D), F32))
        out_specs.append(pl.BlockSpec((COMB_ROWS, D), row))
    return pl.pallas_call(
        functools.partial(_combine_kernel, cap, final),
        grid=(b, nr),
        in_specs=in_specs,
        out_specs=out_specs,
        out_shape=out_shape,
        compiler_params=_params(2),
        name="combine",
    )(*ins)


def _rot_cols(w):
    a, b, c, d = (w[..., 8 * i:8 * (i + 1)] for i in range(4))
    return jnp.concatenate([-b, a, -d, c], axis=-1)


def _rope_tables(n):
    rows = n // GRID_W
    row = jnp.repeat(jnp.arange(rows, dtype=F32), GRID_W)
    col = jnp.tile(jnp.arange(GRID_W, dtype=F32), rows)
    inv = ROPE_THETA ** (-jnp.arange(0, ROPE // 2, 2, dtype=F32) / (ROPE // 2))
    ra = row[:, None] * inv
    ca = col[:, None] * inv
    ang = jnp.concatenate([ra, ra, ca, ca], axis=-1)
    cos, sin = jnp.cos(ang), jnp.sin(ang)
    z32, z64, z96 = (jnp.zeros((n, w), F32) for w in (32, 64, 96))
    cq = jnp.concatenate([jnp.ones((n, 64), F32), cos, z32], axis=1)
    sq = jnp.concatenate([z64, sin, z32], axis=1)
    ck = jnp.concatenate([z64, cos, z32], axis=1)
    sk = jnp.concatenate([z96, sin], axis=1)
    return cq, sq, ck, sk


def _retention_tables(p_f, p_b):
    pos = jnp.arange(CHUNK, dtype=F32)
    diff = pos[:, None] - pos[None, :]

    def one(p, backward):
        lg = jnp.log1p(-jnp.exp2(p.astype(F32)))[:, :, None]
        dd = -diff if backward else diff
        intra = jnp.where(dd >= 0, jnp.exp(jnp.maximum(dd, 0.0) * lg[..., None]), 0.0)
        qexp = (CHUNK - pos) if backward else (pos + 1.0)
        kexp = pos if backward else (CHUNK - 1.0 - pos)
        qd = jnp.exp(qexp * lg)
        kd = jnp.exp(kexp * lg)
        cd = jnp.exp(CHUNK * lg)
        nl = p.shape[0]
        intra_w = jnp.swapaxes(intra, 1, 2).reshape(nl, CHUNK, HEADS * CHUNK)
        qd_w = jnp.repeat(jnp.swapaxes(qd, 1, 2), HEAD_W, axis=2)
        kd_t = jnp.repeat(kd, HEAD_W, axis=1)
        cdw = jnp.broadcast_to(jnp.repeat(cd, HEAD_W, axis=1), (nl, 256, 256))
        return intra_w, qd_w, kd_t, cdw

    f = one(p_f, False)
    b = one(p_b, True)
    return tuple(jnp.stack([x, y], axis=1) for x, y in zip(f, b))


def _prepare(norm1_g, w_in_t, sg_norm_g, sg_w, sg_b, conv_w, conv_b, conv_ln_g, conv_ln_b, conv_pw, ret_decay_f,
             ret_decay_b, ret_gn_g, q_norm_g, w_uq, kv_norm_g, w_ukv, norm2_g, router):
    L = DEPTH
    row = lambda a: a.reshape(L, 1, -1)
    tail_src = w_in_t[:, MAIN_W:, :]
    kpe_rows = tail_src[:, Q_LORA + KV_LORA:, :]
    kpe_rot = jnp.swapaxes(_rot_cols(jnp.swapaxes(kpe_rows, 1, 2)), 1, 2)
    w_tail = jnp.concatenate([tail_src[:, Q_LORA:Q_LORA + KV_LORA, :], tail_src[:, :Q_LORA, :], kpe_rows, kpe_rot],
                             axis=1)
    uq = w_uq.reshape(L, Q_LORA, HEADS, HEAD_W + ROPE)
    pad_rows = lambda a: jnp.pad(a, ((0, 0), (0, 256 - Q_LORA), (0, 0)))
    wq = pad_rows(jnp.pad(uq, ((0, 0), (0, 0), (0, 0), (0, LANES - HEAD_W - ROPE))).reshape(L, Q_LORA, 512))
    uq_rot = jnp.pad(_rot_cols(uq[..., HEAD_W:]), ((0, 0), (0, 0), (0, 0), (HEAD_W, LANES - HEAD_W - ROPE)))
    wqr = pad_rows(uq_rot.reshape(L, Q_LORA, 512))
    ukv = w_ukv.reshape(L, KV_LORA, HEADS, 2 * HEAD_W)
    wk = jnp.pad(ukv[..., :HEAD_W], ((0, 0), (0, 0), (0, 0), (0, LANES - HEAD_W))).reshape(L, KV_LORA, 512)
    wv = ukv[..., HEAD_W:].reshape(L, KV_LORA, 256)
    intra, qd, kd, cd = _retention_tables(ret_decay_f, ret_decay_b)
    r_pad = jnp.pad(router, ((0, 0), (0, 0), (0, LANES - N_EXPERTS)))
    r_hi = r_pad.astype(BF16)
    r_lo = (r_pad - r_hi.astype(F32)).astype(BF16)
    return dict(
        norm1_g=row(norm1_g), w_tail=w_tail, kv_g=row(kv_norm_g),
        q_g=jnp.pad(row(q_norm_g), ((0, 0), (0, 0), (0, 256 - Q_LORA))),
        wq=wq.astype(BF16), wqr=wqr.astype(BF16), wk=wk.astype(BF16), wv=wv.astype(BF16),
        sg_g=row(sg_norm_g), sg_w=sg_w.astype(BF16), sg_bm=jnp.repeat(jnp.swapaxes(sg_b, 1, 2), HEAD_W, axis=2),
        conv_w=jnp.pad(conv_w, ((0, 0), (0, 1), (0, 0))), conv_b=row(conv_b), conv_ln_g=row(conv_ln_g),
        conv_ln_b=row(conv_ln_b), conv_pw=conv_pw.astype(BF16),
        ret_intra=intra, ret_qd=qd, ret_kdt=kd, ret_cd=cd, ret_gn_g=row(ret_gn_g),
        norm2_g=row(norm2_g), r_hilo=jnp.concatenate([r_hi, r_lo], axis=2),
    )


def _block_diag_states(state):
    eye = jnp.eye(HEADS, dtype=state.dtype)
    bd = state[:, :, :, :, None, :] * eye[None, None, :, None, :, None]
    return bd.reshape(state.shape[0], 2, HEADS * HEAD_W, HEADS * HEAD_W)


def _mixers(l, x, mod_p, n, W, w_in, w_out, rope_tabs, s0_bd, cache, results):
    outs = _inproj_call(l, x, mod_p, n, W, w_in, rope_tabs, None if results is None else results[:2])
    ya, cv, rqkv, rg, qt, k, vt = outs[:7]
    yb = _conv_call(l, cv, n, W)
    ret_out = _ret_call(l, rqkv, rg, n, W, s0_bd, None if results is None else results[2])
    yd = _attn_call(l, qt, k, vt, cache, n)
    x1, h2, aff = _outproj_call(l, (ya, yb, ret_out[0], yd), x, mod_p, n, W, w_out)
    slot, xe, gs = _route_call(aff, h2, n)
    new_results = None if results is None else (outs[7], outs[8], ret_out[1])
    return x1, slot, xe, gs, new_results


def kernel(x_prompt, x_sample, cache_mla_ckv, cache_mla_kpe, state_ret, c, c_ctx, w_mod, b_mod, norm1_g, w_in, sg_norm_g, sg_w, sg_b, conv_w, conv_b, conv_ln_g, conv_ln_b, conv_pw, ret_decay_f, ret_decay_b, ret_gn_g, q_norm_g, w_uq, kv_norm_g, w_ukv, w_out, norm2_g, router, w_gate, w_up, w_down, final_norm_g):
    bc, nc_tok, _ = x_prompt.shape
    bl, nl_tok, _ = x_sample.shape
    w_in = jnp.swapaxes(w_in, 1, 2)
    W = _prepare(norm1_g, w_in, sg_norm_g, sg_w, sg_b, conv_w, conv_b, conv_ln_g, conv_ln_b, conv_pw, ret_decay_f,
                 ret_decay_b, ret_gn_g, q_norm_g, w_uq, kv_norm_g, w_ukv, norm2_g, router)
    cvec = jnp.zeros((8, D), F32).at[0].set(c_ctx).at[1:1 + bl].set(c)
    mod_all = _mod_call(cvec, w_mod, b_mod)
    rope_tabs = _rope_tables(nl_tok)
    kpe128 = jnp.pad(cache_mla_kpe, ((0, 0), (0, 0), (0, 0), (HEAD_W, LANES - HEAD_W - ROPE)))
    cache = _kvcache_call(cache_mla_ckv, kpe128, W["wk"], W["wv"])
    s0_all = _block_diag_states(jnp.swapaxes(state_ret, 0, 1).reshape(DEPTH * bl, 2, HEADS, HEAD_W, HEAD_W))
    s0_all = s0_all.reshape(DEPTH, bl, 2, 256, 256)
    fg = final_norm_g.reshape(1, D)
    xp = x_prompt.reshape(bc * nc_tok, D)
    xs = x_sample.reshape(bl * nl_tok, D)
    results = (jnp.zeros((bc, DEPTH, nc_tok, KV_LORA), F32), jnp.zeros((bc, DEPTH, nc_tok, ROPE), F32),
               jnp.zeros((bc, DEPTH, 2, HEADS * HEAD_W, HEAD_W), F32))
    yp = ys = None
    for l in range(DEPTH):
        mod_c = mod_all[l, 0:1].reshape(1, 1, 6 * D)
        mod_l = mod_all[l, 1:1 + bl].reshape(bl, 1, 6 * D)
        x1c, slot_c, xe_c, gs_c, results = _mixers(l, xp, mod_c, nc_tok, W, w_in, w_out, None, None, None, results)
        x1l, slot_l, xe_l, gs_l, _ = _mixers(l, xs, mod_l, nl_tok, W, w_in, w_out, rope_tabs, s0_all[l], cache, None)
        ye_c, ye_l = _ffn_call(l, xe_c, xe_l, gs_c, gs_l, w_gate, w_up, w_down)
        last = l == DEPTH - 1
        res_c = _combine_call(ye_c, slot_c, x1c, mod_c, nc_tok, fg if last else None)
        res_l = _combine_call(ye_l, slot_l, x1l, mod_l, nl_tok, fg if last else None)
        xp, xs = res_c[0], res_l[0]
        if last:
            yp, ys = res_c[1], res_l[1]
    ckv_all, kpe_all, ret_all = results
    return (yp.reshape(bc, nc_tok, D), ys.reshape(bl, nl_tok, D), ckv_all, kpe_all,
            ret_all.reshape(bc, DEPTH, 2, HEADS, HEAD_W, HEAD_W))
```

```python
import functools

import jax
import jax.numpy as jnp
import numpy as np
from jax import lax
from jax.experimental import pallas as pl
from jax.experimental.pallas import tpu as pltpu

F32 = jnp.float32
BF16 = jnp.bfloat16

D = 1024
DEPTH = 4
CHUNK = 128
EPS = 1e-6
GRID_W = 64
CONV_K = 31
HEADS = 4
HEAD_W = 64
Q_LORA = 192
KV_LORA = 128
ROPE = 32
ROPE_THETA = 10000.0
N_EXPERTS = 16
FF = 1024
EC_CAPACITY = 2
MAIN_W = 2304
TAIL_W = 384
ATT_SCALE = (HEAD_W + ROPE) ** -0.5
LANES = 128
SUBLANES = 8
VMEM_LIMIT = 56 * 1024 * 1024


def _iota(shape, dim):
    return lax.broadcasted_iota(jnp.int32, shape, dim)


def _dot(a, b):
    return jnp.dot(a, b, preferred_element_type=F32)


def _dot_nt(a, b):
    return lax.dot_general(a, b, (((1,), (1,)), ((), ())), preferred_element_type=F32)


def _split_bf16(x):
    hi = x.astype(BF16)
    lo = (x - hi.astype(F32)).astype(BF16)
    return hi, lo


def _sigmoid(x):
    return 1.0 / (1.0 + jnp.exp(-x))


def _silu(x):
    return x * _sigmoid(x)


def _rms(x):
    return x * lax.rsqrt(jnp.mean(x * x, axis=-1, keepdims=True) + EPS)


def _wspec(arr, l):
    nd = arr.ndim
    return pl.BlockSpec((None,) + tuple(arr.shape[1:]), lambda *_: (l,) + (0,) * (nd - 1))


def _params(n_axes, sem="parallel"):
    return pltpu.CompilerParams(dimension_semantics=(sem,) * n_axes, vmem_limit_bytes=VMEM_LIMIT)


def _mod_kernel(c_ref, w_ref, b_ref, o_ref):
    cv = c_ref[...]
    s_hi, s_lo = _split_bf16(_silu(cv))
    w_hi, w_lo = _split_bf16(w_ref[...])
    o_ref[...] = _dot(s_hi, w_hi) + _dot(s_lo, w_hi) + _dot(s_hi, w_lo) + b_ref[...]


def _mod_call(cvec, w_mod, b_mod):
    nt = 1536
    return pl.pallas_call(
        _mod_kernel,
        grid=(DEPTH, 6 * D // nt),
        in_specs=[
            pl.BlockSpec((8, D), lambda l, j: (0, 0)),
            pl.BlockSpec((None, D, nt), lambda l, j: (l, 0, j)),
            pl.BlockSpec((None, 1, nt), lambda l, j: (l, 0, j)),
        ],
        out_specs=pl.BlockSpec((None, 8, nt), lambda l, j: (l, 0, j)),
        out_shape=jax.ShapeDtypeStruct((DEPTH, 8, 6 * D), F32),
        compiler_params=_params(2),
        name="mod",
    )(cvec, w_mod, b_mod.reshape(DEPTH, 1, 6 * D))


N_MAIN_BLOCKS = MAIN_W // 256
RET_K_BLOCK = 5


def _inproj_kernel(rope, moe_cap, *refs):
    if moe_cap is None:
        x_ref, refs = refs[0], refs[1:]
    else:
        ye_ref, slot_ref, x1_ref, modp_ref = refs[0:4]
        refs = refs[4:]
    (mod_ref, g1_ref, w_ref, wt_ref, kvg_ref, qg_ref, wq_ref, wqr_ref, wk_ref, wv_ref, sgg_ref, sgw_ref,
     sgb_ref) = refs[0:13]
    rest = list(refs[13:])
    wb_ref = rest.pop()
    x2_ref = rest.pop() if moe_cap is not None else None
    if rope:
        cq_ref, sq_ref, ck_ref, sk_ref, ya_ref, cv_ref, rqkv_ref, rg_ref, qt_ref, k_ref, vt_ref = rest
    else:
        _, _, ya_ref, cv_ref, rqkv_ref, rg_ref, qt_ref, k_ref, vt_ref, ckv_ref, kpe_ref = rest

    @pl.when(pl.program_id(0) == 0)
    def _():
        for cb in range(N_MAIN_BLOCKS):
            blk = w_ref[cb * 256:(cb + 1) * 256, :]
            if cb == RET_K_BLOCK:
                blk = blk * (HEAD_W ** -0.5)
            wb_ref[cb * 256:(cb + 1) * 256, :] = blk.astype(BF16)
        wb_ref[MAIN_W:MAIN_W + TAIL_W, :] = wt_ref[...].astype(BF16)

    if moe_cap is None:
        x = x_ref[...]
    else:
        x = x1_ref[...] + modp_ref[...][:, 5 * D:6 * D] * _scatter_rows(moe_cap, ye_ref, slot_ref[...])
        x2_ref[...] = x
    mod = mod_ref[...]
    h = _rms(x) * g1_ref[...]
    h = h * (1.0 + mod[:, D:2 * D]) + mod[:, 0:D]
    hb = h.astype(BF16)
    proj = lambda cb: _dot_nt(hb, wb_ref[cb * 256:(cb + 1) * 256, :])
    v_gate = proj(1)
    tail = _dot_nt(hb, wb_ref[MAIN_W:MAIN_W + TAIL_W, :])
    u = proj(0)
    cv_ref[:, 0:256] = proj(2)
    cv_ref[:, 256:512] = proj(3)
    for j in range(3):
        rqkv_ref[:, j * 256:(j + 1) * 256] = proj(4 + j).astype(BF16)
    rg_ref[:, 0:256] = proj(7)
    rg_ref[:, 256:512] = proj(8)
    vn = _rms(v_gate) * sgg_ref[...]
    group = lax.shift_right_logical(_iota((1, 256), 1), 6)
    for cidx in range(x.shape[0] // CHUNK):
        rows = slice(cidx * CHUNK, (cidx + 1) * CHUNK)
        acc = jnp.zeros((CHUNK, 256), F32)
        for g in range(HEADS):
            acc = acc + _dot(sgw_ref[g], jnp.where(group == g, vn[rows, :], 0.0).astype(BF16))
        ya_ref[rows, :] = u[rows, :] * (acc + sgb_ref[...])
    ckv_n = _rms(tail[:, 0:128]) * kvg_ref[...]
    c256 = tail[:, 128:384]
    lane256 = _iota((1, 256), 1)
    ms = jnp.sum(jnp.where(lane256 < Q_LORA, c256 * c256, 0.0), axis=-1, keepdims=True) * (1.0 / Q_LORA)
    cqn = (c256 * lax.rsqrt(ms + EPS) * qg_ref[...]).astype(BF16)
    ckvb = ckv_n.astype(BF16)
    q = _dot(cqn, wq_ref[...])
    kn = _dot(ckvb, wk_ref[...])
    vt_ref[...] = _dot(ckvb, wv_ref[...]).T.astype(BF16)
    slab = tail[:, 256:384]
    if rope:
        qr = _dot(cqn, wqr_ref[...])
        kx = slab * ck_ref[...] + pltpu.roll(slab * sk_ref[...], LANES - ROPE, 1)
        q = q * jnp.concatenate([cq_ref[...]] * HEADS, axis=1) + qr * jnp.concatenate([sq_ref[...]] * HEADS, axis=1)
    else:
        lane128 = _iota((1, LANES), 1)
        kx = jnp.where((lane128 >= HEAD_W) & (lane128 < HEAD_W + ROPE), slab, 0.0)
        ckv_ref[...] = ckv_n
        kpe_ref[...] = pltpu.roll(slab, LANES - HEAD_W, 1)[:, 0:ROPE]
    qt_ref[...] = (q * ATT_SCALE).T.astype(BF16)
    for hh in range(HEADS):
        sl = slice(hh * LANES, (hh + 1) * LANES)
        k_ref[:, sl] = (kn[:, sl] + kx).astype(BF16)


def _inproj_call(l, x, mod_p, n_seq_tokens, W, w_in, rope_tabs, cache_out, pending=None):
    tm = COMB_ROWS
    tps = n_seq_tokens // tm
    bm = mod_p.shape[0]
    rope = rope_tabs is not None
    row = lambda i: (i, 0)
    col = lambda i: (0, i)
    mod_spec = lambda: pl.BlockSpec((None, 1, 6 * D), (lambda i: (i // tps, 0, 0)) if bm > 1 else (lambda i: (0, 0, 0)))
    moe_cap = None
    if pending is None:
        t = x.shape[0]
        ins = [x]
        in_specs = [pl.BlockSpec((tm, D), row)]
    else:
        ye, slot, x1, mod_prev = pending
        t = x1.shape[0]
        moe_cap = EC_CAPACITY * n_seq_tokens // N_EXPERTS
        assert tps == 1, "fused scatter expects one sequence per row tile"
        ins = [ye, slot, x1, mod_prev]
        in_specs = [pl.BlockSpec((N_EXPERTS, moe_cap, D), lambda i: (0, i, 0)), pl.BlockSpec((tm, LANES), row),
                    pl.BlockSpec((tm, D), row), mod_spec()]
    rest = [W["w_tail"], W["kv_g"], W["q_g"], W["wq"], W["wqr"], W["wk"], W["wv"], W["sg_g"], W["sg_w"], W["sg_bm"]]
    ins += [mod_p, W["norm1_g"], w_in] + rest
    in_specs += [
        mod_spec(),
        _wspec(W["norm1_g"], l),
        pl.BlockSpec((None, MAIN_W, D), lambda i: (l, 0, 0)),
    ] + [_wspec(a, l) for a in rest]
    out_shape = [
        jax.ShapeDtypeStruct((t, 256), F32), jax.ShapeDtypeStruct((t, 512), F32),
        jax.ShapeDtypeStruct((t, 768), BF16), jax.ShapeDtypeStruct((t, 512), F32),
        jax.ShapeDtypeStruct((512, t), BF16), jax.ShapeDtypeStruct((t, 512), BF16),
        jax.ShapeDtypeStruct((256, t), BF16),
    ]
    out_specs = [pl.BlockSpec((tm, 256), row), pl.BlockSpec((tm, 512), row), pl.BlockSpec((tm, 768), row),
                 pl.BlockSpec((tm, 512), row), pl.BlockSpec((512, tm), col), pl.BlockSpec((tm, 512), row),
                 pl.BlockSpec((256, tm), col)]
    aliases = {}
    if rope:
        ins += list(rope_tabs)
        in_specs += [pl.BlockSpec((tm, LANES), lambda i: (i % tps, 0))] * 4
    else:
        for acc in cache_out:
            aliases[len(ins)] = len(out_shape)
            ins.append(acc)
            in_specs.append(pl.BlockSpec(memory_space=pl.ANY))
            out_shape.append(jax.ShapeDtypeStruct(acc.shape, acc.dtype))
            out_specs.append(pl.BlockSpec((None, None, tm, acc.shape[-1]), lambda i: (i // tps, l, i % tps, 0)))
    if pending is not None:
        out_shape.append(jax.ShapeDtypeStruct((t, D), F32))
        out_specs.append(pl.BlockSpec((tm, D), row))
    return pl.pallas_call(
        functools.partial(_inproj_kernel, rope, moe_cap),
        grid=(t // tm,),
        in_specs=in_specs,
        out_specs=out_specs,
        out_shape=out_shape,
        input_output_aliases=aliases,
        scratch_shapes=[pltpu.VMEM((MAIN_W + TAIL_W, D), BF16)],
        compiler_params=_params(1, "arbitrary"),
        name="in_proj",
    )(*ins)


def _kvcache_kernel(ckv_ref, kpe_ref, wk_ref, wv_ref, k_ref, vt_ref):
    cb = ckv_ref[...].astype(BF16)
    kn = _dot(cb, wk_ref[...])
    kx = kpe_ref[...]
    for hh in range(HEADS):
        sl = slice(hh * LANES, (hh + 1) * LANES)
        k_ref[:, sl] = (kn[:, sl] + kx).astype(BF16)
    vt_ref[...] = _dot(cb, wv_ref[...]).T.astype(BF16)


def _kvcache_call(cache_ckv, kpe128, wk, wv):
    b, depth, m, _ = cache_ckv.shape
    blk = lambda w: pl.BlockSpec((None, None, m, w), lambda l, i: (i, l, 0, 0))
    wblk = lambda a: pl.BlockSpec((None,) + tuple(a.shape[1:]), lambda l, i: (l, 0, 0))
    return pl.pallas_call(
        _kvcache_kernel,
        grid=(depth, b),
        in_specs=[blk(128), blk(128), wblk(wk), wblk(wv)],
        out_specs=[pl.BlockSpec((None, None, m, 512), lambda l, i: (l, i, 0, 0)),
                   pl.BlockSpec((None, None, 256, m), lambda l, i: (l, i, 0, 0))],
        out_shape=[jax.ShapeDtypeStruct((depth, b, m, 512), BF16), jax.ShapeDtypeStruct((depth, b, 256, m), BF16)],
        compiler_params=_params(2),
        name="kv_cache",
    )(cache_ckv, kpe128, wk, wv)


HALO = 16


def _conv_kernel(rb, nblk, cur_ref, prev_ref, next_ref, wdw_ref, bdw_ref, lng_ref, lnb_ref, wpw_ref, o_ref, pad_ref,
                 sh_ref):
    i = pl.program_id(0)
    keep_prev = (i % nblk != 0).astype(F32)
    keep_next = (i % nblk != nblk - 1).astype(F32)

    def glu(blk):
        return blk[:, 0:256] * _sigmoid(blk[:, 256:512])

    pad_ref[0:HALO, :] = glu(prev_ref[...]) * keep_prev
    pad_ref[HALO:HALO + rb, :] = glu(cur_ref[...])
    pad_ref[HALO + rb:2 * HALO + rb, :] = glu(next_ref[...]) * keep_next
    span = rb + 2 * HALO - SUBLANES
    for s in range(SUBLANES):
        sh_ref[s] = pad_ref[s:s + span, :]
    off = HALO - CONV_K // 2
    for j in range(rb // CHUNK):
        acc = jnp.zeros((CHUNK, 256), F32)
        for k in range(CONV_K):
            start = j * CHUNK + (off + k) // SUBLANES * SUBLANES
            acc = acc + sh_ref[(off + k) % SUBLANES, start:start + CHUNK, :] * wdw_ref[k:k + 1, :]
        y = acc + bdw_ref[...]
        mu = jnp.mean(y, axis=-1, keepdims=True)
        dlt = y - mu
        var = jnp.mean(dlt * dlt, axis=-1, keepdims=True)
        z = _silu(dlt * lax.rsqrt(var + EPS) * lng_ref[...] + lnb_ref[...])
        o_ref[j * CHUNK:(j + 1) * CHUNK, :] = _dot(z.astype(BF16), wpw_ref[...])


def _conv_call(l, cv, n_seq_tokens, W):
    t = cv.shape[0]
    rb = min(n_seq_tokens, 512)
    nblk = n_seq_tokens // rb
    per = rb // HALO
    last = t // HALO - 1
    ins = [cv, cv, cv, W["conv_w"], W["conv_b"], W["conv_ln_g"], W["conv_ln_b"], W["conv_pw"]]
    return pl.pallas_call(
        functools.partial(_conv_kernel, rb, nblk),
        grid=(t // rb,),
        in_specs=[
            pl.BlockSpec((rb, 512), lambda i: (i, 0)),
            pl.BlockSpec((HALO, 512), lambda i: (jnp.maximum(i * per - 1, 0), 0)),
            pl.BlockSpec((HALO, 512), lambda i: (jnp.minimum((i + 1) * per, last), 0)),
        ] + [_wspec(a, l) for a in ins[3:]],
        out_specs=pl.BlockSpec((rb, 256), lambda i: (i, 0)),
        out_shape=jax.ShapeDtypeStruct((t, 256), F32),
        scratch_shapes=[pltpu.VMEM((rb + 2 * HALO, 256), F32),
                        pltpu.VMEM((SUBLANES, rb + 2 * HALO - SUBLANES, 256), F32)],
        compiler_params=_params(1),
        name="conv",
    )(*ins)


def _ret_kernel(nb, nc, has_s0, rq_ref, rk_ref, rv_ref, gf_ref, gb_ref, intra_ref, qd_ref, kdt_ref, cd_ref, gng_ref,
                *rest):
    if has_s0:
        s0_ref, y_ref, s_ref, kvb_ref, of_ref, ob_ref = rest
        st_ref = None
    else:
        _, y_ref, st_ref, s_ref, kvb_ref, of_ref, ob_ref = rest
    same_head = lax.shift_right_logical(_iota((256, 256), 0), 6) == lax.shift_right_logical(_iota((256, 256), 1), 6)
    avg = jnp.where(same_head, 1.0 / HEAD_W, 0.0).astype(BF16)
    row_head = lax.shift_right_logical(_iota((256, CHUNK), 0), 6)
    lane_head = lax.shift_right_logical(_iota((CHUNK, 256), 1), 6)

    def head_mean(x):
        hi, lo = _split_bf16(x)
        return _dot(hi, avg) + _dot(lo, avg)

    def rows_of(sq, cidx):
        return pl.ds(pl.multiple_of(sq * nc * CHUNK + cidx * CHUNK, CHUNK), CHUNK)

    def compact(st):
        return st[:, 0:64] + st[:, 64:128] + st[:, 128:192] + st[:, 192:256]

    for sq in range(nb):
        s_ref[sq] = s0_ref[sq, 0] if has_s0 else jnp.zeros((256, 256), F32)

    def fwd_one(sq, cidx):
        rows = rows_of(sq, cidx)
        qb = rq_ref[rows, :]
        vb = rv_ref[rows, :]
        kt = rk_ref[rows, :].astype(F32).T
        kbd = jnp.concatenate([jnp.where(row_head == hh, kt, 0.0).astype(BF16) for hh in range(HEADS)], axis=1)
        s = _dot(qb, kbd)
        p = jnp.concatenate([(s * intra_ref[0]).astype(BF16), (s * intra_ref[1]).astype(BF16)], axis=0)
        zero = jnp.zeros_like(vb)
        vbd = jnp.concatenate([jnp.where(lane_head == hh, vb, zero) for hh in range(HEADS)], axis=0)
        inner = _dot(p, vbd)
        kts = jnp.concatenate([(kt * kdt_ref[0]).astype(BF16), (kt * kdt_ref[1]).astype(BF16)], axis=0)
        kv = _dot(kts, vb)
        kvb_ref[sq * nc + cidx] = jnp.where(same_head, kv[256:512, :], 0.0)
        ob_ref[rows, :] = inner[CHUNK:2 * CHUNK, :]
        st = s_ref[sq]
        of_ref[rows, :] = inner[0:CHUNK, :] + _dot(qb, st.astype(BF16)) * qd_ref[0]
        s_ref[sq] = cd_ref[0] * st + jnp.where(same_head, kv[0:256, :], 0.0)

    def fwd_body(cidx, carry):
        for sq in range(nb):
            fwd_one(sq, cidx)
        return carry

    lax.fori_loop(0, nc, fwd_body, 0, unroll=2)
    for sq in range(nb):
        if st_ref is not None:
            st_ref[sq, 0] = compact(s_ref[sq])
        s_ref[sq] = s0_ref[sq, 1] if has_s0 else jnp.zeros((256, 256), F32)

    def bwd_body(it, carry):
        cidx = nc - 1 - it
        for sq in range(nb):
            rows = rows_of(sq, cidx)
            st = s_ref[sq]
            ob_ref[rows, :] = ob_ref[rows, :] + _dot(rq_ref[rows, :], st.astype(BF16)) * qd_ref[1]
            s_ref[sq] = cd_ref[1] * st + kvb_ref[sq * nc + cidx]
        return carry

    lax.fori_loop(0, nc, bwd_body, 0, unroll=2)
    if st_ref is not None:
        for sq in range(nb):
            st_ref[sq, 1] = compact(s_ref[sq])

    def norm_body(blk, carry):
        rows = pl.ds(pl.multiple_of(blk * 2 * CHUNK, 2 * CHUNK), 2 * CHUNK)
        o = jnp.concatenate([of_ref[rows, :], ob_ref[rows, :]], axis=0)
        dlt = o - head_mean(o)
        nrm = dlt * lax.rsqrt(head_mean(dlt * dlt) + EPS) * gng_ref[...]
        y_ref[rows, :] = (_silu(gf_ref[rows, :]) * nrm[0:2 * CHUNK, :]
                          + _silu(gb_ref[rows, :]) * nrm[2 * CHUNK:4 * CHUNK, :])
        return carry

    lax.fori_loop(0, nb * nc // 2, norm_body, 0, unroll=2)


RET_BLOCK_ROWS = 4096


def _ret_call(l, rqkv, rg, n_seq_tokens, W, s0_bd, state_out):
    t = rqkv.shape[0]
    n = n_seq_tokens
    b = t // n
    nc = n // CHUNK
    nb = min(b, max(1, RET_BLOCK_ROWS // n), 4)
    has_s0 = s0_bd is not None
    col = lambda j: pl.BlockSpec((nb * n, 256), lambda i: (i, j))
    ins = [rqkv] * 3 + [rg] * 2 + [W["ret_intra"], W["ret_qd"], W["ret_kdt"], W["ret_cd"], W["ret_gn_g"]]
    in_specs = [col(0), col(1), col(2), col(0), col(1)] + [_wspec(a, l) for a in ins[5:]]
    out_specs = [pl.BlockSpec((nb * n, 256), lambda i: (i, 0))]
    out_shape = [jax.ShapeDtypeStruct((t, 256), F32)]
    aliases = {}
    if has_s0:
        ins.append(s0_bd)
        in_specs.append(pl.BlockSpec((nb, 2, 256, 256), lambda i: (i, 0, 0, 0)))
    else:
        aliases[len(ins)] = 1
        ins.append(state_out)
        in_specs.append(pl.BlockSpec(memory_space=pl.ANY))
        out_specs.append(pl.BlockSpec((nb, None, 2, 256, HEAD_W), lambda i: (i, l, 0, 0, 0)))
        out_shape.append(jax.ShapeDtypeStruct(state_out.shape, state_out.dtype))
    return pl.pallas_call(
        functools.partial(_ret_kernel, nb, nc, has_s0),
        grid=(b // nb,),
        in_specs=in_specs,
        out_specs=out_specs,
        out_shape=out_shape,
        input_output_aliases=aliases,
        scratch_shapes=[pltpu.VMEM((nb, 256, 256), F32), pltpu.VMEM((nb * nc, 256, 256), F32),
                        pltpu.VMEM((nb * n, 256), F32), pltpu.VMEM((nb * n, 256), F32)],
        compiler_params=_params(1),
        name="ret",
    )(*ins)


ATT_TQ = 256


def _attn_kernel(nparts, nb, qt_ref, *refs):
    k_refs = refs[0:2 * nparts:2]
    vt_refs = refs[1:2 * nparts:2]
    o_ref = refs[2 * nparts]
    n_own = k_refs[-1].shape[0] // nb

    def operands(sq, hh):
        sl = slice(hh * LANES, (hh + 1) * LANES)
        vsl = slice(hh * HEAD_W, (hh + 1) * HEAD_W)
        own = slice(sq * n_own, (sq + 1) * n_own)
        parts = [(k_ref[:, sl], vt_ref[vsl, :]) for k_ref, vt_ref in zip(k_refs[:-1], vt_refs[:-1])]
        return parts + [(k_refs[-1][own, sl], vt_refs[-1][vsl, own])]

    work = [(sq, hh) for sq in range(nb) for hh in range(HEADS)]
    ss_all = [[_dot(kk, qt_ref[hh * LANES:(hh + 1) * LANES, sq * ATT_TQ:(sq + 1) * ATT_TQ])
               for kk, _ in operands(sq, hh)] for sq, hh in work]
    outs = []
    for (sq, hh), ss in zip(work, ss_all):
        m = ss[0].max(axis=0, keepdims=True)
        for s in ss[1:]:
            m = jnp.maximum(m, s.max(axis=0, keepdims=True))
        es = [jnp.exp(s - m) for s in ss]
        den = es[0].sum(axis=0, keepdims=True)
        for e in es[1:]:
            den = den + e.sum(axis=0, keepdims=True)
        oh = None
        for e, (_, vv) in zip(es, operands(sq, hh)):
            part = _dot(vv, e.astype(BF16))
            oh = part if oh is None else oh + part
        outs.append(oh * (1.0 / den))
    for sq in range(nb):
        o_ref[sq * ATT_TQ:(sq + 1) * ATT_TQ, :] = jnp.concatenate(outs[sq * HEADS:(sq + 1) * HEADS], axis=0).T


def _attn_call(l, qt, k, vt, cache, n_seq_tokens):
    t = k.shape[0]
    n = n_seq_tokens
    tq = ATT_TQ
    nq = n // tq
    b = t // n
    nb = min(b, 4) if (cache is None and nq == 1) else 1
    ins = [qt]
    in_specs = [pl.BlockSpec((512, nb * tq), lambda i, j: (0, i * nq + j))]
    if cache is not None:
        kc, vtc = cache
        m = kc.shape[2]
        ins += [kc, vtc]
        in_specs += [pl.BlockSpec((None, None, m, 512), lambda i, j: (l, i, 0, 0)),
                     pl.BlockSpec((None, None, 256, m), lambda i, j: (l, i, 0, 0))]
    ins += [k, vt]
    in_specs += [pl.BlockSpec((nb * n, 512), lambda i, j: (i, 0)), pl.BlockSpec((256, nb * n), lambda i, j: (0, i))]
    return pl.pallas_call(
        functools.partial(_attn_kernel, len(ins) // 2, nb),
        grid=(b // nb, nq),
        in_specs=in_specs,
        out_specs=pl.BlockSpec((nb * tq, 256), lambda i, j: (i * nq + j, 0)),
        out_shape=jax.ShapeDtypeStruct((t, 256), F32),
        compiler_params=_params(2),
        name="attn",
    )(*ins)


def _outproj_kernel(ya_ref, yb_ref, yc_ref, yd_ref, x_ref, mod_ref, wo_ref, g2_ref, rw_ref,
                    x1_ref, h2_ref, aff_ref, wb_ref):
    @pl.when(pl.program_id(0) == 0)
    def _():
        for rb in range(4):
            wb_ref[rb * 256:(rb + 1) * 256, :] = wo_ref[rb * 256:(rb + 1) * 256, :].astype(BF16)

    mix = _dot(ya_ref[...].astype(BF16), wb_ref[0:256, :])
    mix = mix + _dot(yb_ref[...].astype(BF16), wb_ref[256:512, :])
    mix = mix + _dot(yc_ref[...].astype(BF16), wb_ref[512:768, :])
    mix = mix + _dot(yd_ref[...].astype(BF16), wb_ref[768:1024, :])
    mod = mod_ref[...]
    x1 = x_ref[...] + mod[:, 2 * D:3 * D] * mix
    x1_ref[...] = x1
    h2 = _rms(x1) * g2_ref[...]
    h2 = h2 * (1.0 + mod[:, 4 * D:5 * D]) + mod[:, 3 * D:4 * D]
    h_hi, h_lo = _split_bf16(h2)
    h2_ref[...] = h_hi
    both = _dot(h_hi, rw_ref[...])
    logits = both[:, 0:LANES] + both[:, LANES:2 * LANES] + _dot(h_lo, rw_ref[:, 0:LANES])
    lane = _iota((1, LANES), 1)
    logits = jnp.where(lane < N_EXPERTS, logits, -1e30)
    e = jnp.exp(logits - logits.max(axis=-1, keepdims=True))
    aff_ref[...] = e / e.sum(axis=-1, keepdims=True)


def _outproj_call(l, ys, x, mod_p, n_seq_tokens, W, w_out):
    t = x.shape[0]
    tm = 256
    tps = n_seq_tokens // tm
    bm = mod_p.shape[0]
    row = lambda i: (i, 0)
    ins = list(ys) + [x, mod_p, w_out, W["norm2_g"], W["r_hilo"]]
    in_specs = [pl.BlockSpec((tm, 256), row)] * 4 + [
        pl.BlockSpec((tm, D), row),
        pl.BlockSpec((None, 1, 6 * D), (lambda i: (i // tps, 0, 0)) if bm > 1 else (lambda i: (0, 0, 0))),
    ] + [_wspec(a, l) for a in ins[6:]]
    return pl.pallas_call(
        _outproj_kernel,
        grid=(t // tm,),
        in_specs=in_specs,
        out_specs=[pl.BlockSpec((tm, D), row), pl.BlockSpec((tm, D), row), pl.BlockSpec((tm, LANES), row)],
        out_shape=[jax.ShapeDtypeStruct((t, D), F32), jax.ShapeDtypeStruct((t, D), BF16),
                   jax.ShapeDtypeStruct((t, LANES), F32)],
        scratch_shapes=[pltpu.VMEM((D, D), BF16)],
        compiler_params=_params(1, "arbitrary"),
        name="out_proj",
    )(*ins)


GROUP_ROWS = 512
CUM_BLK = 256


def _route_kernel(n, cap, nseq, aff_ref, h2_ref, slot_ref, xe_ref, gs_ref, cum_ref, slot_t_ref):
    lane_row = _iota((1, LANES), 1)
    a = aff_ref[0:n, :]
    for s in range(1, nseq):
        a = a + pltpu.roll(aff_ref[s * n:(s + 1) * n, :], N_EXPERTS * s, 1)
    capf = jnp.float32(cap)
    used = N_EXPERTS * nseq
    fold = LANES // used
    folded = a[0:n // fold, :]
    for k in range(1, fold):
        folded = folded + pltpu.roll(a[k * (n // fold):(k + 1) * (n // fold), :], used * k, 1)

    def bisect(_, lohi):
        lo, hi = lohi
        mid = lo + lax.shift_right_logical(hi - lo, 1)
        cnt = jnp.sum(jnp.where(folded >= pltpu.bitcast(mid, F32), 1.0, 0.0), axis=0, keepdims=True)
        width = used
        while width < LANES:
            cnt = cnt + pltpu.roll(cnt, width, 1)
            width *= 2
        ok = cnt >= capf
        return jnp.where(ok, mid, lo), jnp.where(ok, hi, mid)

    lo0 = jnp.zeros((1, LANES), jnp.int32)
    hi0 = jnp.full((1, LANES), 0x3F800001, jnp.int32)
    lo, hi = lax.fori_loop(0, 31, bisect, (lo0, hi0))
    gt = a >= pltpu.bitcast(hi, F32)
    eq = (a >= pltpu.bitcast(lo, F32)) & jnp.logical_not(gt)
    need = capf - jnp.sum(jnp.where(gt, 1.0, 0.0), axis=0, keepdims=True)
    tri = jnp.where(_iota((CUM_BLK, CUM_BLK), 0) >= _iota((CUM_BLK, CUM_BLK), 1), 1.0, 0.0).astype(BF16)

    def cumsum_rows(flags):
        carry = jnp.zeros((1, LANES), F32)
        for rb in range(n // CUM_BLK):
            rows = slice(rb * CUM_BLK, (rb + 1) * CUM_BLK)
            part = _dot(tri, flags[rows, :].astype(BF16)) + carry
            cum_ref[rows, :] = part
            carry = part[CUM_BLK - 1:CUM_BLK, :]
        return cum_ref[...]

    eq_rank = cumsum_rows(jnp.where(eq, 1.0, 0.0))
    sel = gt | (eq & (eq_rank <= need))
    pos = cumsum_rows(jnp.where(sel, 1.0, 0.0))
    slot = jnp.where(sel, pos - 1.0, -1.0)
    for s in range(nseq):
        own = slot if s == 0 else pltpu.roll(slot, LANES - N_EXPERTS * s, 1)
        slot_ref[s * n:(s + 1) * n, :] = jnp.where(lane_row < N_EXPERTS, own, -1.0)
    slot_t_ref[...] = slot.T
    a_hi, a_lo = _split_bf16(a)
    a_hilo = jnp.concatenate([a_hi, a_lo], axis=1)
    ones = jnp.ones((LANES, LANES), BF16)
    gexp = GROUP_ROWS // cap
    shift = int(np.log2(cap))
    row_e = lax.shift_right_logical(_iota((GROUP_ROWS, LANES), 0), shift)
    slot_id = _iota((cap, n), 0).astype(F32)
    lane = _iota((GROUP_ROWS, LANES), 1)
    def build_onehot(s, g):
        pieces = []
        for j in range(gexp):
            e_lane = N_EXPERTS * s + g * gexp + j
            mine_row = jnp.broadcast_to(slot_t_ref[e_lane:e_lane + 1, :], (cap, n))
            pieces.append(jnp.where(mine_row == slot_id, 1.0, 0.0).astype(BF16))
        return pieces[0] if gexp == 1 else jnp.concatenate(pieces, axis=0)

    items = [(s, g) for s in range(nseq) for g in range(N_EXPERTS * cap // GROUP_ROWS)]
    onehot_next = build_onehot(*items[0])
    for idx, (s, g) in enumerate(items):
        onehot = onehot_next
        if idx + 1 < len(items):
            onehot_next = build_onehot(*items[idx + 1])
        xe = _dot(onehot, h2_ref[s * n:(s + 1) * n, :]).astype(BF16)
        gboth = _dot(onehot, a_hilo)
        mine = lane == row_e + (g * gexp + N_EXPERTS * s)
        g_hi, g_lo = _split_bf16(jnp.where(mine, gboth[:, 0:LANES] + gboth[:, LANES:2 * LANES], 0.0))
        gsb = _dot(g_hi, ones) + _dot(g_lo, ones)
        for j in range(gexp):
            xe_ref[g * gexp + j, s * cap:(s + 1) * cap, :] = xe[j * cap:(j + 1) * cap, :]
            gs_ref[g * gexp + j, s * cap:(s + 1) * cap, :] = gsb[j * cap:(j + 1) * cap, :]


def _route_call(aff, h2, n_seq_tokens):
    t = aff.shape[0]
    n = n_seq_tokens
    b = t // n
    cap = EC_CAPACITY * n // N_EXPERTS
    nseq = min(b, LANES // N_EXPERTS, max(1, 2048 // n))
    return pl.pallas_call(
        functools.partial(_route_kernel, n, cap, nseq),
        grid=(b // nseq,),
        in_specs=[pl.BlockSpec((nseq * n, LANES), lambda i: (i, 0)), pl.BlockSpec((nseq * n, D), lambda i: (i, 0))],
        out_specs=[pl.BlockSpec((nseq * n, LANES), lambda i: (i, 0)),
                   pl.BlockSpec((N_EXPERTS, nseq * cap, D), lambda i: (0, i, 0)),
                   pl.BlockSpec((N_EXPERTS, nseq * cap, LANES), lambda i: (0, i, 0))],
        out_shape=[jax.ShapeDtypeStruct((t, LANES), F32),
                   jax.ShapeDtypeStruct((N_EXPERTS, b * cap, D), BF16),
                   jax.ShapeDtypeStruct((N_EXPERTS, b * cap, LANES), F32)],
        scratch_shapes=[pltpu.VMEM((n, LANES), F32), pltpu.VMEM((LANES, n), F32)],
        compiler_params=_params(1),
        name="route",
    )(aff, h2)


FF_TILE = 512
FFN_ROWS = 512


def _ffn_kernel(xc_ref, xl_ref, gc_ref, gl_ref, wg_ref, wu_ref, wd_ref, yc_ref, yl_ref, accc_ref, accl_ref):
    @pl.when(pl.program_id(1) == 0)
    def _():
        accc_ref[...] = jnp.zeros(accc_ref.shape, F32)
        accl_ref[...] = jnp.zeros(accl_ref.shape, F32)

    wg = wg_ref[...].astype(BF16)
    wu = wu_ref[...].astype(BF16)
    chunks = ([(xc_ref, gc_ref, yc_ref, accc_ref, r0) for r0 in range(0, xc_ref.shape[0], FFN_ROWS)]
              + [(xl_ref, gl_ref, yl_ref, accl_ref, r0) for r0 in range(0, xl_ref.shape[0], FFN_ROWS)])

    def gate_up(idx):
        x_ref, _, _, _, r0 = chunks[idx]
        x = x_ref[r0:r0 + FFN_ROWS, :]
        return _dot(x, wg), _dot(x, wu)

    nxt = gate_up(0)
    wd = wd_ref[...].astype(BF16)
    for idx, (x_ref, g_ref, y_ref, acc_ref, r0) in enumerate(chunks):
        a, up = nxt
        if idx + 1 < len(chunks):
            nxt = gate_up(idx + 1)
        rows = slice(r0, r0 + FFN_ROWS)
        contrib = _dot((_silu(a) * up).astype(BF16), wd)
        total = acc_ref[rows, :] + contrib
        acc_ref[rows, :] = total
        gate = jnp.concatenate([g_ref[rows, :]] * (D // LANES), axis=1)
        y_ref[rows, :] = (total * gate).astype(BF16)


def _ffn_call(l, xe_c, xe_l, gs_c, gs_l, w_gate, w_up, w_down):
    rc = xe_c.shape[1]
    rl = xe_l.shape[1]
    ex = lambda r, w: pl.BlockSpec((None, r, w), lambda e, f: (e, 0, 0))
    return pl.pallas_call(
        _ffn_kernel,
        grid=(N_EXPERTS, FF // FF_TILE),
        in_specs=[
            ex(rc, D), ex(rl, D), ex(rc, LANES), ex(rl, LANES),
            pl.BlockSpec((None, None, D, FF_TILE), lambda e, f: (l, e, 0, f)),
            pl.BlockSpec((None, None, D, FF_TILE), lambda e, f: (l, e, 0, f)),
            pl.BlockSpec((None, None, FF_TILE, D), lambda e, f: (l, e, f, 0)),
        ],
        out_specs=[ex(rc, D), ex(rl, D)],
        out_shape=[jax.ShapeDtypeStruct((N_EXPERTS, rc, D), BF16), jax.ShapeDtypeStruct((N_EXPERTS, rl, D), BF16)],
        scratch_shapes=[pltpu.VMEM((rc, D), F32), pltpu.VMEM((rl, D), F32)],
        compiler_params=pltpu.CompilerParams(dimension_semantics=("parallel", "arbitrary"),
                                             vmem_limit_bytes=VMEM_LIMIT),
        name="ffn",
    )(xe_c, xe_l, gs_c, gs_l, w_gate, w_up, w_down)


COMB_ROWS = 256


def _scatter_rows(cap, ye_ref, slot):
    gexp = GROUP_ROWS // cap
    shift = int(np.log2(cap))
    col_e = lax.shift_right_logical(_iota((LANES, GROUP_ROWS), 1), shift)
    col_s = (_iota((1, GROUP_ROWS), 1) & (cap - 1)).astype(F32)
    lane_e = _iota((LANES, GROUP_ROWS), 0)
    sb = slot.astype(BF16)

    def build_onehot_t(g):
        expand_m = jnp.where(lane_e == col_e + g * gexp, 1.0, 0.0).astype(BF16)
        return jnp.where(_dot(sb, expand_m) == col_s, 1.0, 0.0).astype(BF16)

    ngroups = N_EXPERTS * cap // GROUP_ROWS
    acc = jnp.zeros((COMB_ROWS, D), F32)
    onehot_next = build_onehot_t(0)
    for g in range(ngroups):
        onehot_t = onehot_next
        if g + 1 < ngroups:
            onehot_next = build_onehot_t(g + 1)
        if cap % LANES == 0:
            for j in range(gexp):
                acc = acc + _dot(onehot_t[:, j * cap:(j + 1) * cap], ye_ref[g * gexp + j])
        else:
            ye = jnp.concatenate([ye_ref[g * gexp + j] for j in range(gexp)], axis=0)
            acc = acc + _dot(onehot_t, ye)
    return acc


def _combine_kernel(cap, final, ye_ref, slot_ref, x1_ref, mod_ref, *rest):
    if final:
        fg_ref, x2_ref, yf_ref = rest
    else:
        (x2_ref,) = rest
    mod = mod_ref[...]
    x2 = x1_ref[...] + mod[:, 5 * D:6 * D] * _scatter_rows(cap, ye_ref, slot_ref[...])
    x2_ref[...] = x2
    if final:
        yf_ref[...] = _rms(x2) * fg_ref[...]


def _combine_call(ye, slot, x1, mod_p, n_seq_tokens, final_g):
    t = x1.shape[0]
    n = n_seq_tokens
    b = t // n
    nr = n // COMB_ROWS
    cap = EC_CAPACITY * n // N_EXPERTS
    bm = mod_p.shape[0]
    final = final_g is not None
    row = lambda i, j: (i * nr + j, 0)
    ins = [ye, slot, x1, mod_p]
    in_specs = [
        pl.BlockSpec((N_EXPERTS, cap, D), lambda i, j: (0, i, 0)),
        pl.BlockSpec((COMB_ROWS, LANES), row),
        pl.BlockSpec((COMB_ROWS, D), row),
        pl.BlockSpec((None, 1, 6 * D), (lambda i, j: (i, 0, 0)) if bm > 1 else (lambda i, j: (0, 0, 0))),
    ]
    out_shape = [jax.ShapeDtypeStruct((t, D), F32)]
    out_specs = [pl.BlockSpec((COMB_ROWS, D), row)]
    if final:
        ins.append(final_g)
        in_specs.append(pl.BlockSpec((1, D), lambda i, j: (0, 0)))
        out_shape.append(jax.ShapeDtypeStruct((t, D), F32))
        out_specs.append(pl.BlockSpec((COMB_ROWS, D), row))
    return pl.pallas_call(
        functools.partial(_combine_kernel, cap, final),
        grid=(b, nr),
        in_specs=in_specs,
        out_specs=out_specs,
        out_shape=out_shape,
        compiler_params=_params(2),
        name="combine",
    )(*ins)


def _rot_cols(w):
    a, b, c, d = (w[..., 8 * i:8 * (i + 1)] for i in range(4))
    return jnp.concatenate([-b, a, -d, c], axis=-1)


def _rope_tables(n):
    rows = n // GRID_W
    row = jnp.repeat(jnp.arange(rows, dtype=F32), GRID_W)
    col = jnp.tile(jnp.arange(GRID_W, dtype=F32), rows)
    inv = ROPE_THETA ** (-jnp.arange(0, ROPE // 2, 2, dtype=F32) / (ROPE // 2))
    ra = row[:, None] * inv
    ca = col[:, None] * inv
    ang = jnp.concatenate([ra, ra, ca, ca], axis=-1)
    cos, sin = jnp.cos(ang), jnp.sin(ang)
    z32, z64, z96 = (jnp.zeros((n, w), F32) for w in (32, 64, 96))
    cq = jnp.concatenate([jnp.ones((n, 64), F32), cos, z32], axis=1)
    sq = jnp.concatenate([z64, sin, z32], axis=1)
    ck = jnp.concatenate([z64, cos, z32], axis=1)
    sk = jnp.concatenate([z96, sin], axis=1)
    return cq, sq, ck, sk


def _retention_tables(p_f, p_b):
    pos = jnp.arange(CHUNK, dtype=F32)
    diff = pos[:, None] - pos[None, :]

    def one(p, backward):
        lg = jnp.log1p(-jnp.exp2(p.astype(F32)))[:, :, None]
        dd = -diff if backward else diff
        intra = jnp.where(dd >= 0, jnp.exp(jnp.maximum(dd, 0.0) * lg[..., None]), 0.0)
        qexp = (CHUNK - pos) if backward else (pos + 1.0)
        kexp = pos if backward else (CHUNK - 1.0 - pos)
        qd = jnp.exp(qexp * lg)
        kd = jnp.exp(kexp * lg)
        cd = jnp.exp(CHUNK * lg)
        nl = p.shape[0]
        intra_w = jnp.swapaxes(intra, 1, 2).reshape(nl, CHUNK, HEADS * CHUNK)
        qd_w = jnp.repeat(jnp.swapaxes(qd, 1, 2), HEAD_W, axis=2)
        kd_t = jnp.repeat(kd, HEAD_W, axis=1)
        cdw = jnp.broadcast_to(jnp.repeat(cd, HEAD_W, axis=1), (nl, 256, 256))
        return intra_w, qd_w, kd_t, cdw

    f = one(p_f, False)
    b = one(p_b, True)
    return tuple(jnp.stack([x, y], axis=1) for x, y in zip(f, b))


def _prepare(norm1_g, w_in_t, sg_norm_g, sg_w, sg_b, conv_w, conv_b, conv_ln_g, conv_ln_b, conv_pw, ret_decay_f,
             ret_decay_b, ret_gn_g, q_norm_g, w_uq, kv_norm_g, w_ukv, norm2_g, router):
    L = DEPTH
    row = lambda a: a.reshape(L, 1, -1)
    tail_src = w_in_t[:, MAIN_W:, :]
    kpe_rows = tail_src[:, Q_LORA + KV_LORA:, :]
    kpe_rot = jnp.swapaxes(_rot_cols(jnp.swapaxes(kpe_rows, 1, 2)), 1, 2)
    w_tail = jnp.concatenate([tail_src[:, Q_LORA:Q_LORA + KV_LORA, :], tail_src[:, :Q_LORA, :], kpe_rows, kpe_rot],
                             axis=1)
    uq = w_uq.reshape(L, Q_LORA, HEADS, HEAD_W + ROPE)
    pad_rows = lambda a: jnp.pad(a, ((0, 0), (0, 256 - Q_LORA), (0, 0)))
    wq = pad_rows(jnp.pad(uq, ((0, 0), (0, 0), (0, 0), (0, LANES - HEAD_W - ROPE))).reshape(L, Q_LORA, 512))
    uq_rot = jnp.pad(_rot_cols(uq[..., HEAD_W:]), ((0, 0), (0, 0), (0, 0), (HEAD_W, LANES - HEAD_W - ROPE)))
    wqr = pad_rows(uq_rot.reshape(L, Q_LORA, 512))
    ukv = w_ukv.reshape(L, KV_LORA, HEADS, 2 * HEAD_W)
    wk = jnp.pad(ukv[..., :HEAD_W], ((0, 0), (0, 0), (0, 0), (0, LANES - HEAD_W))).reshape(L, KV_LORA, 512)
    wv = ukv[..., HEAD_W:].reshape(L, KV_LORA, 256)
    intra, qd, kd, cd = _retention_tables(ret_decay_f, ret_decay_b)
    r_pad = jnp.pad(router, ((0, 0), (0, 0), (0, LANES - N_EXPERTS)))
    r_hi = r_pad.astype(BF16)
    r_lo = (r_pad - r_hi.astype(F32)).astype(BF16)
    return dict(
        norm1_g=row(norm1_g), w_tail=w_tail, kv_g=row(kv_norm_g),
        q_g=jnp.pad(row(q_norm_g), ((0, 0), (0, 0), (0, 256 - Q_LORA))),
        wq=wq.astype(BF16), wqr=wqr.astype(BF16), wk=wk.astype(BF16), wv=wv.astype(BF16),
        sg_g=row(sg_norm_g), sg_w=sg_w.astype(BF16), sg_bm=jnp.repeat(jnp.swapaxes(sg_b, 1, 2), HEAD_W, axis=2),
        conv_w=jnp.pad(conv_w, ((0, 0), (0, 1), (0, 0))), conv_b=row(conv_b), conv_ln_g=row(conv_ln_g),
        conv_ln_b=row(conv_ln_b), conv_pw=conv_pw.astype(BF16),
        ret_intra=intra, ret_qd=qd, ret_kdt=kd, ret_cd=cd, ret_gn_g=row(ret_gn_g),
        norm2_g=row(norm2_g), r_hilo=jnp.concatenate([r_hi, r_lo], axis=2),
    )


def _block_diag_states(state):
    eye = jnp.eye(HEADS, dtype=state.dtype)
    bd = state[:, :, :, :, None, :] * eye[None, None, :, None, :, None]
    return bd.reshape(state.shape[0], 2, HEADS * HEAD_W, HEADS * HEAD_W)


def _mixers(l, x, mod_p, n, W, w_in, w_out, rope_tabs, s0_bd, cache, results, pending=None):
    outs = _inproj_call(l, x, mod_p, n, W, w_in, rope_tabs, None if results is None else results[:2], pending)
    if pending is not None:
        x = outs[-1]
    ya, cv, rqkv, rg, qt, k, vt = outs[:7]
    yb = _conv_call(l, cv, n, W)
    ret_out = _ret_call(l, rqkv, rg, n, W, s0_bd, None if results is None else results[2])
    yd = _attn_call(l, qt, k, vt, cache, n)
    x1, h2, aff = _outproj_call(l, (ya, yb, ret_out[0], yd), x, mod_p, n, W, w_out)
    slot, xe, gs = _route_call(aff, h2, n)
    new_results = None if results is None else (outs[7], outs[8], ret_out[1])
    return x1, slot, xe, gs, new_results


def kernel(x_prompt, x_sample, cache_mla_ckv, cache_mla_kpe, state_ret, c, c_ctx, w_mod, b_mod, norm1_g, w_in, sg_norm_g, sg_w, sg_b, conv_w, conv_b, conv_ln_g, conv_ln_b, conv_pw, ret_decay_f, ret_decay_b, ret_gn_g, q_norm_g, w_uq, kv_norm_g, w_ukv, w_out, norm2_g, router, w_gate, w_up, w_down, final_norm_g):
    bc, nc_tok, _ = x_prompt.shape
    bl, nl_tok, _ = x_sample.shape
    w_in = jnp.swapaxes(w_in, 1, 2)
    W = _prepare(norm1_g, w_in, sg_norm_g, sg_w, sg_b, conv_w, conv_b, conv_ln_g, conv_ln_b, conv_pw, ret_decay_f,
                 ret_decay_b, ret_gn_g, q_norm_g, w_uq, kv_norm_g, w_ukv, norm2_g, router)
    cvec = jnp.zeros((8, D), F32).at[0].set(c_ctx).at[1:1 + bl].set(c)
    mod_all = _mod_call(cvec, w_mod, b_mod)
    rope_tabs = _rope_tables(nl_tok)
    kpe128 = jnp.pad(cache_mla_kpe, ((0, 0), (0, 0), (0, 0), (HEAD_W, LANES - HEAD_W - ROPE)))
    cache = _kvcache_call(cache_mla_ckv, kpe128, W["wk"], W["wv"])
    s0_all = _block_diag_states(jnp.swapaxes(state_ret, 0, 1).reshape(DEPTH * bl, 2, HEADS, HEAD_W, HEAD_W))
    s0_all = s0_all.reshape(DEPTH, bl, 2, 256, 256)
    fg = final_norm_g.reshape(1, D)
    xp = x_prompt.reshape(bc * nc_tok, D)
    xs = x_sample.reshape(bl * nl_tok, D)
    results = (jnp.zeros((bc, DEPTH, nc_tok, KV_LORA), F32), jnp.zeros((bc, DEPTH, nc_tok, ROPE), F32),
               jnp.zeros((bc, DEPTH, 2, HEADS * HEAD_W, HEAD_W), F32))
    yp = ys = None
    pending = None
    for l in range(DEPTH):
        mod_c = mod_all[l, 0:1].reshape(1, 1, 6 * D)
        mod_l = mod_all[l, 1:1 + bl].reshape(bl, 1, 6 * D)
        x1c, slot_c, xe_c, gs_c, results = _mixers(l, xp, mod_c, nc_tok, W, w_in, w_out, None, None, None, results,
                                                   pending)
        x1l, slot_l, xe_l, gs_l, _ = _mixers(l, xs, mod_l, nl_tok, W, w_in, w_out, rope_tabs, s0_all[l], cache, None)
        ye_c, ye_l = _ffn_call(l, xe_c, xe_l, gs_c, gs_l, w_gate, w_up, w_down)
        last = l == DEPTH - 1
        res_l = _combine_call(ye_l, slot_l, x1l, mod_l, nl_tok, fg if last else None)
        xs = res_l[0]
        if last:
            yp, ys = _combine_call(ye_c, slot_c, x1c, mod_c, nc_tok, fg)[1], res_l[1]
        else:
            xp, pending = None, (ye_c, slot_c, x1c, mod_c)
    ckv_all, kpe_all, ret_all = results
    return (yp.reshape(bc, nc_tok, D), ys.reshape(bl, nl_tok, D), ckv_all, kpe_all,
            ret_all.reshape(bc, DEPTH, 2, HEADS, HEAD_W, HEAD_W))
```

```python
import functools

import jax
import jax.numpy as jnp
import numpy as np
from jax import lax
from jax.experimental import pallas as pl
from jax.experimental.pallas import tpu as pltpu

F32 = jnp.float32
BF16 = jnp.bfloat16

D = 1024
DEPTH = 4
CHUNK = 128
EPS = 1e-6
GRID_W = 64
CONV_K = 31
HEADS = 4
HEAD_W = 64
Q_LORA = 192
KV_LORA = 128
ROPE = 32
ROPE_THETA = 10000.0
N_EXPERTS = 16
FF = 1024
EC_CAPACITY = 2
MAIN_W = 2304
TAIL_W = 384
ATT_SCALE = (HEAD_W + ROPE) ** -0.5
LANES = 128
SUBLANES = 8
VMEM_LIMIT = 56 * 1024 * 1024


def _iota(shape, dim):
    return lax.broadcasted_iota(jnp.int32, shape, dim)


def _dot(a, b):
    return jnp.dot(a, b, preferred_element_type=F32)


def _dot_nt(a, b):
    return lax.dot_general(a, b, (((1,), (1,)), ((), ())), preferred_element_type=F32)


def _split_bf16(x):
    hi = x.astype(BF16)
    lo = (x - hi.astype(F32)).astype(BF16)
    return hi, lo


def _sigmoid(x):
    return 1.0 / (1.0 + jnp.exp(-x))


def _silu(x):
    return x * _sigmoid(x)


def _rms(x):
    return x * lax.rsqrt(jnp.mean(x * x, axis=-1, keepdims=True) + EPS)


def _wspec(arr, l):
    nd = arr.ndim
    return pl.BlockSpec((None,) + tuple(arr.shape[1:]), lambda *_: (l,) + (0,) * (nd - 1))


def _params(n_axes, sem="parallel"):
    return pltpu.CompilerParams(dimension_semantics=(sem,) * n_axes, vmem_limit_bytes=VMEM_LIMIT)


def _mod_kernel(c_ref, w_ref, b_ref, o_ref):
    cv = c_ref[...]
    s_hi, s_lo = _split_bf16(_silu(cv))
    w_hi, w_lo = _split_bf16(w_ref[...])
    o_ref[...] = _dot(s_hi, w_hi) + _dot(s_lo, w_hi) + _dot(s_hi, w_lo) + b_ref[...]


def _mod_call(cvec, w_mod, b_mod):
    nt = 1536
    return pl.pallas_call(
        _mod_kernel,
        grid=(DEPTH, 6 * D // nt),
        in_specs=[
            pl.BlockSpec((8, D), lambda l, j: (0, 0)),
            pl.BlockSpec((None, D, nt), lambda l, j: (l, 0, j)),
            pl.BlockSpec((None, 1, nt), lambda l, j: (l, 0, j)),
        ],
        out_specs=pl.BlockSpec((None, 8, nt), lambda l, j: (l, 0, j)),
        out_shape=jax.ShapeDtypeStruct((DEPTH, 8, 6 * D), F32),
        compiler_params=_params(2),
        name="mod",
    )(cvec, w_mod, b_mod.reshape(DEPTH, 1, 6 * D))


N_MAIN_BLOCKS = MAIN_W // 256
PROJ_ROWS = 512
RET_K_BLOCK = 5


def _inproj_kernel(rope, moe_cap, *refs):
    if moe_cap is None:
        x_ref, refs = refs[0], refs[1:]
    else:
        ye_ref, slot_ref, x1_ref, modp_ref = refs[0:4]
        refs = refs[4:]
    (mod_ref, g1_ref, w_ref, wt_ref, kvg_ref, qg_ref, wq_ref, wqr_ref, wk_ref, wv_ref, sgg_ref, sgw_ref,
     sgb_ref) = refs[0:13]
    rest = list(refs[13:])
    wb_ref = rest.pop()
    x2_ref = rest.pop() if moe_cap is not None else None
    if rope:
        cq_ref, sq_ref, ck_ref, sk_ref, ya_ref, cv_ref, rqkv_ref, rg_ref, qt_ref, k_ref, vt_ref = rest
    else:
        _, _, ya_ref, cv_ref, rqkv_ref, rg_ref, qt_ref, k_ref, vt_ref, ckv_ref, kpe_ref = rest

    @pl.when(pl.program_id(0) == 0)
    def _():
        for cb in range(N_MAIN_BLOCKS):
            blk = w_ref[cb * 256:(cb + 1) * 256, :]
            if cb == RET_K_BLOCK:
                blk = blk * (HEAD_W ** -0.5)
            wb_ref[cb * 256:(cb + 1) * 256, :] = blk.astype(BF16)
        wb_ref[MAIN_W:MAIN_W + TAIL_W, :] = wt_ref[...].astype(BF16)

    if moe_cap is None:
        x = x_ref[...]
    else:
        moe = [_scatter_rows(moe_cap, ye_ref, slot_ref[sq * COMB_ROWS:(sq + 1) * COMB_ROWS, :], sq)
               for sq in range(x1_ref.shape[0] // COMB_ROWS)]
        x = x1_ref[...] + modp_ref[...][:, 5 * D:6 * D] * (moe[0] if len(moe) == 1 else jnp.concatenate(moe, axis=0))
        x2_ref[...] = x
    mod = mod_ref[...]
    h = _rms(x) * g1_ref[...]
    h = h * (1.0 + mod[:, D:2 * D]) + mod[:, 0:D]
    hb = h.astype(BF16)
    proj = lambda cb: _dot_nt(hb, wb_ref[cb * 256:(cb + 1) * 256, :])
    v_gate = proj(1)
    tail = _dot_nt(hb, wb_ref[MAIN_W:MAIN_W + TAIL_W, :])
    u = proj(0)
    cv_ref[:, 0:256] = proj(2)
    cv_ref[:, 256:512] = proj(3)
    for j in range(3):
        rqkv_ref[:, j * 256:(j + 1) * 256] = proj(4 + j).astype(BF16)
    rg_ref[:, 0:256] = proj(7)
    rg_ref[:, 256:512] = proj(8)
    vn = _rms(v_gate) * sgg_ref[...]
    group = lax.shift_right_logical(_iota((1, 256), 1), 6)
    for cidx in range(x.shape[0] // CHUNK):
        rows = slice(cidx * CHUNK, (cidx + 1) * CHUNK)
        acc = jnp.zeros((CHUNK, 256), F32)
        for g in range(HEADS):
            acc = acc + _dot(sgw_ref[g], jnp.where(group == g, vn[rows, :], 0.0).astype(BF16))
        ya_ref[rows, :] = u[rows, :] * (acc + sgb_ref[...])
    ckv_n = _rms(tail[:, 0:128]) * kvg_ref[...]
    c256 = tail[:, 128:384]
    lane256 = _iota((1, 256), 1)
    ms = jnp.sum(jnp.where(lane256 < Q_LORA, c256 * c256, 0.0), axis=-1, keepdims=True) * (1.0 / Q_LORA)
    cqn = (c256 * lax.rsqrt(ms + EPS) * qg_ref[...]).astype(BF16)
    ckvb = ckv_n.astype(BF16)
    q = _dot(cqn, wq_ref[...])
    kn = _dot(ckvb, wk_ref[...])
    vt_ref[...] = _dot(ckvb, wv_ref[...]).T.astype(BF16)
    slab = tail[:, 256:384]
    if rope:
        qr = _dot(cqn, wqr_ref[...])
        kx = slab * ck_ref[...] + pltpu.roll(slab * sk_ref[...], LANES - ROPE, 1)
        q = q * jnp.concatenate([cq_ref[...]] * HEADS, axis=1) + qr * jnp.concatenate([sq_ref[...]] * HEADS, axis=1)
    else:
        lane128 = _iota((1, LANES), 1)
        kx = jnp.where((lane128 >= HEAD_W) & (lane128 < HEAD_W + ROPE), slab, 0.0)
        kpe = pltpu.roll(slab, LANES - HEAD_W, 1)[:, 0:ROPE]
        n_out = ckv_ref.shape[1]
        for sq in range(ckv_ref.shape[0]):
            ckv_ref[sq] = ckv_n[sq * n_out:(sq + 1) * n_out, :]
            kpe_ref[sq] = kpe[sq * n_out:(sq + 1) * n_out, :]
    qt_ref[...] = (q * ATT_SCALE).T.astype(BF16)
    for hh in range(HEADS):
        sl = slice(hh * LANES, (hh + 1) * LANES)
        k_ref[:, sl] = (kn[:, sl] + kx).astype(BF16)


def _inproj_call(l, x, mod_p, n_seq_tokens, W, w_in, rope_tabs, cache_out, pending=None):
    tm = PROJ_ROWS
    tps = max(1, n_seq_tokens // tm)
    spt = max(1, tm // n_seq_tokens)
    bm = mod_p.shape[0]
    rope = rope_tabs is not None
    row = lambda i: (i, 0)
    col = lambda i: (0, i)
    mod_spec = lambda: pl.BlockSpec((None, 1, 6 * D), (lambda i: (i // tps, 0, 0)) if bm > 1 else (lambda i: (0, 0, 0)))
    moe_cap = None
    if pending is None:
        t = x.shape[0]
        ins = [x]
        in_specs = [pl.BlockSpec((tm, D), row)]
    else:
        ye, slot, x1, mod_prev = pending
        t = x1.shape[0]
        moe_cap = EC_CAPACITY * n_seq_tokens // N_EXPERTS
        assert n_seq_tokens == COMB_ROWS, "fused scatter expects whole sequences of COMB_ROWS tokens per tile"
        ins = [ye, slot, x1, mod_prev]
        in_specs = [pl.BlockSpec((N_EXPERTS, spt * moe_cap, D), lambda i: (0, i, 0)), pl.BlockSpec((tm, LANES), row),
                    pl.BlockSpec((tm, D), row), mod_spec()]
    rest = [W["w_tail"], W["kv_g"], W["q_g"], W["wq"], W["wqr"], W["wk"], W["wv"], W["sg_g"], W["sg_w"], W["sg_bm"]]
    ins += [mod_p, W["norm1_g"], w_in] + rest
    in_specs += [
        mod_spec(),
        _wspec(W["norm1_g"], l),
        pl.BlockSpec((None, MAIN_W, D), lambda i: (l, 0, 0), pipeline_mode=pl.Buffered(1)),
        pl.BlockSpec((None, TAIL_W, D), lambda i: (l, 0, 0), pipeline_mode=pl.Buffered(1)),
    ] + [_wspec(a, l) for a in rest[1:]]
    out_shape = [
        jax.ShapeDtypeStruct((t, 256), F32), jax.ShapeDtypeStruct((t, 512), F32),
        jax.ShapeDtypeStruct((t, 768), BF16), jax.ShapeDtypeStruct((t, 512), F32),
        jax.ShapeDtypeStruct((512, t), BF16), jax.ShapeDtypeStruct((t, 512), BF16),
        jax.ShapeDtypeStruct((256, t), BF16),
    ]
    out_specs = [pl.BlockSpec((tm, 256), row), pl.BlockSpec((tm, 512), row), pl.BlockSpec((tm, 768), row),
                 pl.BlockSpec((tm, 512), row), pl.BlockSpec((512, tm), col), pl.BlockSpec((tm, 512), row),
                 pl.BlockSpec((256, tm), col)]
    aliases = {}
    if rope:
        ins += list(rope_tabs)
        in_specs += [pl.BlockSpec((tm, LANES), lambda i: (i % tps, 0))] * 4
    else:
        for acc in cache_out:
            aliases[len(ins)] = len(out_shape)
            ins.append(acc)
            in_specs.append(pl.BlockSpec(memory_space=pl.ANY))
            out_shape.append(jax.ShapeDtypeStruct(acc.shape, acc.dtype))
            out_specs.append(pl.BlockSpec((spt, None, tm // spt, acc.shape[-1]), lambda i: (i // tps, l, i % tps, 0)))
    if pending is not None:
        out_shape.append(jax.ShapeDtypeStruct((t, D), F32))
        out_specs.append(pl.BlockSpec((tm, D), row))
    return pl.pallas_call(
        functools.partial(_inproj_kernel, rope, moe_cap),
        grid=(t // tm,),
        in_specs=in_specs,
        out_specs=out_specs,
        out_shape=out_shape,
        input_output_aliases=aliases,
        scratch_shapes=[pltpu.VMEM((MAIN_W + TAIL_W, D), BF16)],
        compiler_params=_params(1, "arbitrary"),
        name="in_proj",
    )(*ins)


def _kvcache_kernel(ckv_ref, kpe_ref, wk_ref, wv_ref, k_ref, vt_ref):
    cb = ckv_ref[...].astype(BF16)
    kn = _dot(cb, wk_ref[...])
    kx = kpe_ref[...]
    for hh in range(HEADS):
        sl = slice(hh * LANES, (hh + 1) * LANES)
        k_ref[:, sl] = (kn[:, sl] + kx).astype(BF16)
    vt_ref[...] = _dot(cb, wv_ref[...]).T.astype(BF16)


def _kvcache_call(cache_ckv, kpe128, wk, wv):
    b, depth, m, _ = cache_ckv.shape
    blk = lambda w: pl.BlockSpec((None, None, m, w), lambda l, i: (i, l, 0, 0))
    wblk = lambda a: pl.BlockSpec((None,) + tuple(a.shape[1:]), lambda l, i: (l, 0, 0))
    return pl.pallas_call(
        _kvcache_kernel,
        grid=(depth, b),
        in_specs=[blk(128), blk(128), wblk(wk), wblk(wv)],
        out_specs=[pl.BlockSpec((None, None, m, 512), lambda l, i: (l, i, 0, 0)),
                   pl.BlockSpec((None, None, 256, m), lambda l, i: (l, i, 0, 0))],
        out_shape=[jax.ShapeDtypeStruct((depth, b, m, 512), BF16), jax.ShapeDtypeStruct((depth, b, 256, m), BF16)],
        compiler_params=_params(2),
        name="kv_cache",
    )(cache_ckv, kpe128, wk, wv)


HALO = 16


def _conv_kernel(rb, nblk, cur_ref, prev_ref, next_ref, wdw_ref, bdw_ref, lng_ref, lnb_ref, wpw_ref, o_ref, pad_ref,
                 sh_ref):
    i = pl.program_id(0)
    keep_prev = (i % nblk != 0).astype(F32)
    keep_next = (i % nblk != nblk - 1).astype(F32)

    def glu(blk):
        return blk[:, 0:256] * _sigmoid(blk[:, 256:512])

    pad_ref[0:HALO, :] = glu(prev_ref[...]) * keep_prev
    pad_ref[HALO:HALO + rb, :] = glu(cur_ref[...])
    pad_ref[HALO + rb:2 * HALO + rb, :] = glu(next_ref[...]) * keep_next
    span = rb + 2 * HALO - SUBLANES
    for s in range(SUBLANES):
        sh_ref[s] = pad_ref[s:s + span, :]
    off = HALO - CONV_K // 2
    for j in range(rb // CHUNK):
        acc = jnp.zeros((CHUNK, 256), F32)
        for k in range(CONV_K):
            start = j * CHUNK + (off + k) // SUBLANES * SUBLANES
            acc = acc + sh_ref[(off + k) % SUBLANES, start:start + CHUNK, :] * wdw_ref[k:k + 1, :]
        y = acc + bdw_ref[...]
        mu = jnp.mean(y, axis=-1, keepdims=True)
        dlt = y - mu
        var = jnp.mean(dlt * dlt, axis=-1, keepdims=True)
        z = _silu(dlt * lax.rsqrt(var + EPS) * lng_ref[...] + lnb_ref[...])
        o_ref[j * CHUNK:(j + 1) * CHUNK, :] = _dot(z.astype(BF16), wpw_ref[...])


def _conv_call(l, cv, n_seq_tokens, W):
    t = cv.shape[0]
    rb = min(n_seq_tokens, 512)
    nblk = n_seq_tokens // rb
    per = rb // HALO
    last = t // HALO - 1
    ins = [cv, cv, cv, W["conv_w"], W["conv_b"], W["conv_ln_g"], W["conv_ln_b"], W["conv_pw"]]
    return pl.pallas_call(
        functools.partial(_conv_kernel, rb, nblk),
        grid=(t // rb,),
        in_specs=[
            pl.BlockSpec((rb, 512), lambda i: (i, 0)),
            pl.BlockSpec((HALO, 512), lambda i: (jnp.maximum(i * per - 1, 0), 0)),
            pl.BlockSpec((HALO, 512), lambda i: (jnp.minimum((i + 1) * per, last), 0)),
        ] + [_wspec(a, l) for a in ins[3:]],
        out_specs=pl.BlockSpec((rb, 256), lambda i: (i, 0)),
        out_shape=jax.ShapeDtypeStruct((t, 256), F32),
        scratch_shapes=[pltpu.VMEM((rb + 2 * HALO, 256), F32),
                        pltpu.VMEM((SUBLANES, rb + 2 * HALO - SUBLANES, 256), F32)],
        compiler_params=_params(1),
        name="conv",
    )(*ins)


def _ret_kernel(nb, nc, has_s0, rq_ref, rk_ref, rv_ref, gf_ref, gb_ref, intra_ref, qd_ref, kdt_ref, cd_ref, gng_ref,
                *rest):
    if has_s0:
        s0_ref, y_ref, s_ref, kvb_ref, of_ref, ob_ref = rest
        st_ref = None
    else:
        _, y_ref, st_ref, s_ref, kvb_ref, of_ref, ob_ref = rest
    same_head = lax.shift_right_logical(_iota((256, 256), 0), 6) == lax.shift_right_logical(_iota((256, 256), 1), 6)
    avg = jnp.where(same_head, 1.0 / HEAD_W, 0.0).astype(BF16)
    row_head = lax.shift_right_logical(_iota((256, CHUNK), 0), 6)
    lane_head = lax.shift_right_logical(_iota((CHUNK, 256), 1), 6)

    def head_mean(x):
        hi, lo = _split_bf16(x)
        return _dot(hi, avg) + _dot(lo, avg)

    def rows_of(sq, cidx):
        return pl.ds(pl.multiple_of(sq * nc * CHUNK + cidx * CHUNK, CHUNK), CHUNK)

    def compact(st):
        return st[:, 0:64] + st[:, 64:128] + st[:, 128:192] + st[:, 192:256]

    for sq in range(nb):
        s_ref[sq] = s0_ref[sq, 0] if has_s0 else jnp.zeros((256, 256), F32)

    def fwd_one(sq, cidx):
        rows = rows_of(sq, cidx)
        qb = rq_ref[rows, :]
        vb = rv_ref[rows, :]
        kt = rk_ref[rows, :].astype(F32).T
        kbd = jnp.concatenate([jnp.where(row_head == hh, kt, 0.0).astype(BF16) for hh in range(HEADS)], axis=1)
        s = _dot(qb, kbd)
        p = jnp.concatenate([(s * intra_ref[0]).astype(BF16), (s * intra_ref[1]).astype(BF16)], axis=0)
        zero = jnp.zeros_like(vb)
        vbd = jnp.concatenate([jnp.where(lane_head == hh, vb, zero) for hh in range(HEADS)], axis=0)
        inner = _dot(p, vbd)
        kts = jnp.concatenate([(kt * kdt_ref[0]).astype(BF16), (kt * kdt_ref[1]).astype(BF16)], axis=0)
        kv = _dot(kts, vb)
        kvb_ref[sq * nc + cidx] = jnp.where(same_head, kv[256:512, :], 0.0)
        ob_ref[rows, :] = inner[CHUNK:2 * CHUNK, :]
        st = s_ref[sq]
        of_ref[rows, :] = inner[0:CHUNK, :] + _dot(qb, st.astype(BF16)) * qd_ref[0]
        s_ref[sq] = cd_ref[0] * st + jnp.where(same_head, kv[0:256, :], 0.0)

    def fwd_body(cidx, carry):
        for sq in range(nb):
            fwd_one(sq, cidx)
        return carry

    lax.fori_loop(0, nc, fwd_body, 0, unroll=2)
    for sq in range(nb):
        if st_ref is not None:
            st_ref[sq, 0] = compact(s_ref[sq])
        s_ref[sq] = s0_ref[sq, 1] if has_s0 else jnp.zeros((256, 256), F32)

    def bwd_body(it, carry):
        cidx = nc - 1 - it
        for sq in range(nb):
            rows = rows_of(sq, cidx)
            st = s_ref[sq]
            ob_ref[rows, :] = ob_ref[rows, :] + _dot(rq_ref[rows, :], st.astype(BF16)) * qd_ref[1]
            s_ref[sq] = cd_ref[1] * st + kvb_ref[sq * nc + cidx]
        return carry

    lax.fori_loop(0, nc, bwd_body, 0, unroll=2)
    if st_ref is not None:
        for sq in range(nb):
            st_ref[sq, 1] = compact(s_ref[sq])

    def norm_body(blk, carry):
        rows = pl.ds(pl.multiple_of(blk * 2 * CHUNK, 2 * CHUNK), 2 * CHUNK)
        o = jnp.concatenate([of_ref[rows, :], ob_ref[rows, :]], axis=0)
        dlt = o - head_mean(o)
        nrm = dlt * lax.rsqrt(head_mean(dlt * dlt) + EPS) * gng_ref[...]
        y_ref[rows, :] = (_silu(gf_ref[rows, :]) * nrm[0:2 * CHUNK, :]
                          + _silu(gb_ref[rows, :]) * nrm[2 * CHUNK:4 * CHUNK, :])
        return carry

    lax.fori_loop(0, nb * nc // 2, norm_body, 0, unroll=2)


RET_BLOCK_ROWS = 4096


def _ret_call(l, rqkv, rg, n_seq_tokens, W, s0_bd, state_out):
    t = rqkv.shape[0]
    n = n_seq_tokens
    b = t // n
    nc = n // CHUNK
    nb = min(b, max(1, RET_BLOCK_ROWS // n), 4)
    has_s0 = s0_bd is not None
    col = lambda j: pl.BlockSpec((nb * n, 256), lambda i: (i, j))
    ins = [rqkv] * 3 + [rg] * 2 + [W["ret_intra"], W["ret_qd"], W["ret_kdt"], W["ret_cd"], W["ret_gn_g"]]
    in_specs = [col(0), col(1), col(2), col(0), col(1)] + [_wspec(a, l) for a in ins[5:]]
    out_specs = [pl.BlockSpec((nb * n, 256), lambda i: (i, 0))]
    out_shape = [jax.ShapeDtypeStruct((t, 256), F32)]
    aliases = {}
    if has_s0:
        ins.append(s0_bd)
        in_specs.append(pl.BlockSpec((nb, 2, 256, 256), lambda i: (i, 0, 0, 0)))
    else:
        aliases[len(ins)] = 1
        ins.append(state_out)
        in_specs.append(pl.BlockSpec(memory_space=pl.ANY))
        out_specs.append(pl.BlockSpec((nb, None, 2, 256, HEAD_W), lambda i: (i, l, 0, 0, 0)))
        out_shape.append(jax.ShapeDtypeStruct(state_out.shape, state_out.dtype))
    return pl.pallas_call(
        functools.partial(_ret_kernel, nb, nc, has_s0),
        grid=(b // nb,),
        in_specs=in_specs,
        out_specs=out_specs,
        out_shape=out_shape,
        input_output_aliases=aliases,
        scratch_shapes=[pltpu.VMEM((nb, 256, 256), F32), pltpu.VMEM((nb * nc, 256, 256), F32),
                        pltpu.VMEM((nb * n, 256), F32), pltpu.VMEM((nb * n, 256), F32)],
        compiler_params=_params(1),
        name="ret",
    )(*ins)


ATT_TQ = 256


def _attn_kernel(nparts, nb, qt_ref, *refs):
    k_refs = refs[0:2 * nparts:2]
    vt_refs = refs[1:2 * nparts:2]
    o_ref = refs[2 * nparts]
    n_own = k_refs[-1].shape[0] // nb

    def operands(sq, hh):
        sl = slice(hh * LANES, (hh + 1) * LANES)
        vsl = slice(hh * HEAD_W, (hh + 1) * HEAD_W)
        own = slice(sq * n_own, (sq + 1) * n_own)
        parts = [(k_ref[:, sl], vt_ref[vsl, :]) for k_ref, vt_ref in zip(k_refs[:-1], vt_refs[:-1])]
        return parts + [(k_refs[-1][own, sl], vt_refs[-1][vsl, own])]

    work = [(sq, hh) for sq in range(nb) for hh in range(HEADS)]
    ss_all = [[_dot(kk, qt_ref[hh * LANES:(hh + 1) * LANES, sq * ATT_TQ:(sq + 1) * ATT_TQ])
               for kk, _ in operands(sq, hh)] for sq, hh in work]
    outs = []
    for (sq, hh), ss in zip(work, ss_all):
        m = ss[0].max(axis=0, keepdims=True)
        for s in ss[1:]:
            m = jnp.maximum(m, s.max(axis=0, keepdims=True))
        es = [jnp.exp(s - m) for s in ss]
        den = es[0].sum(axis=0, keepdims=True)
        for e in es[1:]:
            den = den + e.sum(axis=0, keepdims=True)
        oh = None
        for e, (_, vv) in zip(es, operands(sq, hh)):
            part = _dot(vv, e.astype(BF16))
            oh = part if oh is None else oh + part
        outs.append(oh * (1.0 / den))
    for sq in range(nb):
        o_ref[sq * ATT_TQ:(sq + 1) * ATT_TQ, :] = jnp.concatenate(outs[sq * HEADS:(sq + 1) * HEADS], axis=0).T


def _attn_call(l, qt, k, vt, cache, n_seq_tokens):
    t = k.shape[0]
    n = n_seq_tokens
    tq = ATT_TQ
    nq = n // tq
    b = t // n
    nb = min(b, 4) if (cache is None and nq == 1) else 1
    ins = [qt]
    in_specs = [pl.BlockSpec((512, nb * tq), lambda i, j: (0, i * nq + j))]
    if cache is not None:
        kc, vtc = cache
        m = kc.shape[2]
        ins += [kc, vtc]
        in_specs += [pl.BlockSpec((None, None, m, 512), lambda i, j: (l, i, 0, 0)),
                     pl.BlockSpec((None, None, 256, m), lambda i, j: (l, i, 0, 0))]
    ins += [k, vt]
    in_specs += [pl.BlockSpec((nb * n, 512), lambda i, j: (i, 0)), pl.BlockSpec((256, nb * n), lambda i, j: (0, i))]
    return pl.pallas_call(
        functools.partial(_attn_kernel, len(ins) // 2, nb),
        grid=(b // nb, nq),
        in_specs=in_specs,
        out_specs=pl.BlockSpec((nb * tq, 256), lambda i, j: (i * nq + j, 0)),
        out_shape=jax.ShapeDtypeStruct((t, 256), F32),
        compiler_params=_params(2),
        name="attn",
    )(*ins)


def _outproj_kernel(ya_ref, yb_ref, yc_ref, yd_ref, x_ref, mod_ref, wo_ref, g2_ref, rw_ref,
                    x1_ref, h2_ref, aff_ref, wb_ref):
    @pl.when(pl.program_id(0) == 0)
    def _():
        for rb in range(4):
            wb_ref[rb * 256:(rb + 1) * 256, :] = wo_ref[rb * 256:(rb + 1) * 256, :].astype(BF16)

    mix = _dot(ya_ref[...].astype(BF16), wb_ref[0:256, :])
    mix = mix + _dot(yb_ref[...].astype(BF16), wb_ref[256:512, :])
    mix = mix + _dot(yc_ref[...].astype(BF16), wb_ref[512:768, :])
    mix = mix + _dot(yd_ref[...].astype(BF16), wb_ref[768:1024, :])
    mod = mod_ref[...]
    x1 = x_ref[...] + mod[:, 2 * D:3 * D] * mix
    x1_ref[...] = x1
    h2 = _rms(x1) * g2_ref[...]
    h2 = h2 * (1.0 + mod[:, 4 * D:5 * D]) + mod[:, 3 * D:4 * D]
    h_hi, h_lo = _split_bf16(h2)
    h2_ref[...] = h_hi
    both = _dot(h_hi, rw_ref[...])
    logits = both[:, 0:LANES] + both[:, LANES:2 * LANES] + _dot(h_lo, rw_ref[:, 0:LANES])
    lane = _iota((1, LANES), 1)
    logits = jnp.where(lane < N_EXPERTS, logits, -1e30)
    e = jnp.exp(logits - logits.max(axis=-1, keepdims=True))
    aff_ref[...] = e / e.sum(axis=-1, keepdims=True)


def _outproj_call(l, ys, x, mod_p, n_seq_tokens, W, w_out):
    t = x.shape[0]
    tm = PROJ_ROWS
    tps = max(1, n_seq_tokens // tm)
    bm = mod_p.shape[0]
    row = lambda i: (i, 0)
    ins = list(ys) + [x, mod_p, w_out, W["norm2_g"], W["r_hilo"]]
    in_specs = [pl.BlockSpec((tm, 256), row)] * 4 + [
        pl.BlockSpec((tm, D), row),
        pl.BlockSpec((None, 1, 6 * D), (lambda i: (i // tps, 0, 0)) if bm > 1 else (lambda i: (0, 0, 0))),
        pl.BlockSpec((None, D, D), lambda i: (l, 0, 0), pipeline_mode=pl.Buffered(1)),
    ] + [_wspec(a, l) for a in ins[7:]]
    return pl.pallas_call(
        _outproj_kernel,
        grid=(t // tm,),
        in_specs=in_specs,
        out_specs=[pl.BlockSpec((tm, D), row), pl.BlockSpec((tm, D), row), pl.BlockSpec((tm, LANES), row)],
        out_shape=[jax.ShapeDtypeStruct((t, D), F32), jax.ShapeDtypeStruct((t, D), BF16),
                   jax.ShapeDtypeStruct((t, LANES), F32)],
        scratch_shapes=[pltpu.VMEM((D, D), BF16)],
        compiler_params=_params(1, "arbitrary"),
        name="out_proj",
    )(*ins)


GROUP_ROWS = 512
CUM_BLK = 256


def _route_kernel(n, cap, nseq, aff_ref, h2_ref, slot_ref, xe_ref, gs_ref, cum_ref, slot_t_ref):
    lane_row = _iota((1, LANES), 1)
    a = aff_ref[0:n, :]
    for s in range(1, nseq):
        a = a + pltpu.roll(aff_ref[s * n:(s + 1) * n, :], N_EXPERTS * s, 1)
    capf = jnp.float32(cap)
    used = N_EXPERTS * nseq
    fold = LANES // used
    folded = a[0:n // fold, :]
    for k in range(1, fold):
        folded = folded + pltpu.roll(a[k * (n // fold):(k + 1) * (n // fold), :], used * k, 1)

    def bisect(_, lohi):
        lo, hi = lohi
        mid = lo + lax.shift_right_logical(hi - lo, 1)
        cnt = jnp.sum(jnp.where(folded >= pltpu.bitcast(mid, F32), 1.0, 0.0), axis=0, keepdims=True)
        width = used
        while width < LANES:
            cnt = cnt + pltpu.roll(cnt, width, 1)
            width *= 2
        ok = cnt >= capf
        return jnp.where(ok, mid, lo), jnp.where(ok, hi, mid)

    lo0 = jnp.zeros((1, LANES), jnp.int32)
    hi0 = jnp.full((1, LANES), 0x3F800001, jnp.int32)
    lo, hi = lax.fori_loop(0, 31, bisect, (lo0, hi0))
    gt = a >= pltpu.bitcast(hi, F32)
    eq = (a >= pltpu.bitcast(lo, F32)) & jnp.logical_not(gt)
    need = capf - jnp.sum(jnp.where(gt, 1.0, 0.0), axis=0, keepdims=True)
    tri = jnp.where(_iota((CUM_BLK, CUM_BLK), 0) >= _iota((CUM_BLK, CUM_BLK), 1), 1.0, 0.0).astype(BF16)

    def cumsum_rows(flags):
        carry = jnp.zeros((1, LANES), F32)
        for rb in range(n // CUM_BLK):
            rows = slice(rb * CUM_BLK, (rb + 1) * CUM_BLK)
            part = _dot(tri, flags[rows, :].astype(BF16)) + carry
            cum_ref[rows, :] = part
            carry = part[CUM_BLK - 1:CUM_BLK, :]
        return cum_ref[...]

    eq_rank = cumsum_rows(jnp.where(eq, 1.0, 0.0))
    sel = gt | (eq & (eq_rank <= need))
    pos = cumsum_rows(jnp.where(sel, 1.0, 0.0))
    slot = jnp.where(sel, pos - 1.0, -1.0)
    for s in range(nseq):
        own = slot if s == 0 else pltpu.roll(slot, LANES - N_EXPERTS * s, 1)
        slot_ref[s * n:(s + 1) * n, :] = jnp.where(lane_row < N_EXPERTS, own, -1.0)
    slot_t_ref[...] = slot.T
    a_hi, a_lo = _split_bf16(a)
    a_hilo = jnp.concatenate([a_hi, a_lo], axis=1)
    ones = jnp.ones((LANES, LANES), BF16)
    gexp = GROUP_ROWS // cap
    shift = int(np.log2(cap))
    row_e = lax.shift_right_logical(_iota((GROUP_ROWS, LANES), 0), shift)
    slot_id = _iota((cap, n), 0).astype(F32)
    lane = _iota((GROUP_ROWS, LANES), 1)
    def build_onehot(s, g):
        pieces = []
        for j in range(gexp):
            e_lane = N_EXPERTS * s + g * gexp + j
            mine_row = jnp.broadcast_to(slot_t_ref[e_lane:e_lane + 1, :], (cap, n))
            pieces.append(jnp.where(mine_row == slot_id, 1.0, 0.0).astype(BF16))
        return pieces[0] if gexp == 1 else jnp.concatenate(pieces, axis=0)

    items = [(s, g) for s in range(nseq) for g in range(N_EXPERTS * cap // GROUP_ROWS)]
    onehot_next = build_onehot(*items[0])
    for idx, (s, g) in enumerate(items):
        onehot = onehot_next
        if idx + 1 < len(items):
            onehot_next = build_onehot(*items[idx + 1])
        xe = _dot(onehot, h2_ref[s * n:(s + 1) * n, :]).astype(BF16)
        gboth = _dot(onehot, a_hilo)
        mine = lane == row_e + (g * gexp + N_EXPERTS * s)
        g_hi, g_lo = _split_bf16(jnp.where(mine, gboth[:, 0:LANES] + gboth[:, LANES:2 * LANES], 0.0))
        gsb = _dot(g_hi, ones) + _dot(g_lo, ones)
        for j in range(gexp):
            xe_ref[g * gexp + j, s * cap:(s + 1) * cap, :] = xe[j * cap:(j + 1) * cap, :]
            gs_ref[g * gexp + j, s * cap:(s + 1) * cap, :] = gsb[j * cap:(j + 1) * cap, :]


def _route_call(aff, h2, n_seq_tokens):
    t = aff.shape[0]
    n = n_seq_tokens
    b = t // n
    cap = EC_CAPACITY * n // N_EXPERTS
    nseq = min(b, LANES // N_EXPERTS, max(1, 2048 // n))
    return pl.pallas_call(
        functools.partial(_route_kernel, n, cap, nseq),
        grid=(b // nseq,),
        in_specs=[pl.BlockSpec((nseq * n, LANES), lambda i: (i, 0)), pl.BlockSpec((nseq * n, D), lambda i: (i, 0))],
        out_specs=[pl.BlockSpec((nseq * n, LANES), lambda i: (i, 0)),
                   pl.BlockSpec((N_EXPERTS, nseq * cap, D), lambda i: (0, i, 0)),
                   pl.BlockSpec((N_EXPERTS, nseq * cap, LANES), lambda i: (0, i, 0))],
        out_shape=[jax.ShapeDtypeStruct((t, LANES), F32),
                   jax.ShapeDtypeStruct((N_EXPERTS, b * cap, D), BF16),
                   jax.ShapeDtypeStruct((N_EXPERTS, b * cap, LANES), F32)],
        scratch_shapes=[pltpu.VMEM((n, LANES), F32), pltpu.VMEM((LANES, n), F32)],
        compiler_params=_params(1),
        name="route",
    )(aff, h2)


FF_TILE = 512
FFN_ROWS = 512


def _ffn_kernel(xc_ref, xl_ref, gc_ref, gl_ref, wg_ref, wu_ref, wd_ref, yc_ref, yl_ref, accc_ref, accl_ref):
    @pl.when(pl.program_id(1) == 0)
    def _():
        accc_ref[...] = jnp.zeros(accc_ref.shape, F32)
        accl_ref[...] = jnp.zeros(accl_ref.shape, F32)

    wg = wg_ref[...].astype(BF16)
    wu = wu_ref[...].astype(BF16)
    chunks = ([(xc_ref, gc_ref, yc_ref, accc_ref, r0) for r0 in range(0, xc_ref.shape[0], FFN_ROWS)]
              + [(xl_ref, gl_ref, yl_ref, accl_ref, r0) for r0 in range(0, xl_ref.shape[0], FFN_ROWS)])

    def gate_up(idx):
        x_ref, _, _, _, r0 = chunks[idx]
        x = x_ref[r0:r0 + FFN_ROWS, :]
        return _dot(x, wg), _dot(x, wu)

    nxt = gate_up(0)
    wd = wd_ref[...].astype(BF16)
    for idx, (x_ref, g_ref, y_ref, acc_ref, r0) in enumerate(chunks):
        a, up = nxt
        if idx + 1 < len(chunks):
            nxt = gate_up(idx + 1)
        rows = slice(r0, r0 + FFN_ROWS)
        contrib = _dot((_silu(a) * up).astype(BF16), wd)
        total = acc_ref[rows, :] + contrib
        acc_ref[rows, :] = total
        gate = jnp.concatenate([g_ref[rows, :]] * (D // LANES), axis=1)
        y_ref[rows, :] = (total * gate).astype(BF16)


def _ffn_call(l, xe_c, xe_l, gs_c, gs_l, w_gate, w_up, w_down):
    rc = xe_c.shape[1]
    rl = xe_l.shape[1]
    ex = lambda r, w: pl.BlockSpec((None, r, w), lambda e, f: (e, 0, 0))
    return pl.pallas_call(
        _ffn_kernel,
        grid=(N_EXPERTS, FF // FF_TILE),
        in_specs=[
            ex(rc, D), ex(rl, D), ex(rc, LANES), ex(rl, LANES),
            pl.BlockSpec((None, None, D, FF_TILE), lambda e, f: (l, e, 0, f)),
            pl.BlockSpec((None, None, D, FF_TILE), lambda e, f: (l, e, 0, f)),
            pl.BlockSpec((None, None, FF_TILE, D), lambda e, f: (l, e, f, 0)),
        ],
        out_specs=[ex(rc, D), ex(rl, D)],
        out_shape=[jax.ShapeDtypeStruct((N_EXPERTS, rc, D), BF16), jax.ShapeDtypeStruct((N_EXPERTS, rl, D), BF16)],
        scratch_shapes=[pltpu.VMEM((rc, D), F32), pltpu.VMEM((rl, D), F32)],
        compiler_params=pltpu.CompilerParams(dimension_semantics=("parallel", "arbitrary"),
                                             vmem_limit_bytes=VMEM_LIMIT),
        name="ffn",
    )(xe_c, xe_l, gs_c, gs_l, w_gate, w_up, w_down)


COMB_ROWS = 256


def _scatter_rows(cap, ye_ref, slot, seq=0):
    own = slice(seq * cap, (seq + 1) * cap)
    gexp = GROUP_ROWS // cap
    shift = int(np.log2(cap))
    col_e = lax.shift_right_logical(_iota((LANES, GROUP_ROWS), 1), shift)
    col_s = (_iota((1, GROUP_ROWS), 1) & (cap - 1)).astype(F32)
    lane_e = _iota((LANES, GROUP_ROWS), 0)
    sb = slot.astype(BF16)

    def build_onehot_t(g):
        expand_m = jnp.where(lane_e == col_e + g * gexp, 1.0, 0.0).astype(BF16)
        return jnp.where(_dot(sb, expand_m) == col_s, 1.0, 0.0).astype(BF16)

    ngroups = N_EXPERTS * cap // GROUP_ROWS
    acc = jnp.zeros((COMB_ROWS, D), F32)
    onehot_next = build_onehot_t(0)
    for g in range(ngroups):
        onehot_t = onehot_next
        if g + 1 < ngroups:
            onehot_next = build_onehot_t(g + 1)
        if cap % LANES == 0:
            for j in range(gexp):
                acc = acc + _dot(onehot_t[:, j * cap:(j + 1) * cap], ye_ref[g * gexp + j, own, :])
        else:
            ye = jnp.concatenate([ye_ref[g * gexp + j, own, :] for j in range(gexp)], axis=0)
            acc = acc + _dot(onehot_t, ye)
    return acc


def _combine_kernel(cap, final, ye_ref, slot_ref, x1_ref, mod_ref, *rest):
    if final:
        fg_ref, x2_ref, yf_ref = rest
    else:
        (x2_ref,) = rest
    mod = mod_ref[...]
    x2 = x1_ref[...] + mod[:, 5 * D:6 * D] * _scatter_rows(cap, ye_ref, slot_ref[...])
    x2_ref[...] = x2
    if final:
        yf_ref[...] = _rms(x2) * fg_ref[...]


def _combine_call(ye, slot, x1, mod_p, n_seq_tokens, final_g):
    t = x1.shape[0]
    n = n_seq_tokens
    b = t // n
    nr = n // COMB_ROWS
    cap = EC_CAPACITY * n // N_EXPERTS
    bm = mod_p.shape[0]
    final = final_g is not None
    row = lambda i, j: (i * nr + j, 0)
    ins = [ye, slot, x1, mod_p]
    in_specs = [
        pl.BlockSpec((N_EXPERTS, cap, D), lambda i, j: (0, i, 0)),
        pl.BlockSpec((COMB_ROWS, LANES), row),
        pl.BlockSpec((COMB_ROWS, D), row),
        pl.BlockSpec((None, 1, 6 * D), (lambda i, j: (i, 0, 0)) if bm > 1 else (lambda i, j: (0, 0, 0))),
    ]
    out_shape = [jax.ShapeDtypeStruct((t, D), F32)]
    out_specs = [pl.BlockSpec((COMB_ROWS, D), row)]
    if final:
        ins.append(final_g)
        in_specs.append(pl.BlockSpec((1, D), lambda i, j: (0, 0)))
        out_shape.append(jax.ShapeDtypeStruct((t, D), F32))
        out_specs.append(pl.BlockSpec((COMB_ROWS, D), row))
    return pl.pallas_call(
        functools.partial(_combine_kernel, cap, final),
        grid=(b, nr),
        in_specs=in_specs,
        out_specs=out_specs,
        out_shape=out_shape,
        compiler_params=_params(2),
        name="combine",
    )(*ins)


def _rot_cols(w):
    a, b, c, d = (w[..., 8 * i:8 * (i + 1)] for i in range(4))
    return jnp.concatenate([-b, a, -d, c], axis=-1)


def _rope_tables(n):
    rows = n // GRID_W
    row = jnp.repeat(jnp.arange(rows, dtype=F32), GRID_W)
    col = jnp.tile(jnp.arange(GRID_W, dtype=F32), rows)
    inv = ROPE_THETA ** (-jnp.arange(0, ROPE // 2, 2, dtype=F32) / (ROPE // 2))
    ra = row[:, None] * inv
    ca = col[:, None] * inv
    ang = jnp.concatenate([ra, ra, ca, ca], axis=-1)
    cos, sin = jnp.cos(ang), jnp.sin(ang)
    z32, z64, z96 = (jnp.zeros((n, w), F32) for w in (32, 64, 96))
    cq = jnp.concatenate([jnp.ones((n, 64), F32), cos, z32], axis=1)
    sq = jnp.concatenate([z64, sin, z32], axis=1)
    ck = jnp.concatenate([z64, cos, z32], axis=1)
    sk = jnp.concatenate([z96, sin], axis=1)
    return cq, sq, ck, sk


def _retention_tables(p_f, p_b):
    pos = jnp.arange(CHUNK, dtype=F32)
    diff = pos[:, None] - pos[None, :]

    def one(p, backward):
        lg = jnp.log1p(-jnp.exp2(p.astype(F32)))[:, :, None]
        dd = -diff if backward else diff
        intra = jnp.where(dd >= 0, jnp.exp(jnp.maximum(dd, 0.0) * lg[..., None]), 0.0)
        qexp = (CHUNK - pos) if backward else (pos + 1.0)
        kexp = pos if backward else (CHUNK - 1.0 - pos)
        qd = jnp.exp(qexp * lg)
        kd = jnp.exp(kexp * lg)
        cd = jnp.exp(CHUNK * lg)
        nl = p.shape[0]
        intra_w = jnp.swapaxes(intra, 1, 2).reshape(nl, CHUNK, HEADS * CHUNK)
        qd_w = jnp.repeat(jnp.swapaxes(qd, 1, 2), HEAD_W, axis=2)
        kd_t = jnp.repeat(kd, HEAD_W, axis=1)
        cdw = jnp.broadcast_to(jnp.repeat(cd, HEAD_W, axis=1), (nl, 256, 256))
        return intra_w, qd_w, kd_t, cdw

    f = one(p_f, False)
    b = one(p_b, True)
    return tuple(jnp.stack([x, y], axis=1) for x, y in zip(f, b))


def _prepare(norm1_g, w_in_t, sg_norm_g, sg_w, sg_b, conv_w, conv_b, conv_ln_g, conv_ln_b, conv_pw, ret_decay_f,
             ret_decay_b, ret_gn_g, q_norm_g, w_uq, kv_norm_g, w_ukv, norm2_g, router):
    L = DEPTH
    row = lambda a: a.reshape(L, 1, -1)
    tail_src = w_in_t[:, MAIN_W:, :]
    kpe_rows = tail_src[:, Q_LORA + KV_LORA:, :]
    kpe_rot = jnp.swapaxes(_rot_cols(jnp.swapaxes(kpe_rows, 1, 2)), 1, 2)
    w_tail = jnp.concatenate([tail_src[:, Q_LORA:Q_LORA + KV_LORA, :], tail_src[:, :Q_LORA, :], kpe_rows, kpe_rot],
                             axis=1)
    uq = w_uq.reshape(L, Q_LORA, HEADS, HEAD_W + ROPE)
    pad_rows = lambda a: jnp.pad(a, ((0, 0), (0, 256 - Q_LORA), (0, 0)))
    wq = pad_rows(jnp.pad(uq, ((0, 0), (0, 0), (0, 0), (0, LANES - HEAD_W - ROPE))).reshape(L, Q_LORA, 512))
    uq_rot = jnp.pad(_rot_cols(uq[..., HEAD_W:]), ((0, 0), (0, 0), (0, 0), (HEAD_W, LANES - HEAD_W - ROPE)))
    wqr = pad_rows(uq_rot.reshape(L, Q_LORA, 512))
    ukv = w_ukv.reshape(L, KV_LORA, HEADS, 2 * HEAD_W)
    wk = jnp.pad(ukv[..., :HEAD_W], ((0, 0), (0, 0), (0, 0), (0, LANES - HEAD_W))).reshape(L, KV_LORA, 512)
    wv = ukv[..., HEAD_W:].reshape(L, KV_LORA, 256)
    intra, qd, kd, cd = _retention_tables(ret_decay_f, ret_decay_b)
    r_pad = jnp.pad(router, ((0, 0), (0, 0), (0, LANES - N_EXPERTS)))
    r_hi = r_pad.astype(BF16)
    r_lo = (r_pad - r_hi.astype(F32)).astype(BF16)
    return dict(
        norm1_g=row(norm1_g), w_tail=w_tail, kv_g=row(kv_norm_g),
        q_g=jnp.pad(row(q_norm_g), ((0, 0), (0, 0), (0, 256 - Q_LORA))),
        wq=wq.astype(BF16), wqr=wqr.astype(BF16), wk=wk.astype(BF16), wv=wv.astype(BF16),
        sg_g=row(sg_norm_g), sg_w=sg_w.astype(BF16), sg_bm=jnp.repeat(jnp.swapaxes(sg_b, 1, 2), HEAD_W, axis=2),
        conv_w=jnp.pad(conv_w, ((0, 0), (0, 1), (0, 0))), conv_b=row(conv_b), conv_ln_g=row(conv_ln_g),
        conv_ln_b=row(conv_ln_b), conv_pw=conv_pw.astype(BF16),
        ret_intra=intra, ret_qd=qd, ret_kdt=kd, ret_cd=cd, ret_gn_g=row(ret_gn_g),
        norm2_g=row(norm2_g), r_hilo=jnp.concatenate([r_hi, r_lo], axis=2),
    )


def _block_diag_states(state):
    eye = jnp.eye(HEADS, dtype=state.dtype)
    bd = state[:, :, :, :, None, :] * eye[None, None, :, None, :, None]
    return bd.reshape(state.shape[0], 2, HEADS * HEAD_W, HEADS * HEAD_W)


def _mixers(l, x, mod_p, n, W, w_in, w_out, rope_tabs, s0_bd, cache, results, pending=None):
    outs = _inproj_call(l, x, mod_p, n, W, w_in, rope_tabs, None if results is None else results[:2], pending)
    if pending is not None:
        x = outs[-1]
    ya, cv, rqkv, rg, qt, k, vt = outs[:7]
    yb = _conv_call(l, cv, n, W)
    ret_out = _ret_call(l, rqkv, rg, n, W, s0_bd, None if results is None else results[2])
    yd = _attn_call(l, qt, k, vt, cache, n)
    x1, h2, aff = _outproj_call(l, (ya, yb, ret_out[0], yd), x, mod_p, n, W, w_out)
    slot, xe, gs = _route_call(aff, h2, n)
    new_results = None if results is None else (outs[7], outs[8], ret_out[1])
    return x1, slot, xe, gs, new_results


def kernel(x_prompt, x_sample, cache_mla_ckv, cache_mla_kpe, state_ret, c, c_ctx, w_mod, b_mod, norm1_g, w_in, sg_norm_g, sg_w, sg_b, conv_w, conv_b, conv_ln_g, conv_ln_b, conv_pw, ret_decay_f, ret_decay_b, ret_gn_g, q_norm_g, w_uq, kv_norm_g, w_ukv, w_out, norm2_g, router, w_gate, w_up, w_down, final_norm_g):
    bc, nc_tok, _ = x_prompt.shape
    bl, nl_tok, _ = x_sample.shape
    w_in = jnp.swapaxes(w_in, 1, 2)
    W = _prepare(norm1_g, w_in, sg_norm_g, sg_w, sg_b, conv_w, conv_b, conv_ln_g, conv_ln_b, conv_pw, ret_decay_f,
                 ret_decay_b, ret_gn_g, q_norm_g, w_uq, kv_norm_g, w_ukv, norm2_g, router)
    cvec = jnp.zeros((8, D), F32).at[0].set(c_ctx).at[1:1 + bl].set(c)
    mod_all = _mod_call(cvec, w_mod, b_mod)
    rope_tabs = _rope_tables(nl_tok)
    kpe128 = jnp.pad(cache_mla_kpe, ((0, 0), (0, 0), (0, 0), (HEAD_W, LANES - HEAD_W - ROPE)))
    cache = _kvcache_call(cache_mla_ckv, kpe128, W["wk"], W["wv"])
    s0_all = _block_diag_states(jnp.swapaxes(state_ret, 0, 1).reshape(DEPTH * bl, 2, HEADS, HEAD_W, HEAD_W))
    s0_all = s0_all.reshape(DEPTH, bl, 2, 256, 256)
    fg = final_norm_g.reshape(1, D)
    xp = x_prompt.reshape(bc * nc_tok, D)
    xs = x_sample.reshape(bl * nl_tok, D)
    results = (jnp.zeros((bc, DEPTH, nc_tok, KV_LORA), F32), jnp.zeros((bc, DEPTH, nc_tok, ROPE), F32),
               jnp.zeros((bc, DEPTH, 2, HEADS * HEAD_W, HEAD_W), F32))
    yp = ys = None
    pending = None
    for l in range(DEPTH):
        mod_c = mod_all[l, 0:1].reshape(1, 1, 6 * D)
        mod_l = mod_all[l, 1:1 + bl].reshape(bl, 1, 6 * D)
        x1c, slot_c, xe_c, gs_c, results = _mixers(l, xp, mod_c, nc_tok, W, w_in, w_out, None, None, None, results,
                                                   pending)
        x1l, slot_l, xe_l, gs_l, _ = _mixers(l, xs, mod_l, nl_tok, W, w_in, w_out, rope_tabs, s0_all[l], cache, None)
        ye_c, ye_l = _ffn_call(l, xe_c, xe_l, gs_c, gs_l, w_gate, w_up, w_down)
        last = l == DEPTH - 1
        res_l = _combine_call(ye_l, slot_l, x1l, mod_l, nl_tok, fg if last else None)
        xs = res_l[0]
        if last:
            yp, ys = _combine_call(ye_c, slot_c, x1c, mod_c, nc_tok, fg)[1], res_l[1]
        else:
            xp, pending = None, (ye_c, slot_c, x1c, mod_c)
    ckv_all, kpe_all, ret_all = results
    return (yp.reshape(bc, nc_tok, D), ys.reshape(bl, nl_tok, D), ckv_all, kpe_all,
            ret_all.reshape(bc, DEPTH, 2, HEADS, HEAD_W, HEAD_W))
```

```python
import functools

import jax
import jax.numpy as jnp
import numpy as np
from jax import lax
from jax.experimental import pallas as pl
from jax.experimental.pallas import tpu as pltpu

F32 = jnp.float32
BF16 = jnp.bfloat16

D = 1024
DEPTH = 4
CHUNK = 128
EPS = 1e-6
GRID_W = 64
CONV_K = 31
HEADS = 4
HEAD_W = 64
Q_LORA = 192
KV_LORA = 128
ROPE = 32
ROPE_THETA = 10000.0
N_EXPERTS = 16
FF = 1024
EC_CAPACITY = 2
MAIN_W = 2304
TAIL_W = 384
ATT_SCALE = (HEAD_W + ROPE) ** -0.5
LANES = 128
SUBLANES = 8
VMEM_LIMIT = 56 * 1024 * 1024


def _iota(shape, dim):
    return lax.broadcasted_iota(jnp.int32, shape, dim)


def _dot(a, b):
    return jnp.dot(a, b, preferred_element_type=F32)


def _dot_nt(a, b):
    return lax.dot_general(a, b, (((1,), (1,)), ((), ())), preferred_element_type=F32)


def _split_bf16(x):
    hi = x.astype(BF16)
    lo = (x - hi.astype(F32)).astype(BF16)
    return hi, lo


def _sigmoid(x):
    return 1.0 / (1.0 + jnp.exp(-x))


def _silu(x):
    return x * _sigmoid(x)


def _rms(x):
    return x * lax.rsqrt(jnp.mean(x * x, axis=-1, keepdims=True) + EPS)


def _wspec(arr, l):
    nd = arr.ndim
    return pl.BlockSpec((None,) + tuple(arr.shape[1:]), lambda *_: (l,) + (0,) * (nd - 1))


def _params(n_axes, sem="parallel"):
    return pltpu.CompilerParams(dimension_semantics=(sem,) * n_axes, vmem_limit_bytes=VMEM_LIMIT)


def _mod_kernel(c_ref, w_ref, b_ref, o_ref):
    cv = c_ref[...]
    s_hi, s_lo = _split_bf16(_silu(cv))
    w_hi, w_lo = _split_bf16(w_ref[...])
    o_ref[...] = _dot(s_hi, w_hi) + _dot(s_lo, w_hi) + _dot(s_hi, w_lo) + b_ref[...]


def _mod_call(cvec, w_mod, b_mod):
    nt = 1536
    return pl.pallas_call(
        _mod_kernel,
        grid=(DEPTH, 6 * D // nt),
        in_specs=[
            pl.BlockSpec((8, D), lambda l, j: (0, 0)),
            pl.BlockSpec((None, D, nt), lambda l, j: (l, 0, j)),
            pl.BlockSpec((None, 1, nt), lambda l, j: (l, 0, j)),
        ],
        out_specs=pl.BlockSpec((None, 8, nt), lambda l, j: (l, 0, j)),
        out_shape=jax.ShapeDtypeStruct((DEPTH, 8, 6 * D), F32),
        compiler_params=_params(2),
        name="mod",
    )(cvec, w_mod, b_mod.reshape(DEPTH, 1, 6 * D))


N_MAIN_BLOCKS = MAIN_W // 256
PROJ_ROWS = 512
RET_K_BLOCK = 5


def _inproj_kernel(rope, moe_cap, *refs):
    if moe_cap is None:
        x_ref, refs = refs[0], refs[1:]
    else:
        ye_ref, slot_ref, x1_ref, modp_ref = refs[0:4]
        refs = refs[4:]
    (mod_ref, g1_ref, w_ref, wt_ref, kvg_ref, qg_ref, wq_ref, wqr_ref, wk_ref, wv_ref, sgg_ref, sgw_ref,
     sgb_ref) = refs[0:13]
    rest = list(refs[13:])
    wb_ref = rest.pop()
    x2_ref = rest.pop() if moe_cap is not None else None
    if rope:
        cq_ref, sq_ref, ck_ref, sk_ref, ya_ref, cv_ref, rqkv_ref, rg_ref, qt_ref, k_ref, vt_ref = rest
    else:
        _, _, ya_ref, cv_ref, rqkv_ref, rg_ref, qt_ref, k_ref, vt_ref, ckv_ref, kpe_ref = rest

    @pl.when(pl.program_id(0) == 0)
    def _():
        for cb in range(N_MAIN_BLOCKS):
            blk = w_ref[cb * 256:(cb + 1) * 256, :]
            if cb == RET_K_BLOCK:
                blk = blk * (HEAD_W ** -0.5)
            wb_ref[cb * 256:(cb + 1) * 256, :] = blk.astype(BF16)
        wb_ref[MAIN_W:MAIN_W + TAIL_W, :] = wt_ref[...].astype(BF16)

    if moe_cap is None:
        x = x_ref[...]
    else:
        per_block_seq = ye_ref.shape[1] > moe_cap
        moe = [_scatter_rows(moe_cap, ye_ref, slot_ref[rb * COMB_ROWS:(rb + 1) * COMB_ROWS, :],
                             rb if per_block_seq else 0) for rb in range(x1_ref.shape[0] // COMB_ROWS)]
        x = x1_ref[...] + modp_ref[...][:, 5 * D:6 * D] * (moe[0] if len(moe) == 1 else jnp.concatenate(moe, axis=0))
        x2_ref[...] = x
    mod = mod_ref[...]
    h = _rms(x) * g1_ref[...]
    h = h * (1.0 + mod[:, D:2 * D]) + mod[:, 0:D]
    hb = h.astype(BF16)
    proj = lambda cb: _dot_nt(hb, wb_ref[cb * 256:(cb + 1) * 256, :])
    v_gate = proj(1)
    tail = _dot_nt(hb, wb_ref[MAIN_W:MAIN_W + TAIL_W, :])
    u = proj(0)
    cv_ref[:, 0:256] = proj(2)
    cv_ref[:, 256:512] = proj(3)
    for j in range(3):
        rqkv_ref[:, j * 256:(j + 1) * 256] = proj(4 + j).astype(BF16)
    rg_ref[:, 0:256] = proj(7)
    rg_ref[:, 256:512] = proj(8)
    vn = _rms(v_gate) * sgg_ref[...]
    group = lax.shift_right_logical(_iota((1, 256), 1), 6)
    for cidx in range(x.shape[0] // CHUNK):
        rows = slice(cidx * CHUNK, (cidx + 1) * CHUNK)
        acc = jnp.zeros((CHUNK, 256), F32)
        for g in range(HEADS):
            acc = acc + _dot(sgw_ref[g], jnp.where(group == g, vn[rows, :], 0.0).astype(BF16))
        ya_ref[rows, :] = u[rows, :] * (acc + sgb_ref[...])
    ckv_n = _rms(tail[:, 0:128]) * kvg_ref[...]
    c256 = tail[:, 128:384]
    lane256 = _iota((1, 256), 1)
    ms = jnp.sum(jnp.where(lane256 < Q_LORA, c256 * c256, 0.0), axis=-1, keepdims=True) * (1.0 / Q_LORA)
    cqn = (c256 * lax.rsqrt(ms + EPS) * qg_ref[...]).astype(BF16)
    ckvb = ckv_n.astype(BF16)
    q = _dot(cqn, wq_ref[...])
    kn = _dot(ckvb, wk_ref[...])
    vt_ref[...] = _dot(ckvb, wv_ref[...]).T.astype(BF16)
    slab = tail[:, 256:384]
    if rope:
        qr = _dot(cqn, wqr_ref[...])
        kx = slab * ck_ref[...] + pltpu.roll(slab * sk_ref[...], LANES - ROPE, 1)
        q = q * jnp.concatenate([cq_ref[...]] * HEADS, axis=1) + qr * jnp.concatenate([sq_ref[...]] * HEADS, axis=1)
    else:
        lane128 = _iota((1, LANES), 1)
        kx = jnp.where((lane128 >= HEAD_W) & (lane128 < HEAD_W + ROPE), slab, 0.0)
        kpe = pltpu.roll(slab, LANES - HEAD_W, 1)[:, 0:ROPE]
        n_out = ckv_ref.shape[1]
        for sq in range(ckv_ref.shape[0]):
            ckv_ref[sq] = ckv_n[sq * n_out:(sq + 1) * n_out, :]
            kpe_ref[sq] = kpe[sq * n_out:(sq + 1) * n_out, :]
    qt_ref[...] = (q * ATT_SCALE).T.astype(BF16)
    for hh in range(HEADS):
        sl = slice(hh * LANES, (hh + 1) * LANES)
        k_ref[:, sl] = (kn[:, sl] + kx).astype(BF16)


def _inproj_call(l, x, mod_p, n_seq_tokens, W, w_in, rope_tabs, cache_out, pending=None):
    tm = PROJ_ROWS
    tps = max(1, n_seq_tokens // tm)
    spt = max(1, tm // n_seq_tokens)
    bm = mod_p.shape[0]
    rope = rope_tabs is not None
    row = lambda i: (i, 0)
    col = lambda i: (0, i)
    mod_spec = lambda: pl.BlockSpec((None, 1, 6 * D), (lambda i: (i // tps, 0, 0)) if bm > 1 else (lambda i: (0, 0, 0)))
    moe_cap = None
    if pending is None:
        t = x.shape[0]
        ins = [x]
        in_specs = [pl.BlockSpec((tm, D), row)]
    else:
        ye, slot, x1, mod_prev = pending
        t = x1.shape[0]
        moe_cap = EC_CAPACITY * n_seq_tokens // N_EXPERTS
        assert n_seq_tokens == COMB_ROWS or n_seq_tokens % tm == 0
        ins = [ye, slot, x1, mod_prev]
        in_specs = [pl.BlockSpec((N_EXPERTS, spt * moe_cap, D), lambda i: (0, i // tps, 0),
                                 pipeline_mode=pl.Buffered(1) if tps > 1 else None),
                    pl.BlockSpec((tm, LANES), row), pl.BlockSpec((tm, D), row), mod_spec()]
    rest = [W["w_tail"], W["kv_g"], W["q_g"], W["wq"], W["wqr"], W["wk"], W["wv"], W["sg_g"], W["sg_w"], W["sg_bm"]]
    ins += [mod_p, W["norm1_g"], w_in] + rest
    in_specs += [
        mod_spec(),
        _wspec(W["norm1_g"], l),
        pl.BlockSpec((None, MAIN_W, D), lambda i: (l, 0, 0), pipeline_mode=pl.Buffered(1)),
        pl.BlockSpec((None, TAIL_W, D), lambda i: (l, 0, 0), pipeline_mode=pl.Buffered(1)),
    ] + [_wspec(a, l) for a in rest[1:]]
    out_shape = [
        jax.ShapeDtypeStruct((t, 256), F32), jax.ShapeDtypeStruct((t, 512), F32),
        jax.ShapeDtypeStruct((t, 768), BF16), jax.ShapeDtypeStruct((t, 512), F32),
        jax.ShapeDtypeStruct((512, t), BF16), jax.ShapeDtypeStruct((t, 512), BF16),
        jax.ShapeDtypeStruct((256, t), BF16),
    ]
    out_specs = [pl.BlockSpec((tm, 256), row), pl.BlockSpec((tm, 512), row), pl.BlockSpec((tm, 768), row),
                 pl.BlockSpec((tm, 512), row), pl.BlockSpec((512, tm), col), pl.BlockSpec((tm, 512), row),
                 pl.BlockSpec((256, tm), col)]
    aliases = {}
    if rope:
        ins += list(rope_tabs)
        in_specs += [pl.BlockSpec((tm, LANES), lambda i: (i % tps, 0))] * 4
    else:
        for acc in cache_out:
            aliases[len(ins)] = len(out_shape)
            ins.append(acc)
            in_specs.append(pl.BlockSpec(memory_space=pl.ANY))
            out_shape.append(jax.ShapeDtypeStruct(acc.shape, acc.dtype))
            out_specs.append(pl.BlockSpec((spt, None, tm // spt, acc.shape[-1]), lambda i: (i // tps, l, i % tps, 0)))
    if pending is not None:
        out_shape.append(jax.ShapeDtypeStruct((t, D), F32))
        out_specs.append(pl.BlockSpec((tm, D), row))
    return pl.pallas_call(
        functools.partial(_inproj_kernel, rope, moe_cap),
        grid=(t // tm,),
        in_specs=in_specs,
        out_specs=out_specs,
        out_shape=out_shape,
        input_output_aliases=aliases,
        scratch_shapes=[pltpu.VMEM((MAIN_W + TAIL_W, D), BF16)],
        compiler_params=_params(1, "arbitrary"),
        name="in_proj",
    )(*ins)


def _kvcache_kernel(ckv_ref, kpe_ref, wk_ref, wv_ref, k_ref, vt_ref):
    cb = ckv_ref[...].astype(BF16)
    kn = _dot(cb, wk_ref[...])
    kx = kpe_ref[...]
    for hh in range(HEADS):
        sl = slice(hh * LANES, (hh + 1) * LANES)
        k_ref[:, sl] = (kn[:, sl] + kx).astype(BF16)
    vt_ref[...] = _dot(cb, wv_ref[...]).T.astype(BF16)


def _kvcache_call(cache_ckv, kpe128, wk, wv):
    b, depth, m, _ = cache_ckv.shape
    blk = lambda w: pl.BlockSpec((None, None, m, w), lambda l, i: (i, l, 0, 0))
    wblk = lambda a: pl.BlockSpec((None,) + tuple(a.shape[1:]), lambda l, i: (l, 0, 0))
    return pl.pallas_call(
        _kvcache_kernel,
        grid=(depth, b),
        in_specs=[blk(128), blk(128), wblk(wk), wblk(wv)],
        out_specs=[pl.BlockSpec((None, None, m, 512), lambda l, i: (l, i, 0, 0)),
                   pl.BlockSpec((None, None, 256, m), lambda l, i: (l, i, 0, 0))],
        out_shape=[jax.ShapeDtypeStruct((depth, b, m, 512), BF16), jax.ShapeDtypeStruct((depth, b, 256, m), BF16)],
        compiler_params=_params(2),
        name="kv_cache",
    )(cache_ckv, kpe128, wk, wv)


HALO = 16


def _conv_kernel(rb, nblk, cur_ref, prev_ref, next_ref, wdw_ref, bdw_ref, lng_ref, lnb_ref, wpw_ref, o_ref, pad_ref,
                 sh_ref):
    i = pl.program_id(0)
    keep_prev = (i % nblk != 0).astype(F32)
    keep_next = (i % nblk != nblk - 1).astype(F32)

    def glu(blk):
        return blk[:, 0:256] * _sigmoid(blk[:, 256:512])

    pad_ref[0:HALO, :] = glu(prev_ref[...]) * keep_prev
    pad_ref[HALO:HALO + rb, :] = glu(cur_ref[...])
    pad_ref[HALO + rb:2 * HALO + rb, :] = glu(next_ref[...]) * keep_next
    span = rb + 2 * HALO - SUBLANES
    for s in range(SUBLANES):
        sh_ref[s] = pad_ref[s:s + span, :]
    off = HALO - CONV_K // 2
    for j in range(rb // CHUNK):
        acc = jnp.zeros((CHUNK, 256), F32)
        for k in range(CONV_K):
            start = j * CHUNK + (off + k) // SUBLANES * SUBLANES
            acc = acc + sh_ref[(off + k) % SUBLANES, start:start + CHUNK, :] * wdw_ref[k:k + 1, :]
        y = acc + bdw_ref[...]
        mu = jnp.mean(y, axis=-1, keepdims=True)
        dlt = y - mu
        var = jnp.mean(dlt * dlt, axis=-1, keepdims=True)
        z = _silu(dlt * lax.rsqrt(var + EPS) * lng_ref[...] + lnb_ref[...])
        o_ref[j * CHUNK:(j + 1) * CHUNK, :] = _dot(z.astype(BF16), wpw_ref[...])


def _conv_call(l, cv, n_seq_tokens, W):
    t = cv.shape[0]
    rb = min(n_seq_tokens, 512)
    nblk = n_seq_tokens // rb
    per = rb // HALO
    last = t // HALO - 1
    ins = [cv, cv, cv, W["conv_w"], W["conv_b"], W["conv_ln_g"], W["conv_ln_b"], W["conv_pw"]]
    return pl.pallas_call(
        functools.partial(_conv_kernel, rb, nblk),
        grid=(t // rb,),
        in_specs=[
            pl.BlockSpec((rb, 512), lambda i: (i, 0)),
            pl.BlockSpec((HALO, 512), lambda i: (jnp.maximum(i * per - 1, 0), 0)),
            pl.BlockSpec((HALO, 512), lambda i: (jnp.minimum((i + 1) * per, last), 0)),
        ] + [_wspec(a, l) for a in ins[3:]],
        out_specs=pl.BlockSpec((rb, 256), lambda i: (i, 0)),
        out_shape=jax.ShapeDtypeStruct((t, 256), F32),
        scratch_shapes=[pltpu.VMEM((rb + 2 * HALO, 256), F32),
                        pltpu.VMEM((SUBLANES, rb + 2 * HALO - SUBLANES, 256), F32)],
        compiler_params=_params(1),
        name="conv",
    )(*ins)


def _ret_kernel(nb, nc, has_s0, rq_ref, rk_ref, rv_ref, gf_ref, gb_ref, intra_ref, qd_ref, kdt_ref, cd_ref, gng_ref,
                *rest):
    if has_s0:
        s0_ref, y_ref, s_ref, kvb_ref, of_ref, ob_ref = rest
        st_ref = None
    else:
        _, y_ref, st_ref, s_ref, kvb_ref, of_ref, ob_ref = rest
    same_head = lax.shift_right_logical(_iota((256, 256), 0), 6) == lax.shift_right_logical(_iota((256, 256), 1), 6)
    avg = jnp.where(same_head, 1.0 / HEAD_W, 0.0).astype(BF16)
    row_head = lax.shift_right_logical(_iota((256, CHUNK), 0), 6)
    lane_head = lax.shift_right_logical(_iota((CHUNK, 256), 1), 6)

    def head_mean(x):
        hi, lo = _split_bf16(x)
        return _dot(hi, avg) + _dot(lo, avg)

    def rows_of(sq, cidx):
        return pl.ds(pl.multiple_of(sq * nc * CHUNK + cidx * CHUNK, CHUNK), CHUNK)

    def compact(st):
        return st[:, 0:64] + st[:, 64:128] + st[:, 128:192] + st[:, 192:256]

    for sq in range(nb):
        s_ref[sq] = s0_ref[sq, 0] if has_s0 else jnp.zeros((256, 256), F32)

    def fwd_one(sq, cidx):
        rows = rows_of(sq, cidx)
        qb = rq_ref[rows, :]
        vb = rv_ref[rows, :]
        kt = rk_ref[rows, :].astype(F32).T
        kbd = jnp.concatenate([jnp.where(row_head == hh, kt, 0.0).astype(BF16) for hh in range(HEADS)], axis=1)
        s = _dot(qb, kbd)
        p = jnp.concatenate([(s * intra_ref[0]).astype(BF16), (s * intra_ref[1]).astype(BF16)], axis=0)
        zero = jnp.zeros_like(vb)
        vbd = jnp.concatenate([jnp.where(lane_head == hh, vb, zero) for hh in range(HEADS)], axis=0)
        inner = _dot(p, vbd)
        kts = jnp.concatenate([(kt * kdt_ref[0]).astype(BF16), (kt * kdt_ref[1]).astype(BF16)], axis=0)
        kv = _dot(kts, vb)
        kvb_ref[sq * nc + cidx] = jnp.where(same_head, kv[256:512, :], 0.0)
        ob_ref[rows, :] = inner[CHUNK:2 * CHUNK, :]
        st = s_ref[sq]
        of_ref[rows, :] = inner[0:CHUNK, :] + _dot(qb, st.astype(BF16)) * qd_ref[0]
        s_ref[sq] = cd_ref[0] * st + jnp.where(same_head, kv[0:256, :], 0.0)

    def fwd_body(cidx, carry):
        for sq in range(nb):
            fwd_one(sq, cidx)
        return carry

    lax.fori_loop(0, nc, fwd_body, 0, unroll=2)
    for sq in range(nb):
        if st_ref is not None:
            st_ref[sq, 0] = compact(s_ref[sq])
        s_ref[sq] = s0_ref[sq, 1] if has_s0 else jnp.zeros((256, 256), F32)

    def bwd_body(it, carry):
        cidx = nc - 1 - it
        for sq in range(nb):
            rows = rows_of(sq, cidx)
            st = s_ref[sq]
            ob_ref[rows, :] = ob_ref[rows, :] + _dot(rq_ref[rows, :], st.astype(BF16)) * qd_ref[1]
            s_ref[sq] = cd_ref[1] * st + kvb_ref[sq * nc + cidx]
        return carry

    lax.fori_loop(0, nc, bwd_body, 0, unroll=2)
    if st_ref is not None:
        for sq in range(nb):
            st_ref[sq, 1] = compact(s_ref[sq])

    def norm_body(blk, carry):
        rows = pl.ds(pl.multiple_of(blk * 2 * CHUNK, 2 * CHUNK), 2 * CHUNK)
        o = jnp.concatenate([of_ref[rows, :], ob_ref[rows, :]], axis=0)
        dlt = o - head_mean(o)
        nrm = dlt * lax.rsqrt(head_mean(dlt * dlt) + EPS) * gng_ref[...]
        y_ref[rows, :] = (_silu(gf_ref[rows, :]) * nrm[0:2 * CHUNK, :]
                          + _silu(gb_ref[rows, :]) * nrm[2 * CHUNK:4 * CHUNK, :])
        return carry

    lax.fori_loop(0, nb * nc // 2, norm_body, 0, unroll=2)


RET_BLOCK_ROWS = 4096


def _ret_call(l, rqkv, rg, n_seq_tokens, W, s0_bd, state_out):
    t = rqkv.shape[0]
    n = n_seq_tokens
    b = t // n
    nc = n // CHUNK
    nb = min(b, max(1, RET_BLOCK_ROWS // n), 4)
    has_s0 = s0_bd is not None
    col = lambda j: pl.BlockSpec((nb * n, 256), lambda i: (i, j))
    ins = [rqkv] * 3 + [rg] * 2 + [W["ret_intra"], W["ret_qd"], W["ret_kdt"], W["ret_cd"], W["ret_gn_g"]]
    in_specs = [col(0), col(1), col(2), col(0), col(1)] + [_wspec(a, l) for a in ins[5:]]
    out_specs = [pl.BlockSpec((nb * n, 256), lambda i: (i, 0))]
    out_shape = [jax.ShapeDtypeStruct((t, 256), F32)]
    aliases = {}
    if has_s0:
        ins.append(s0_bd)
        in_specs.append(pl.BlockSpec((nb, 2, 256, 256), lambda i: (i, 0, 0, 0)))
    else:
        aliases[len(ins)] = 1
        ins.append(state_out)
        in_specs.append(pl.BlockSpec(memory_space=pl.ANY))
        out_specs.append(pl.BlockSpec((nb, None, 2, 256, HEAD_W), lambda i: (i, l, 0, 0, 0)))
        out_shape.append(jax.ShapeDtypeStruct(state_out.shape, state_out.dtype))
    return pl.pallas_call(
        functools.partial(_ret_kernel, nb, nc, has_s0),
        grid=(b // nb,),
        in_specs=in_specs,
        out_specs=out_specs,
        out_shape=out_shape,
        input_output_aliases=aliases,
        scratch_shapes=[pltpu.VMEM((nb, 256, 256), F32), pltpu.VMEM((nb * nc, 256, 256), F32),
                        pltpu.VMEM((nb * n, 256), F32), pltpu.VMEM((nb * n, 256), F32)],
        compiler_params=_params(1),
        name="ret",
    )(*ins)


ATT_TQ = 256


def _attn_kernel(nparts, nb, qt_ref, *refs):
    k_refs = refs[0:2 * nparts:2]
    vt_refs = refs[1:2 * nparts:2]
    o_ref = refs[2 * nparts]
    n_own = k_refs[-1].shape[0] // nb

    def operands(sq, hh):
        sl = slice(hh * LANES, (hh + 1) * LANES)
        vsl = slice(hh * HEAD_W, (hh + 1) * HEAD_W)
        own = slice(sq * n_own, (sq + 1) * n_own)
        parts = [(k_ref[:, sl], vt_ref[vsl, :]) for k_ref, vt_ref in zip(k_refs[:-1], vt_refs[:-1])]
        return parts + [(k_refs[-1][own, sl], vt_refs[-1][vsl, own])]

    work = [(sq, hh) for sq in range(nb) for hh in range(HEADS)]
    ss_all = [[_dot(kk, qt_ref[hh * LANES:(hh + 1) * LANES, sq * ATT_TQ:(sq + 1) * ATT_TQ])
               for kk, _ in operands(sq, hh)] for sq, hh in work]
    outs = []
    for (sq, hh), ss in zip(work, ss_all):
        m = ss[0].max(axis=0, keepdims=True)
        for s in ss[1:]:
            m = jnp.maximum(m, s.max(axis=0, keepdims=True))
        es = [jnp.exp(s - m) for s in ss]
        den = es[0].sum(axis=0, keepdims=True)
        for e in es[1:]:
            den = den + e.sum(axis=0, keepdims=True)
        oh = None
        for e, (_, vv) in zip(es, operands(sq, hh)):
            part = _dot(vv, e.astype(BF16))
            oh = part if oh is None else oh + part
        outs.append(oh * (1.0 / den))
    for sq in range(nb):
        o_ref[sq * ATT_TQ:(sq + 1) * ATT_TQ, :] = jnp.concatenate(outs[sq * HEADS:(sq + 1) * HEADS], axis=0).T


def _attn_call(l, qt, k, vt, cache, n_seq_tokens):
    t = k.shape[0]
    n = n_seq_tokens
    tq = ATT_TQ
    nq = n // tq
    b = t // n
    nb = min(b, 4) if (cache is None and nq == 1) else 1
    ins = [qt]
    in_specs = [pl.BlockSpec((512, nb * tq), lambda i, j: (0, i * nq + j))]
    if cache is not None:
        kc, vtc = cache
        m = kc.shape[2]
        ins += [kc, vtc]
        in_specs += [pl.BlockSpec((None, None, m, 512), lambda i, j: (l, i, 0, 0)),
                     pl.BlockSpec((None, None, 256, m), lambda i, j: (l, i, 0, 0))]
    ins += [k, vt]
    in_specs += [pl.BlockSpec((nb * n, 512), lambda i, j: (i, 0)), pl.BlockSpec((256, nb * n), lambda i, j: (0, i))]
    return pl.pallas_call(
        functools.partial(_attn_kernel, len(ins) // 2, nb),
        grid=(b // nb, nq),
        in_specs=in_specs,
        out_specs=pl.BlockSpec((nb * tq, 256), lambda i, j: (i * nq + j, 0)),
        out_shape=jax.ShapeDtypeStruct((t, 256), F32),
        compiler_params=_params(2),
        name="attn",
    )(*ins)


def _outproj_kernel(ya_ref, yb_ref, yc_ref, yd_ref, x_ref, mod_ref, wo_ref, g2_ref, rw_ref,
                    x1_ref, h2_ref, aff_ref, wb_ref):
    @pl.when(pl.program_id(0) == 0)
    def _():
        for rb in range(4):
            wb_ref[rb * 256:(rb + 1) * 256, :] = wo_ref[rb * 256:(rb + 1) * 256, :].astype(BF16)

    mix = _dot(ya_ref[...].astype(BF16), wb_ref[0:256, :])
    mix = mix + _dot(yb_ref[...].astype(BF16), wb_ref[256:512, :])
    mix = mix + _dot(yc_ref[...].astype(BF16), wb_ref[512:768, :])
    mix = mix + _dot(yd_ref[...].astype(BF16), wb_ref[768:1024, :])
    mod = mod_ref[...]
    x1 = x_ref[...] + mod[:, 2 * D:3 * D] * mix
    x1_ref[...] = x1
    h2 = _rms(x1) * g2_ref[...]
    h2 = h2 * (1.0 + mod[:, 4 * D:5 * D]) + mod[:, 3 * D:4 * D]
    h_hi, h_lo = _split_bf16(h2)
    h2_ref[...] = h_hi
    both = _dot(h_hi, rw_ref[...])
    logits = both[:, 0:LANES] + both[:, LANES:2 * LANES] + _dot(h_lo, rw_ref[:, 0:LANES])
    lane = _iota((1, LANES), 1)
    logits = jnp.where(lane < N_EXPERTS, logits, -1e30)
    e = jnp.exp(logits - logits.max(axis=-1, keepdims=True))
    aff_ref[...] = e / e.sum(axis=-1, keepdims=True)


def _outproj_call(l, ys, x, mod_p, n_seq_tokens, W, w_out):
    t = x.shape[0]
    tm = PROJ_ROWS
    tps = max(1, n_seq_tokens // tm)
    bm = mod_p.shape[0]
    row = lambda i: (i, 0)
    ins = list(ys) + [x, mod_p, w_out, W["norm2_g"], W["r_hilo"]]
    in_specs = [pl.BlockSpec((tm, 256), row)] * 4 + [
        pl.BlockSpec((tm, D), row),
        pl.BlockSpec((None, 1, 6 * D), (lambda i: (i // tps, 0, 0)) if bm > 1 else (lambda i: (0, 0, 0))),
        pl.BlockSpec((None, D, D), lambda i: (l, 0, 0), pipeline_mode=pl.Buffered(1)),
    ] + [_wspec(a, l) for a in ins[7:]]
    return pl.pallas_call(
        _outproj_kernel,
        grid=(t // tm,),
        in_specs=in_specs,
        out_specs=[pl.BlockSpec((tm, D), row), pl.BlockSpec((tm, D), row), pl.BlockSpec((tm, LANES), row)],
        out_shape=[jax.ShapeDtypeStruct((t, D), F32), jax.ShapeDtypeStruct((t, D), BF16),
                   jax.ShapeDtypeStruct((t, LANES), F32)],
        scratch_shapes=[pltpu.VMEM((D, D), BF16)],
        compiler_params=_params(1, "arbitrary"),
        name="out_proj",
    )(*ins)


GROUP_ROWS = 512
CUM_BLK = 256


def _route_kernel(n, cap, nseq, aff_ref, h2_ref, slot_ref, xe_ref, gs_ref, cum_ref, slot_t_ref):
    lane_row = _iota((1, LANES), 1)
    a = aff_ref[0:n, :]
    for s in range(1, nseq):
        a = a + pltpu.roll(aff_ref[s * n:(s + 1) * n, :], N_EXPERTS * s, 1)
    capf = jnp.float32(cap)
    used = N_EXPERTS * nseq
    fold = LANES // used
    folded = a[0:n // fold, :]
    for k in range(1, fold):
        folded = folded + pltpu.roll(a[k * (n // fold):(k + 1) * (n // fold), :], used * k, 1)

    def bisect(_, lohi):
        lo, hi = lohi
        mid = lo + lax.shift_right_logical(hi - lo, 1)
        cnt = jnp.sum(jnp.where(folded >= pltpu.bitcast(mid, F32), 1.0, 0.0), axis=0, keepdims=True)
        width = used
        while width < LANES:
            cnt = cnt + pltpu.roll(cnt, width, 1)
            width *= 2
        ok = cnt >= capf
        return jnp.where(ok, mid, lo), jnp.where(ok, hi, mid)

    lo0 = jnp.zeros((1, LANES), jnp.int32)
    hi0 = jnp.full((1, LANES), 0x3F800001, jnp.int32)
    lo, hi = lax.fori_loop(0, 31, bisect, (lo0, hi0))
    gt = a >= pltpu.bitcast(hi, F32)
    eq = (a >= pltpu.bitcast(lo, F32)) & jnp.logical_not(gt)
    need = capf - jnp.sum(jnp.where(gt, 1.0, 0.0), axis=0, keepdims=True)
    tri = jnp.where(_iota((CUM_BLK, CUM_BLK), 0) >= _iota((CUM_BLK, CUM_BLK), 1), 1.0, 0.0).astype(BF16)

    def cumsum_rows(flags):
        carry = jnp.zeros((1, LANES), F32)
        for rb in range(n // CUM_BLK):
            rows = slice(rb * CUM_BLK, (rb + 1) * CUM_BLK)
            part = _dot(tri, flags[rows, :].astype(BF16)) + carry
            cum_ref[rows, :] = part
            carry = part[CUM_BLK - 1:CUM_BLK, :]
        return cum_ref[...]

    eq_rank = cumsum_rows(jnp.where(eq, 1.0, 0.0))
    sel = gt | (eq & (eq_rank <= need))
    pos = cumsum_rows(jnp.where(sel, 1.0, 0.0))
    slot = jnp.where(sel, pos - 1.0, -1.0)
    for s in range(nseq):
        own = slot if s == 0 else pltpu.roll(slot, LANES - N_EXPERTS * s, 1)
        slot_ref[s * n:(s + 1) * n, :] = jnp.where(lane_row < N_EXPERTS, own, -1.0)
    slot_t_ref[...] = slot.T
    a_hi, a_lo = _split_bf16(a)
    a_hilo = jnp.concatenate([a_hi, a_lo], axis=1)
    ones = jnp.ones((LANES, LANES), BF16)
    gexp = GROUP_ROWS // cap
    shift = int(np.log2(cap))
    row_e = lax.shift_right_logical(_iota((GROUP_ROWS, LANES), 0), shift)
    slot_id = _iota((cap, n), 0).astype(F32)
    lane = _iota((GROUP_ROWS, LANES), 1)
    def build_onehot(s, g):
        pieces = []
        for j in range(gexp):
            e_lane = N_EXPERTS * s + g * gexp + j
            mine_row = jnp.broadcast_to(slot_t_ref[e_lane:e_lane + 1, :], (cap, n))
            pieces.append(jnp.where(mine_row == slot_id, 1.0, 0.0).astype(BF16))
        return pieces[0] if gexp == 1 else jnp.concatenate(pieces, axis=0)

    items = [(s, g) for s in range(nseq) for g in range(N_EXPERTS * cap // GROUP_ROWS)]
    onehot_next = build_onehot(*items[0])
    for idx, (s, g) in enumerate(items):
        onehot = onehot_next
        if idx + 1 < len(items):
            onehot_next = build_onehot(*items[idx + 1])
        xe = _dot(onehot, h2_ref[s * n:(s + 1) * n, :]).astype(BF16)
        gboth = _dot(onehot, a_hilo)
        mine = lane == row_e + (g * gexp + N_EXPERTS * s)
        g_hi, g_lo = _split_bf16(jnp.where(mine, gboth[:, 0:LANES] + gboth[:, LANES:2 * LANES], 0.0))
        gsb = _dot(g_hi, ones) + _dot(g_lo, ones)
        for j in range(gexp):
            xe_ref[g * gexp + j, s * cap:(s + 1) * cap, :] = xe[j * cap:(j + 1) * cap, :]
            gs_ref[g * gexp + j, s * cap:(s + 1) * cap, :] = gsb[j * cap:(j + 1) * cap, :]


def _route_call(aff, h2, n_seq_tokens):
    t = aff.shape[0]
    n = n_seq_tokens
    b = t // n
    cap = EC_CAPACITY * n // N_EXPERTS
    nseq = min(b, LANES // N_EXPERTS, max(1, 2048 // n))
    return pl.pallas_call(
        functools.partial(_route_kernel, n, cap, nseq),
        grid=(b // nseq,),
        in_specs=[pl.BlockSpec((nseq * n, LANES), lambda i: (i, 0)), pl.BlockSpec((nseq * n, D), lambda i: (i, 0))],
        out_specs=[pl.BlockSpec((nseq * n, LANES), lambda i: (i, 0)),
                   pl.BlockSpec((N_EXPERTS, nseq * cap, D), lambda i: (0, i, 0)),
                   pl.BlockSpec((N_EXPERTS, nseq * cap, LANES), lambda i: (0, i, 0))],
        out_shape=[jax.ShapeDtypeStruct((t, LANES), F32),
                   jax.ShapeDtypeStruct((N_EXPERTS, b * cap, D), BF16),
                   jax.ShapeDtypeStruct((N_EXPERTS, b * cap, LANES), F32)],
        scratch_shapes=[pltpu.VMEM((n, LANES), F32), pltpu.VMEM((LANES, n), F32)],
        compiler_params=_params(1),
        name="route",
    )(aff, h2)


FF_TILE = 512
FFN_ROWS = 512


def _ffn_kernel(xc_ref, xl_ref, gc_ref, gl_ref, wg_ref, wu_ref, wd_ref, yc_ref, yl_ref, wgb_ref, wub_ref, wdb_ref):
    chunks = ([(xc_ref, gc_ref, yc_ref, r0) for r0 in range(0, xc_ref.shape[0], FFN_ROWS)]
              + [(xl_ref, gl_ref, yl_ref, r0) for r0 in range(0, xl_ref.shape[0], FFN_ROWS)])
    tiles = FF // FF_TILE
    units = [(ci, ft) for ci in range(len(chunks)) for ft in range(tiles)]
    for ft in range(tiles):
        cols = slice(ft * FF_TILE, (ft + 1) * FF_TILE)
        wgb_ref[:, cols] = wg_ref[:, cols].astype(BF16)
        wub_ref[:, cols] = wu_ref[:, cols].astype(BF16)

    def gate_up(unit):
        ci, ft = unit
        x_ref, _, _, r0 = chunks[ci]
        x = x_ref[r0:r0 + FFN_ROWS, :]
        cols = slice(ft * FF_TILE, (ft + 1) * FF_TILE)
        return _dot(x, wgb_ref[:, cols]), _dot(x, wub_ref[:, cols])

    nxt = gate_up(units[0])
    for ft in range(tiles):
        rows = slice(ft * FF_TILE, (ft + 1) * FF_TILE)
        wdb_ref[rows, :] = wd_ref[rows, :].astype(BF16)
    total = None
    for idx, (ci, ft) in enumerate(units):
        a, up = nxt
        if idx + 1 < len(units):
            nxt = gate_up(units[idx + 1])
        contrib = _dot((_silu(a) * up).astype(BF16), wdb_ref[ft * FF_TILE:(ft + 1) * FF_TILE, :])
        total = contrib if ft == 0 else total + contrib
        if ft == tiles - 1:
            _, g_ref, y_ref, r0 = chunks[ci]
            rows = slice(r0, r0 + FFN_ROWS)
            gate = jnp.concatenate([g_ref[rows, :]] * (D // LANES), axis=1)
            y_ref[rows, :] = (total * gate).astype(BF16)


def _ffn_call(l, xe_c, xe_l, gs_c, gs_l, w_gate, w_up, w_down):
    rc = xe_c.shape[1]
    rl = xe_l.shape[1]
    ex = lambda r, w: pl.BlockSpec((None, r, w), lambda e: (e, 0, 0))
    wspec = lambda rows, cols: pl.BlockSpec((None, None, rows, cols), lambda e: (l, e, 0, 0))
    return pl.pallas_call(
        _ffn_kernel,
        grid=(N_EXPERTS,),
        in_specs=[ex(rc, D), ex(rl, D), ex(rc, LANES), ex(rl, LANES), wspec(D, FF), wspec(D, FF), wspec(FF, D)],
        out_specs=[ex(rc, D), ex(rl, D)],
        out_shape=[jax.ShapeDtypeStruct((N_EXPERTS, rc, D), BF16), jax.ShapeDtypeStruct((N_EXPERTS, rl, D), BF16)],
        scratch_shapes=[pltpu.VMEM((D, FF), BF16), pltpu.VMEM((D, FF), BF16), pltpu.VMEM((FF, D), BF16)],
        compiler_params=_params(1),
        name="ffn",
    )(xe_c, xe_l, gs_c, gs_l, w_gate, w_up, w_down)


COMB_ROWS = 256


def _scatter_rows(cap, ye_ref, slot, seq=0):
    own = slice(seq * cap, (seq + 1) * cap)
    gexp = GROUP_ROWS // cap
    shift = int(np.log2(cap))
    col_e = lax.shift_right_logical(_iota((LANES, GROUP_ROWS), 1), shift)
    col_s = (_iota((1, GROUP_ROWS), 1) & (cap - 1)).astype(F32)
    lane_e = _iota((LANES, GROUP_ROWS), 0)
    sb = slot.astype(BF16)

    def build_onehot_t(g):
        expand_m = jnp.where(lane_e == col_e + g * gexp, 1.0, 0.0).astype(BF16)
        return jnp.where(_dot(sb, expand_m) == col_s, 1.0, 0.0).astype(BF16)

    ngroups = N_EXPERTS * cap // GROUP_ROWS
    acc = jnp.zeros((COMB_ROWS, D), F32)
    onehot_next = build_onehot_t(0)
    for g in range(ngroups):
        onehot_t = onehot_next
        if g + 1 < ngroups:
            onehot_next = build_onehot_t(g + 1)
        if cap % LANES == 0:
            for j in range(gexp):
                acc = acc + _dot(onehot_t[:, j * cap:(j + 1) * cap], ye_ref[g * gexp + j, own, :])
        else:
            ye = jnp.concatenate([ye_ref[g * gexp + j, own, :] for j in range(gexp)], axis=0)
            acc = acc + _dot(onehot_t, ye)
    return acc


def _combine_kernel(cap, final, ye_ref, slot_ref, x1_ref, mod_ref, *rest):
    if final:
        fg_ref, x2_ref, yf_ref = rest
    else:
        (x2_ref,) = rest
    mod = mod_ref[...]
    x2 = x1_ref[...] + mod[:, 5 * D:6 * D] * _scatter_rows(cap, ye_ref, slot_ref[...])
    x2_ref[...] = x2
    if final:
        yf_ref[...] = _rms(x2) * fg_ref[...]


def _combine_call(ye, slot, x1, mod_p, n_seq_tokens, final_g):
    t = x1.shape[0]
    n = n_seq_tokens
    b = t // n
    nr = n // COMB_ROWS
    cap = EC_CAPACITY * n // N_EXPERTS
    bm = mod_p.shape[0]
    final = final_g is not None
    row = lambda i, j: (i * nr + j, 0)
    ins = [ye, slot, x1, mod_p]
    in_specs = [
        pl.BlockSpec((N_EXPERTS, cap, D), lambda i, j: (0, i, 0)),
        pl.BlockSpec((COMB_ROWS, LANES), row),
        pl.BlockSpec((COMB_ROWS, D), row),
        pl.BlockSpec((None, 1, 6 * D), (lambda i, j: (i, 0, 0)) if bm > 1 else (lambda i, j: (0, 0, 0))),
    ]
    out_shape = [jax.ShapeDtypeStruct((t, D), F32)]
    out_specs = [pl.BlockSpec((COMB_ROWS, D), row)]
    if final:
        ins.append(final_g)
        in_specs.append(pl.BlockSpec((1, D), lambda i, j: (0, 0)))
        out_shape.append(jax.ShapeDtypeStruct((t, D), F32))
        out_specs.append(pl.BlockSpec((COMB_ROWS, D), row))
    return pl.pallas_call(
        functools.partial(_combine_kernel, cap, final),
        grid=(b, nr),
        in_specs=in_specs,
        out_specs=out_specs,
        out_shape=out_shape,
        compiler_params=_params(2),
        name="combine",
    )(*ins)


def _rot_cols(w):
    a, b, c, d = (w[..., 8 * i:8 * (i + 1)] for i in range(4))
    return jnp.concatenate([-b, a, -d, c], axis=-1)


def _rope_tables(n):
    rows = n // GRID_W
    row = jnp.repeat(jnp.arange(rows, dtype=F32), GRID_W)
    col = jnp.tile(jnp.arange(GRID_W, dtype=F32), rows)
    inv = ROPE_THETA ** (-jnp.arange(0, ROPE // 2, 2, dtype=F32) / (ROPE // 2))
    ra = row[:, None] * inv
    ca = col[:, None] * inv
    ang = jnp.concatenate([ra, ra, ca, ca], axis=-1)
    cos, sin = jnp.cos(ang), jnp.sin(ang)
    z32, z64, z96 = (jnp.zeros((n, w), F32) for w in (32, 64, 96))
    cq = jnp.concatenate([jnp.ones((n, 64), F32), cos, z32], axis=1)
    sq = jnp.concatenate([z64, sin, z32], axis=1)
    ck = jnp.concatenate([z64, cos, z32], axis=1)
    sk = jnp.concatenate([z96, sin], axis=1)
    return cq, sq, ck, sk


def _retention_tables(p_f, p_b):
    pos = jnp.arange(CHUNK, dtype=F32)
    diff = pos[:, None] - pos[None, :]

    def one(p, backward):
        lg = jnp.log1p(-jnp.exp2(p.astype(F32)))[:, :, None]
        dd = -diff if backward else diff
        intra = jnp.where(dd >= 0, jnp.exp(jnp.maximum(dd, 0.0) * lg[..., None]), 0.0)
        qexp = (CHUNK - pos) if backward else (pos + 1.0)
        kexp = pos if backward else (CHUNK - 1.0 - pos)
        qd = jnp.exp(qexp * lg)
        kd = jnp.exp(kexp * lg)
        cd = jnp.exp(CHUNK * lg)
        nl = p.shape[0]
        intra_w = jnp.swapaxes(intra, 1, 2).reshape(nl, CHUNK, HEADS * CHUNK)
        qd_w = jnp.repeat(jnp.swapaxes(qd, 1, 2), HEAD_W, axis=2)
        kd_t = jnp.repeat(kd, HEAD_W, axis=1)
        cdw = jnp.broadcast_to(jnp.repeat(cd, HEAD_W, axis=1), (nl, 256, 256))
        return intra_w, qd_w, kd_t, cdw

    f = one(p_f, False)
    b = one(p_b, True)
    return tuple(jnp.stack([x, y], axis=1) for x, y in zip(f, b))


def _prepare(norm1_g, w_in_t, sg_norm_g, sg_w, sg_b, conv_w, conv_b, conv_ln_g, conv_ln_b, conv_pw, ret_decay_f,
             ret_decay_b, ret_gn_g, q_norm_g, w_uq, kv_norm_g, w_ukv, norm2_g, router):
    L = DEPTH
    row = lambda a: a.reshape(L, 1, -1)
    tail_src = w_in_t[:, MAIN_W:, :]
    kpe_rows = tail_src[:, Q_LORA + KV_LORA:, :]
    kpe_rot = jnp.swapaxes(_rot_cols(jnp.swapaxes(kpe_rows, 1, 2)), 1, 2)
    w_tail = jnp.concatenate([tail_src[:, Q_LORA:Q_LORA + KV_LORA, :], tail_src[:, :Q_LORA, :], kpe_rows, kpe_rot],
                             axis=1)
    uq = w_uq.reshape(L, Q_LORA, HEADS, HEAD_W + ROPE)
    pad_rows = lambda a: jnp.pad(a, ((0, 0), (0, 256 - Q_LORA), (0, 0)))
    wq = pad_rows(jnp.pad(uq, ((0, 0), (0, 0), (0, 0), (0, LANES - HEAD_W - ROPE))).reshape(L, Q_LORA, 512))
    uq_rot = jnp.pad(_rot_cols(uq[..., HEAD_W:]), ((0, 0), (0, 0), (0, 0), (HEAD_W, LANES - HEAD_W - ROPE)))
    wqr = pad_rows(uq_rot.reshape(L, Q_LORA, 512))
    ukv = w_ukv.reshape(L, KV_LORA, HEADS, 2 * HEAD_W)
    wk = jnp.pad(ukv[..., :HEAD_W], ((0, 0), (0, 0), (0, 0), (0, LANES - HEAD_W))).reshape(L, KV_LORA, 512)
    wv = ukv[..., HEAD_W:].reshape(L, KV_LORA, 256)
    intra, qd, kd, cd = _retention_tables(ret_decay_f, ret_decay_b)
    r_pad = jnp.pad(router, ((0, 0), (0, 0), (0, LANES - N_EXPERTS)))
    r_hi = r_pad.astype(BF16)
    r_lo = (r_pad - r_hi.astype(F32)).astype(BF16)
    return dict(
        norm1_g=row(norm1_g), w_tail=w_tail, kv_g=row(kv_norm_g),
        q_g=jnp.pad(row(q_norm_g), ((0, 0), (0, 0), (0, 256 - Q_LORA))),
        wq=wq.astype(BF16), wqr=wqr.astype(BF16), wk=wk.astype(BF16), wv=wv.astype(BF16),
        sg_g=row(sg_norm_g), sg_w=sg_w.astype(BF16), sg_bm=jnp.repeat(jnp.swapaxes(sg_b, 1, 2), HEAD_W, axis=2),
        conv_w=jnp.pad(conv_w, ((0, 0), (0, 1), (0, 0))), conv_b=row(conv_b), conv_ln_g=row(conv_ln_g),
        conv_ln_b=row(conv_ln_b), conv_pw=conv_pw.astype(BF16),
        ret_intra=intra, ret_qd=qd, ret_kdt=kd, ret_cd=cd, ret_gn_g=row(ret_gn_g),
        norm2_g=row(norm2_g), r_hilo=jnp.concatenate([r_hi, r_lo], axis=2),
    )


def _block_diag_states(state):
    eye = jnp.eye(HEADS, dtype=state.dtype)
    bd = state[:, :, :, :, None, :] * eye[None, None, :, None, :, None]
    return bd.reshape(state.shape[0], 2, HEADS * HEAD_W, HEADS * HEAD_W)


def _mixers(l, x, mod_p, n, W, w_in, w_out, rope_tabs, s0_bd, cache, results, pending=None):
    outs = _inproj_call(l, x, mod_p, n, W, w_in, rope_tabs, None if results is None else results[:2], pending)
    if pending is not None:
        x = outs[-1]
    ya, cv, rqkv, rg, qt, k, vt = outs[:7]
    yb = _conv_call(l, cv, n, W)
    ret_out = _ret_call(l, rqkv, rg, n, W, s0_bd, None if results is None else results[2])
    yd = _attn_call(l, qt, k, vt, cache, n)
    x1, h2, aff = _outproj_call(l, (ya, yb, ret_out[0], yd), x, mod_p, n, W, w_out)
    slot, xe, gs = _route_call(aff, h2, n)
    new_results = None if results is None else (outs[7], outs[8], ret_out[1])
    return x1, slot, xe, gs, new_results


def kernel(x_prompt, x_sample, cache_mla_ckv, cache_mla_kpe, state_ret, c, c_ctx, w_mod, b_mod, norm1_g, w_in, sg_norm_g, sg_w, sg_b, conv_w, conv_b, conv_ln_g, conv_ln_b, conv_pw, ret_decay_f, ret_decay_b, ret_gn_g, q_norm_g, w_uq, kv_norm_g, w_ukv, w_out, norm2_g, router, w_gate, w_up, w_down, final_norm_g):
    bc, nc_tok, _ = x_prompt.shape
    bl, nl_tok, _ = x_sample.shape
    w_in = jnp.swapaxes(w_in, 1, 2)
    W = _prepare(norm1_g, w_in, sg_norm_g, sg_w, sg_b, conv_w, conv_b, conv_ln_g, conv_ln_b, conv_pw, ret_decay_f,
                 ret_decay_b, ret_gn_g, q_norm_g, w_uq, kv_norm_g, w_ukv, norm2_g, router)
    cvec = jnp.zeros((8, D), F32).at[0].set(c_ctx).at[1:1 + bl].set(c)
    mod_all = _mod_call(cvec, w_mod, b_mod)
    rope_tabs = _rope_tables(nl_tok)
    kpe128 = jnp.pad(cache_mla_kpe, ((0, 0), (0, 0), (0, 0), (HEAD_W, LANES - HEAD_W - ROPE)))
    cache = _kvcache_call(cache_mla_ckv, kpe128, W["wk"], W["wv"])
    s0_all = _block_diag_states(jnp.swapaxes(state_ret, 0, 1).reshape(DEPTH * bl, 2, HEADS, HEAD_W, HEAD_W))
    s0_all = s0_all.reshape(DEPTH, bl, 2, 256, 256)
    fg = final_norm_g.reshape(1, D)
    xp = x_prompt.reshape(bc * nc_tok, D)
    xs = x_sample.reshape(bl * nl_tok, D)
    results = (jnp.zeros((bc, DEPTH, nc_tok, KV_LORA), F32), jnp.zeros((bc, DEPTH, nc_tok, ROPE), F32),
               jnp.zeros((bc, DEPTH, 2, HEADS * HEAD_W, HEAD_W), F32))
    yp = ys = None
    pend_c = pend_l = None
    for l in range(DEPTH):
        mod_c = mod_all[l, 0:1].reshape(1, 1, 6 * D)
        mod_l = mod_all[l, 1:1 + bl].reshape(bl, 1, 6 * D)
        x1c, slot_c, xe_c, gs_c, results = _mixers(l, xp, mod_c, nc_tok, W, w_in, w_out, None, None, None, results,
                                                   pend_c)
        x1l, slot_l, xe_l, gs_l, _ = _mixers(l, xs, mod_l, nl_tok, W, w_in, w_out, rope_tabs, s0_all[l], cache, None,
                                             pend_l)
        ye_c, ye_l = _ffn_call(l, xe_c, xe_l, gs_c, gs_l, w_gate, w_up, w_down)
        if l == DEPTH - 1:
            yp = _combine_call(ye_c, slot_c, x1c, mod_c, nc_tok, fg)[1]
            ys = _combine_call(ye_l, slot_l, x1l, mod_l, nl_tok, fg)[1]
        else:
            xp = xs = None
            pend_c, pend_l = (ye_c, slot_c, x1c, mod_c), (ye_l, slot_l, x1l, mod_l)
    ckv_all, kpe_all, ret_all = results
    return (yp.reshape(bc, nc_tok, D), ys.reshape(bl, nl_tok, D), ckv_all, kpe_all,
            ret_all.reshape(bc, DEPTH, 2, HEADS, HEAD_W, HEAD_W))
```

```python
import functools

import jax
import jax.numpy as jnp
import numpy as np
from jax import lax
from jax.experimental import pallas as pl
from jax.experimental.pallas import tpu as pltpu

F32 = jnp.float32
BF16 = jnp.bfloat16

D = 1024
DEPTH = 4
CHUNK = 128
EPS = 1e-6
GRID_W = 64
CONV_K = 31
HEADS = 4
HEAD_W = 64
Q_LORA = 192
KV_LORA = 128
ROPE = 32
ROPE_THETA = 10000.0
N_EXPERTS = 16
FF = 1024
EC_CAPACITY = 2
MAIN_W = 2304
TAIL_W = 384
ATT_SCALE = (HEAD_W + ROPE) ** -0.5
LOG2_E = 1.4426950408889634
LANES = 128
SUBLANES = 8
VMEM_LIMIT = 56 * 1024 * 1024


def _iota(shape, dim):
    return lax.broadcasted_iota(jnp.int32, shape, dim)


def _dot(a, b):
    return jnp.dot(a, b, preferred_element_type=F32)


def _dot_nt(a, b):
    return lax.dot_general(a, b, (((1,), (1,)), ((), ())), preferred_element_type=F32)


def _split_bf16(x):
    hi = x.astype(BF16)
    lo = (x - hi.astype(F32)).astype(BF16)
    return hi, lo


def _sigmoid(x):
    return 1.0 / (1.0 + jnp.exp(-x))


def _silu(x):
    return x * _sigmoid(x)


def _rms(x):
    return x * lax.rsqrt(jnp.mean(x * x, axis=-1, keepdims=True) + EPS)


def _wspec(arr, l):
    nd = arr.ndim
    return pl.BlockSpec((None,) + tuple(arr.shape[1:]), lambda *_: (l,) + (0,) * (nd - 1))


def _params(n_axes, sem="parallel"):
    return pltpu.CompilerParams(dimension_semantics=(sem,) * n_axes, vmem_limit_bytes=VMEM_LIMIT)


def _mod_kernel(c_ref, w_ref, b_ref, o_ref):
    cv = c_ref[...]
    s_hi, s_lo = _split_bf16(_silu(cv))
    w_hi, w_lo = _split_bf16(w_ref[...])
    o_ref[...] = _dot(s_hi, w_hi) + _dot(s_lo, w_hi) + _dot(s_hi, w_lo) + b_ref[...]


def _mod_call(cvec, w_mod, b_mod):
    nt = 1536
    return pl.pallas_call(
        _mod_kernel,
        grid=(DEPTH, 6 * D // nt),
        in_specs=[
            pl.BlockSpec((8, D), lambda l, j: (0, 0)),
            pl.BlockSpec((None, D, nt), lambda l, j: (l, 0, j)),
            pl.BlockSpec((None, 1, nt), lambda l, j: (l, 0, j)),
        ],
        out_specs=pl.BlockSpec((None, 8, nt), lambda l, j: (l, 0, j)),
        out_shape=jax.ShapeDtypeStruct((DEPTH, 8, 6 * D), F32),
        compiler_params=_params(2),
        name="mod",
    )(cvec, w_mod, b_mod.reshape(DEPTH, 1, 6 * D))


N_MAIN_BLOCKS = MAIN_W // 256
PROJ_ROWS = 512
RET_K_BLOCK = 5


def _inproj_kernel(rope, moe_cap, *refs):
    if moe_cap is None:
        x_ref, refs = refs[0], refs[1:]
    else:
        ye_ref, slot_ref, x1_ref, modp_ref = refs[0:4]
        refs = refs[4:]
    (mod_ref, g1_ref, w_ref, wt_ref, kvg_ref, qg_ref, wq_ref, wqr_ref, wk_ref, wv_ref, sgg_ref, sgw_ref,
     sgb_ref) = refs[0:13]
    rest = list(refs[13:])
    wb_ref = rest.pop()
    x2_ref = rest.pop() if moe_cap is not None else None
    if rope:
        cq_ref, sq_ref, ck_ref, sk_ref, ya_ref, cv_ref, rqkv_ref, rg_ref, qt_ref, k_ref, vt_ref = rest
    else:
        _, _, ya_ref, cv_ref, rqkv_ref, rg_ref, qt_ref, k_ref, vt_ref, ckv_ref, kpe_ref = rest

    @pl.when(pl.program_id(0) == 0)
    def _():
        for cb in range(N_MAIN_BLOCKS):
            blk = w_ref[cb * 256:(cb + 1) * 256, :]
            if cb == RET_K_BLOCK:
                blk = blk * (HEAD_W ** -0.5)
            wb_ref[cb * 256:(cb + 1) * 256, :] = blk.astype(BF16)
        wb_ref[MAIN_W:MAIN_W + TAIL_W, :] = wt_ref[...].astype(BF16)

    if moe_cap is None:
        x = x_ref[...]
    else:
        per_block_seq = ye_ref.shape[1] > moe_cap
        moe = [_scatter_rows(moe_cap, ye_ref, slot_ref[rb * COMB_ROWS:(rb + 1) * COMB_ROWS, :],
                             rb if per_block_seq else 0) for rb in range(x1_ref.shape[0] // COMB_ROWS)]
        x = x1_ref[...] + modp_ref[...][:, 5 * D:6 * D] * (moe[0] if len(moe) == 1 else jnp.concatenate(moe, axis=0))
        x2_ref[...] = x
    mod = mod_ref[...]
    h = _rms(x) * g1_ref[...]
    h = h * (1.0 + mod[:, D:2 * D]) + mod[:, 0:D]
    hb = h.astype(BF16)
    proj = lambda cb: _dot_nt(hb, wb_ref[cb * 256:(cb + 1) * 256, :])
    v_gate = proj(1)
    tail = _dot_nt(hb, wb_ref[MAIN_W:MAIN_W + TAIL_W, :])
    u = proj(0)
    cv_ref[:, 0:256] = proj(2)
    cv_ref[:, 256:512] = proj(3)
    for j in range(3):
        rqkv_ref[:, j * 256:(j + 1) * 256] = proj(4 + j).astype(BF16)
    rg_ref[:, 0:256] = proj(7)
    rg_ref[:, 256:512] = proj(8)
    vn = _rms(v_gate) * sgg_ref[...]
    group = lax.shift_right_logical(_iota((1, 256), 1), 6)
    for cidx in range(x.shape[0] // CHUNK):
        rows = slice(cidx * CHUNK, (cidx + 1) * CHUNK)
        acc = jnp.zeros((CHUNK, 256), F32)
        for g in range(HEADS):
            acc = acc + _dot(sgw_ref[g], jnp.where(group == g, vn[rows, :], 0.0).astype(BF16))
        ya_ref[rows, :] = u[rows, :] * (acc + sgb_ref[...])
    ckv_n = _rms(tail[:, 0:128]) * kvg_ref[...]
    c256 = tail[:, 128:384]
    lane256 = _iota((1, 256), 1)
    ms = jnp.sum(jnp.where(lane256 < Q_LORA, c256 * c256, 0.0), axis=-1, keepdims=True) * (1.0 / Q_LORA)
    cqn = (c256 * lax.rsqrt(ms + EPS) * qg_ref[...]).astype(BF16)
    ckvb = ckv_n.astype(BF16)
    q = _dot(cqn, wq_ref[...])
    kn = _dot(ckvb, wk_ref[...])
    vt_ref[...] = _dot(ckvb, wv_ref[...]).T.astype(BF16)
    slab = tail[:, 256:384]
    if rope:
        qr = _dot(cqn, wqr_ref[...])
        kx = slab * ck_ref[...] + pltpu.roll(slab * sk_ref[...], LANES - ROPE, 1)
        q = q * jnp.concatenate([cq_ref[...]] * HEADS, axis=1) + qr * jnp.concatenate([sq_ref[...]] * HEADS, axis=1)
    else:
        lane128 = _iota((1, LANES), 1)
        kx = jnp.where((lane128 >= HEAD_W) & (lane128 < HEAD_W + ROPE), slab, 0.0)
        kpe = pltpu.roll(slab, LANES - HEAD_W, 1)[:, 0:ROPE]
        n_out = ckv_ref.shape[1]
        for sq in range(ckv_ref.shape[0]):
            ckv_ref[sq] = ckv_n[sq * n_out:(sq + 1) * n_out, :]
            kpe_ref[sq] = kpe[sq * n_out:(sq + 1) * n_out, :]
    qt_ref[...] = (q * (ATT_SCALE * LOG2_E)).T.astype(BF16)
    for hh in range(HEADS):
        sl = slice(hh * LANES, (hh + 1) * LANES)
        k_ref[:, sl] = (kn[:, sl] + kx).astype(BF16)


def _inproj_call(l, x, mod_p, n_seq_tokens, W, w_in, rope_tabs, cache_out, pending=None):
    tm = PROJ_ROWS
    tps = max(1, n_seq_tokens // tm)
    spt = max(1, tm // n_seq_tokens)
    bm = mod_p.shape[0]
    rope = rope_tabs is not None
    row = lambda i: (i, 0)
    col = lambda i: (0, i)
    mod_spec = lambda: pl.BlockSpec((None, 1, 6 * D), (lambda i: (i // tps, 0, 0)) if bm > 1 else (lambda i: (0, 0, 0)))
    moe_cap = None
    if pending is None:
        t = x.shape[0]
        ins = [x]
        in_specs = [pl.BlockSpec((tm, D), row)]
    else:
        ye, slot, x1, mod_prev = pending
        t = x1.shape[0]
        moe_cap = EC_CAPACITY * n_seq_tokens // N_EXPERTS
        assert n_seq_tokens == COMB_ROWS or n_seq_tokens % tm == 0
        ins = [ye, slot, x1, mod_prev]
        in_specs = [pl.BlockSpec((N_EXPERTS, spt * moe_cap, D), lambda i: (0, i // tps, 0),
                                 pipeline_mode=pl.Buffered(1) if tps > 1 else None),
                    pl.BlockSpec((tm, LANES), row), pl.BlockSpec((tm, D), row), mod_spec()]
    rest = [W["w_tail"], W["kv_g"], W["q_g"], W["wq"], W["wqr"], W["wk"], W["wv"], W["sg_g"], W["sg_w"], W["sg_bm"]]
    ins += [mod_p, W["norm1_g"], w_in] + rest
    in_specs += [
        mod_spec(),
        _wspec(W["norm1_g"], l),
        pl.BlockSpec((None, MAIN_W, D), lambda i: (l, 0, 0), pipeline_mode=pl.Buffered(1)),
        pl.BlockSpec((None, TAIL_W, D), lambda i: (l, 0, 0), pipeline_mode=pl.Buffered(1)),
    ] + [_wspec(a, l) for a in rest[1:]]
    out_shape = [
        jax.ShapeDtypeStruct((t, 256), F32), jax.ShapeDtypeStruct((t, 512), F32),
        jax.ShapeDtypeStruct((t, 768), BF16), jax.ShapeDtypeStruct((t, 512), F32),
        jax.ShapeDtypeStruct((512, t), BF16), jax.ShapeDtypeStruct((t, 512), BF16),
        jax.ShapeDtypeStruct((256, t), BF16),
    ]
    out_specs = [pl.BlockSpec((tm, 256), row), pl.BlockSpec((tm, 512), row), pl.BlockSpec((tm, 768), row),
                 pl.BlockSpec((tm, 512), row), pl.BlockSpec((512, tm), col), pl.BlockSpec((tm, 512), row),
                 pl.BlockSpec((256, tm), col)]
    aliases = {}
    if rope:
        ins += list(rope_tabs)
        in_specs += [pl.BlockSpec((tm, LANES), lambda i: (i % tps, 0))] * 4
    else:
        for acc in cache_out:
            aliases[len(ins)] = len(out_shape)
            ins.append(acc)
            in_specs.append(pl.BlockSpec(memory_space=pl.ANY))
            out_shape.append(jax.ShapeDtypeStruct(acc.shape, acc.dtype))
            out_specs.append(pl.BlockSpec((spt, None, tm // spt, acc.shape[-1]), lambda i: (i // tps, l, i % tps, 0)))
    if pending is not None:
        out_shape.append(jax.ShapeDtypeStruct((t, D), F32))
        out_specs.append(pl.BlockSpec((tm, D), row))
    return pl.pallas_call(
        functools.partial(_inproj_kernel, rope, moe_cap),
        grid=(t // tm,),
        in_specs=in_specs,
        out_specs=out_specs,
        out_shape=out_shape,
        input_output_aliases=aliases,
        scratch_shapes=[pltpu.VMEM((MAIN_W + TAIL_W, D), BF16)],
        compiler_params=_params(1, "arbitrary"),
        name="in_proj",
    )(*ins)


def _kvcache_kernel(ckv_ref, kpe_ref, wk_ref, wv_ref, k_ref, vt_ref):
    cb = ckv_ref[...].astype(BF16)
    kn = _dot(cb, wk_ref[...])
    kx = kpe_ref[...]
    for hh in range(HEADS):
        sl = slice(hh * LANES, (hh + 1) * LANES)
        k_ref[:, sl] = (kn[:, sl] + kx).astype(BF16)
    vt_ref[...] = _dot(cb, wv_ref[...]).T.astype(BF16)


def _kvcache_call(cache_ckv, kpe128, wk, wv):
    b, depth, m, _ = cache_ckv.shape
    blk = lambda w: pl.BlockSpec((None, None, m, w), lambda l, i: (i, l, 0, 0))
    wblk = lambda a: pl.BlockSpec((None,) + tuple(a.shape[1:]), lambda l, i: (l, 0, 0))
    return pl.pallas_call(
        _kvcache_kernel,
        grid=(depth, b),
        in_specs=[blk(128), blk(128), wblk(wk), wblk(wv)],
        out_specs=[pl.BlockSpec((None, None, m, 512), lambda l, i: (l, i, 0, 0)),
                   pl.BlockSpec((None, None, 256, m), lambda l, i: (l, i, 0, 0))],
        out_shape=[jax.ShapeDtypeStruct((depth, b, m, 512), BF16), jax.ShapeDtypeStruct((depth, b, 256, m), BF16)],
        compiler_params=_params(2),
        name="kv_cache",
    )(cache_ckv, kpe128, wk, wv)


HALO = 16


def _conv_kernel(rb, nblk, cur_ref, prev_ref, next_ref, wdw_ref, bdw_ref, lng_ref, lnb_ref, wpw_ref, o_ref, pad_ref,
                 sh_ref):
    i = pl.program_id(0)
    keep_prev = (i % nblk != 0).astype(F32)
    keep_next = (i % nblk != nblk - 1).astype(F32)

    def glu(blk):
        return blk[:, 0:256] * _sigmoid(blk[:, 256:512])

    pad_ref[0:HALO, :] = glu(prev_ref[...]) * keep_prev
    pad_ref[HALO:HALO + rb, :] = glu(cur_ref[...])
    pad_ref[HALO + rb:2 * HALO + rb, :] = glu(next_ref[...]) * keep_next
    span = rb + 2 * HALO - SUBLANES
    for s in range(SUBLANES):
        sh_ref[s] = pad_ref[s:s + span, :]
    off = HALO - CONV_K // 2
    for j in range(rb // CHUNK):
        acc = jnp.zeros((CHUNK, 256), F32)
        for k in range(CONV_K):
            start = j * CHUNK + (off + k) // SUBLANES * SUBLANES
            acc = acc + sh_ref[(off + k) % SUBLANES, start:start + CHUNK, :] * wdw_ref[k:k + 1, :]
        y = acc + bdw_ref[...]
        mu = jnp.mean(y, axis=-1, keepdims=True)
        dlt = y - mu
        var = jnp.mean(dlt * dlt, axis=-1, keepdims=True)
        z = _silu(dlt * lax.rsqrt(var + EPS) * lng_ref[...] + lnb_ref[...])
        o_ref[j * CHUNK:(j + 1) * CHUNK, :] = _dot(z.astype(BF16), wpw_ref[...])


def _conv_call(l, cv, n_seq_tokens, W):
    t = cv.shape[0]
    rb = min(n_seq_tokens, 512)
    nblk = n_seq_tokens // rb
    per = rb // HALO
    last = t // HALO - 1
    ins = [cv, cv, cv, W["conv_w"], W["conv_b"], W["conv_ln_g"], W["conv_ln_b"], W["conv_pw"]]
    return pl.pallas_call(
        functools.partial(_conv_kernel, rb, nblk),
        grid=(t // rb,),
        in_specs=[
            pl.BlockSpec((rb, 512), lambda i: (i, 0)),
            pl.BlockSpec((HALO, 512), lambda i: (jnp.maximum(i * per - 1, 0), 0)),
            pl.BlockSpec((HALO, 512), lambda i: (jnp.minimum((i + 1) * per, last), 0)),
        ] + [_wspec(a, l) for a in ins[3:]],
        out_specs=pl.BlockSpec((rb, 256), lambda i: (i, 0)),
        out_shape=jax.ShapeDtypeStruct((t, 256), F32),
        scratch_shapes=[pltpu.VMEM((rb + 2 * HALO, 256), F32),
                        pltpu.VMEM((SUBLANES, rb + 2 * HALO - SUBLANES, 256), F32)],
        compiler_params=_params(1),
        name="conv",
    )(*ins)


def _ret_kernel(nb, nc, has_s0, rq_ref, rk_ref, rv_ref, gf_ref, gb_ref, intra_ref, qd_ref, kdt_ref, cd_ref, gng_ref,
                *rest):
    if has_s0:
        s0_ref, y_ref, s_ref, kvb_ref, of_ref, ob_ref = rest
        st_ref = None
    else:
        _, y_ref, st_ref, s_ref, kvb_ref, of_ref, ob_ref = rest
    same_head = lax.shift_right_logical(_iota((256, 256), 0), 6) == lax.shift_right_logical(_iota((256, 256), 1), 6)
    avg = jnp.where(same_head, 1.0 / HEAD_W, 0.0).astype(BF16)
    row_head = lax.shift_right_logical(_iota((256, CHUNK), 0), 6)
    lane_head = lax.shift_right_logical(_iota((CHUNK, 256), 1), 6)

    def head_mean(x):
        hi, lo = _split_bf16(x)
        return _dot(hi, avg) + _dot(lo, avg)

    def rows_of(sq, cidx):
        return pl.ds(pl.multiple_of(sq * nc * CHUNK + cidx * CHUNK, CHUNK), CHUNK)

    def compact(st):
        return st[:, 0:64] + st[:, 64:128] + st[:, 128:192] + st[:, 192:256]

    for sq in range(nb):
        s_ref[sq] = s0_ref[sq, 0] if has_s0 else jnp.zeros((256, 256), F32)

    def fwd_one(sq, cidx):
        rows = rows_of(sq, cidx)
        qb = rq_ref[rows, :]
        vb = rv_ref[rows, :]
        kt = rk_ref[rows, :].astype(F32).T
        kbd = jnp.concatenate([jnp.where(row_head == hh, kt, 0.0).astype(BF16) for hh in range(HEADS)], axis=1)
        s = _dot(qb, kbd)
        p = jnp.concatenate([(s * intra_ref[0]).astype(BF16), (s * intra_ref[1]).astype(BF16)], axis=0)
        zero = jnp.zeros_like(vb)
        vbd = jnp.concatenate([jnp.where(lane_head == hh, vb, zero) for hh in range(HEADS)], axis=0)
        inner = _dot(p, vbd)
        kts = jnp.concatenate([(kt * kdt_ref[0]).astype(BF16), (kt * kdt_ref[1]).astype(BF16)], axis=0)
        kv = _dot(kts, vb)
        kvb_ref[sq * nc + cidx] = jnp.where(same_head, kv[256:512, :], 0.0)
        ob_ref[rows, :] = inner[CHUNK:2 * CHUNK, :]
        st = s_ref[sq]
        of_ref[rows, :] = inner[0:CHUNK, :] + _dot(qb, st.astype(BF16)) * qd_ref[0]
        s_ref[sq] = cd_ref[0] * st + jnp.where(same_head, kv[0:256, :], 0.0)

    def fwd_body(cidx, carry):
        for sq in range(nb):
            fwd_one(sq, cidx)
        return carry

    lax.fori_loop(0, nc, fwd_body, 0, unroll=2)
    for sq in range(nb):
        if st_ref is not None:
            st_ref[sq, 0] = compact(s_ref[sq])
        s_ref[sq] = s0_ref[sq, 1] if has_s0 else jnp.zeros((256, 256), F32)

    def bwd_body(it, carry):
        cidx = nc - 1 - it
        for sq in range(nb):
            rows = rows_of(sq, cidx)
            st = s_ref[sq]
            ob_ref[rows, :] = ob_ref[rows, :] + _dot(rq_ref[rows, :], st.astype(BF16)) * qd_ref[1]
            s_ref[sq] = cd_ref[1] * st + kvb_ref[sq * nc + cidx]
        return carry

    lax.fori_loop(0, nc, bwd_body, 0, unroll=2)
    if st_ref is not None:
        for sq in range(nb):
            st_ref[sq, 1] = compact(s_ref[sq])

    def norm_body(blk, carry):
        rows = pl.ds(pl.multiple_of(blk * 2 * CHUNK, 2 * CHUNK), 2 * CHUNK)
        o = jnp.concatenate([of_ref[rows, :], ob_ref[rows, :]], axis=0)
        dlt = o - head_mean(o)
        nrm = dlt * lax.rsqrt(head_mean(dlt * dlt) + EPS) * gng_ref[...]
        y_ref[rows, :] = (_silu(gf_ref[rows, :]) * nrm[0:2 * CHUNK, :]
                          + _silu(gb_ref[rows, :]) * nrm[2 * CHUNK:4 * CHUNK, :])
        return carry

    lax.fori_loop(0, nb * nc // 2, norm_body, 0, unroll=2)


RET_BLOCK_ROWS = 4096
RET_MAX_SEQS = 8


def _ret_call(l, rqkv, rg, n_seq_tokens, W, s0_bd, state_out):
    t = rqkv.shape[0]
    n = n_seq_tokens
    b = t // n
    nc = n // CHUNK
    nb = min(b, max(1, RET_BLOCK_ROWS // n), RET_MAX_SEQS)
    has_s0 = s0_bd is not None
    col = lambda j: pl.BlockSpec((nb * n, 256), lambda i: (i, j))
    ins = [rqkv] * 3 + [rg] * 2 + [W["ret_intra"], W["ret_qd"], W["ret_kdt"], W["ret_cd"], W["ret_gn_g"]]
    in_specs = [col(0), col(1), col(2), col(0), col(1)] + [_wspec(a, l) for a in ins[5:]]
    out_specs = [pl.BlockSpec((nb * n, 256), lambda i: (i, 0))]
    out_shape = [jax.ShapeDtypeStruct((t, 256), F32)]
    aliases = {}
    if has_s0:
        ins.append(s0_bd)
        in_specs.append(pl.BlockSpec((nb, 2, 256, 256), lambda i: (i, 0, 0, 0)))
    else:
        aliases[len(ins)] = 1
        ins.append(state_out)
        in_specs.append(pl.BlockSpec(memory_space=pl.ANY))
        out_specs.append(pl.BlockSpec((nb, None, 2, 256, HEAD_W), lambda i: (i, l, 0, 0, 0)))
        out_shape.append(jax.ShapeDtypeStruct(state_out.shape, state_out.dtype))
    return pl.pallas_call(
        functools.partial(_ret_kernel, nb, nc, has_s0),
        grid=(b // nb,),
        in_specs=in_specs,
        out_specs=out_specs,
        out_shape=out_shape,
        input_output_aliases=aliases,
        scratch_shapes=[pltpu.VMEM((nb, 256, 256), F32), pltpu.VMEM((nb * nc, 256, 256), F32),
                        pltpu.VMEM((nb * n, 256), F32), pltpu.VMEM((nb * n, 256), F32)],
        compiler_params=_params(1),
        name="ret",
    )(*ins)


ATT_TQ = 256


def _attn_kernel(nparts, nb, qt_ref, *refs):
    k_refs = refs[0:2 * nparts:2]
    vt_refs = refs[1:2 * nparts:2]
    o_ref = refs[2 * nparts]
    n_own = k_refs[-1].shape[0] // nb

    def operands(sq, hh):
        sl = slice(hh * LANES, (hh + 1) * LANES)
        vsl = slice(hh * HEAD_W, (hh + 1) * HEAD_W)
        own = slice(sq * n_own, (sq + 1) * n_own)
        parts = [(k_ref[:, sl], vt_ref[vsl, :]) for k_ref, vt_ref in zip(k_refs[:-1], vt_refs[:-1])]
        return parts + [(k_refs[-1][own, sl], vt_refs[-1][vsl, own])]

    work = [(sq, hh) for sq in range(nb) for hh in range(HEADS)]
    ss_all = [[_dot(kk, qt_ref[hh * LANES:(hh + 1) * LANES, sq * ATT_TQ:(sq + 1) * ATT_TQ])
               for kk, _ in operands(sq, hh)] for sq, hh in work]
    outs = []
    for (sq, hh), ss in zip(work, ss_all):
        m = ss[0].max(axis=0, keepdims=True)
        for s in ss[1:]:
            m = jnp.maximum(m, s.max(axis=0, keepdims=True))
        es = [jnp.exp2(s - m) for s in ss]
        den = es[0].sum(axis=0, keepdims=True)
        for e in es[1:]:
            den = den + e.sum(axis=0, keepdims=True)
        oh = None
        for e, (_, vv) in zip(es, operands(sq, hh)):
            part = _dot(vv, e.astype(BF16))
            oh = part if oh is None else oh + part
        outs.append(oh * (1.0 / den))
    for sq in range(nb):
        o_ref[sq * ATT_TQ:(sq + 1) * ATT_TQ, :] = jnp.concatenate(outs[sq * HEADS:(sq + 1) * HEADS], axis=0).T


def _attn_call(l, qt, k, vt, cache, n_seq_tokens):
    t = k.shape[0]
    n = n_seq_tokens
    tq = ATT_TQ
    nq = n // tq
    b = t // n
    nb = min(b, 8) if (cache is None and nq == 1) else 1
    ins = [qt]
    in_specs = [pl.BlockSpec((512, nb * tq), lambda i, j: (0, i * nq + j))]
    if cache is not None:
        kc, vtc = cache
        m = kc.shape[2]
        ins += [kc, vtc]
        in_specs += [pl.BlockSpec((None, None, m, 512), lambda i, j: (l, i, 0, 0)),
                     pl.BlockSpec((None, None, 256, m), lambda i, j: (l, i, 0, 0))]
    ins += [k, vt]
    in_specs += [pl.BlockSpec((nb * n, 512), lambda i, j: (i, 0)), pl.BlockSpec((256, nb * n), lambda i, j: (0, i))]
    return pl.pallas_call(
        functools.partial(_attn_kernel, len(ins) // 2, nb),
        grid=(b // nb, nq),
        in_specs=in_specs,
        out_specs=pl.BlockSpec((nb * tq, 256), lambda i, j: (i * nq + j, 0)),
        out_shape=jax.ShapeDtypeStruct((t, 256), F32),
        compiler_params=_params(2),
        name="attn",
    )(*ins)


def _outproj_kernel(ya_ref, yb_ref, yc_ref, yd_ref, x_ref, mod_ref, wo_ref, g2_ref, rw_ref,
                    x1_ref, h2_ref, aff_ref, wb_ref):
    @pl.when(pl.program_id(0) == 0)
    def _():
        for rb in range(4):
            wb_ref[rb * 256:(rb + 1) * 256, :] = wo_ref[rb * 256:(rb + 1) * 256, :].astype(BF16)

    mix = _dot(ya_ref[...].astype(BF16), wb_ref[0:256, :])
    mix = mix + _dot(yb_ref[...].astype(BF16), wb_ref[256:512, :])
    mix = mix + _dot(yc_ref[...].astype(BF16), wb_ref[512:768, :])
    mix = mix + _dot(yd_ref[...].astype(BF16), wb_ref[768:1024, :])
    mod = mod_ref[...]
    x1 = x_ref[...] + mod[:, 2 * D:3 * D] * mix
    x1_ref[...] = x1
    h2 = _rms(x1) * g2_ref[...]
    h2 = h2 * (1.0 + mod[:, 4 * D:5 * D]) + mod[:, 3 * D:4 * D]
    h_hi, h_lo = _split_bf16(h2)
    h2_ref[...] = h_hi
    both = _dot(h_hi, rw_ref[...])
    logits = both[:, 0:LANES] + both[:, LANES:2 * LANES] + _dot(h_lo, rw_ref[:, 0:LANES])
    lane = _iota((1, LANES), 1)
    logits = jnp.where(lane < N_EXPERTS, logits, -1e30)
    e = jnp.exp(logits - logits.max(axis=-1, keepdims=True))
    aff_ref[...] = e / e.sum(axis=-1, keepdims=True)


def _outproj_call(l, ys, x, mod_p, n_seq_tokens, W, w_out):
    t = x.shape[0]
    tm = PROJ_ROWS
    tps = max(1, n_seq_tokens // tm)
    bm = mod_p.shape[0]
    row = lambda i: (i, 0)
    ins = list(ys) + [x, mod_p, w_out, W["norm2_g"], W["r_hilo"]]
    in_specs = [pl.BlockSpec((tm, 256), row)] * 4 + [
        pl.BlockSpec((tm, D), row),
        pl.BlockSpec((None, 1, 6 * D), (lambda i: (i // tps, 0, 0)) if bm > 1 else (lambda i: (0, 0, 0))),
        pl.BlockSpec((None, D, D), lambda i: (l, 0, 0), pipeline_mode=pl.Buffered(1)),
    ] + [_wspec(a, l) for a in ins[7:]]
    return pl.pallas_call(
        _outproj_kernel,
        grid=(t // tm,),
        in_specs=in_specs,
        out_specs=[pl.BlockSpec((tm, D), row), pl.BlockSpec((tm, D), row), pl.BlockSpec((tm, LANES), row)],
        out_shape=[jax.ShapeDtypeStruct((t, D), F32), jax.ShapeDtypeStruct((t, D), BF16),
                   jax.ShapeDtypeStruct((t, LANES), F32)],
        scratch_shapes=[pltpu.VMEM((D, D), BF16)],
        compiler_params=_params(1, "arbitrary"),
        name="out_proj",
    )(*ins)


GROUP_ROWS = 512
CUM_BLK = 256


def _route_kernel(n, cap, nseq, aff_ref, h2_ref, slot_ref, xe_ref, gs_ref, cum_ref, slot_t_ref):
    lane_row = _iota((1, LANES), 1)
    a = aff_ref[0:n, :]
    for s in range(1, nseq):
        a = a + pltpu.roll(aff_ref[s * n:(s + 1) * n, :], N_EXPERTS * s, 1)
    capf = jnp.float32(cap)
    used = N_EXPERTS * nseq
    fold = LANES // used
    folded = a[0:n // fold, :]
    for k in range(1, fold):
        folded = folded + pltpu.roll(a[k * (n // fold):(k + 1) * (n // fold), :], used * k, 1)

    def bisect(_, lohi):
        lo, hi = lohi
        mid = lo + lax.shift_right_logical(hi - lo, 1)
        cnt = jnp.sum(jnp.where(folded >= pltpu.bitcast(mid, F32), 1.0, 0.0), axis=0, keepdims=True)
        width = used
        while width < LANES:
            cnt = cnt + pltpu.roll(cnt, width, 1)
            width *= 2
        ok = cnt >= capf
        return jnp.where(ok, mid, lo), jnp.where(ok, hi, mid)

    lo0 = jnp.zeros((1, LANES), jnp.int32)
    hi0 = jnp.full((1, LANES), 0x3F800001, jnp.int32)
    lo, hi = lax.fori_loop(0, 31, bisect, (lo0, hi0))
    gt = a >= pltpu.bitcast(hi, F32)
    eq = (a >= pltpu.bitcast(lo, F32)) & jnp.logical_not(gt)
    need = capf - jnp.sum(jnp.where(gt, 1.0, 0.0), axis=0, keepdims=True)
    tri = jnp.where(_iota((CUM_BLK, CUM_BLK), 0) >= _iota((CUM_BLK, CUM_BLK), 1), 1.0, 0.0).astype(BF16)

    def cumsum_rows(flags):
        carry = jnp.zeros((1, LANES), F32)
        for rb in range(n // CUM_BLK):
            rows = slice(rb * CUM_BLK, (rb + 1) * CUM_BLK)
            part = _dot(tri, flags[rows, :].astype(BF16)) + carry
            cum_ref[rows, :] = part
            carry = part[CUM_BLK - 1:CUM_BLK, :]
        return cum_ref[...]

    eq_rank = cumsum_rows(jnp.where(eq, 1.0, 0.0))
    sel = gt | (eq & (eq_rank <= need))
    pos = cumsum_rows(jnp.where(sel, 1.0, 0.0))
    slot = jnp.where(sel, pos - 1.0, -1.0)
    for s in range(nseq):
        own = slot if s == 0 else pltpu.roll(slot, LANES - N_EXPERTS * s, 1)
        slot_ref[s * n:(s + 1) * n, :] = jnp.where(lane_row < N_EXPERTS, own, -1.0)
    slot_t_ref[...] = slot.T
    a_hi, a_lo = _split_bf16(a)
    a_hilo = jnp.concatenate([a_hi, a_lo], axis=1)
    ones = jnp.ones((LANES, LANES), BF16)
    gexp = GROUP_ROWS // cap
    shift = int(np.log2(cap))
    row_e = lax.shift_right_logical(_iota((GROUP_ROWS, LANES), 0), shift)
    slot_id = _iota((cap, n), 0).astype(F32)
    lane = _iota((GROUP_ROWS, LANES), 1)
    def build_onehot(s, g):
        pieces = []
        for j in range(gexp):
            e_lane = N_EXPERTS * s + g * gexp + j
            mine_row = jnp.broadcast_to(slot_t_ref[e_lane:e_lane + 1, :], (cap, n))
            pieces.append(jnp.where(mine_row == slot_id, 1.0, 0.0).astype(BF16))
        return pieces[0] if gexp == 1 else jnp.concatenate(pieces, axis=0)

    items = [(s, g) for s in range(nseq) for g in range(N_EXPERTS * cap // GROUP_ROWS)]
    onehot_next = build_onehot(*items[0])
    for idx, (s, g) in enumerate(items):
        onehot = onehot_next
        if idx + 1 < len(items):
            onehot_next = build_onehot(*items[idx + 1])
        xe = _dot(onehot, h2_ref[s * n:(s + 1) * n, :]).astype(BF16)
        gboth = _dot(onehot, a_hilo)
        mine = lane == row_e + (g * gexp + N_EXPERTS * s)
        g_hi, g_lo = _split_bf16(jnp.where(mine, gboth[:, 0:LANES] + gboth[:, LANES:2 * LANES], 0.0))
        gsb = _dot(g_hi, ones) + _dot(g_lo, ones)
        for j in range(gexp):
            xe_ref[g * gexp + j, s * cap:(s + 1) * cap, :] = xe[j * cap:(j + 1) * cap, :]
            gs_ref[g * gexp + j, s * cap:(s + 1) * cap, :] = gsb[j * cap:(j + 1) * cap, :]


def _route_call(aff, h2, n_seq_tokens):
    t = aff.shape[0]
    n = n_seq_tokens
    b = t // n
    cap = EC_CAPACITY * n // N_EXPERTS
    nseq = min(b, LANES // N_EXPERTS, max(1, 2048 // n))
    return pl.pallas_call(
        functools.partial(_route_kernel, n, cap, nseq),
        grid=(b // nseq,),
        in_specs=[pl.BlockSpec((nseq * n, LANES), lambda i: (i, 0)), pl.BlockSpec((nseq * n, D), lambda i: (i, 0))],
        out_specs=[pl.BlockSpec((nseq * n, LANES), lambda i: (i, 0)),
                   pl.BlockSpec((N_EXPERTS, nseq * cap, D), lambda i: (0, i, 0)),
                   pl.BlockSpec((N_EXPERTS, nseq * cap, LANES), lambda i: (0, i, 0))],
        out_shape=[jax.ShapeDtypeStruct((t, LANES), F32),
                   jax.ShapeDtypeStruct((N_EXPERTS, b * cap, D), BF16),
                   jax.ShapeDtypeStruct((N_EXPERTS, b * cap, LANES), F32)],
        scratch_shapes=[pltpu.VMEM((n, LANES), F32), pltpu.VMEM((LANES, n), F32)],
        compiler_params=_params(1),
        name="route",
    )(aff, h2)


FF_TILE = 512
FFN_ROWS = 512


def _ffn_kernel(xc_ref, xl_ref, gc_ref, gl_ref, wg_ref, wu_ref, wd_ref, yc_ref, yl_ref, wgb_ref, wub_ref, wdb_ref):
    chunks = ([(xc_ref, gc_ref, yc_ref, r0) for r0 in range(0, xc_ref.shape[0], FFN_ROWS)]
              + [(xl_ref, gl_ref, yl_ref, r0) for r0 in range(0, xl_ref.shape[0], FFN_ROWS)])
    tiles = FF // FF_TILE
    units = [(ci, ft) for ci in range(len(chunks)) for ft in range(tiles)]
    for ft in range(tiles):
        cols = slice(ft * FF_TILE, (ft + 1) * FF_TILE)
        wgb_ref[:, cols] = wg_ref[:, cols].astype(BF16)
        wub_ref[:, cols] = wu_ref[:, cols].astype(BF16)

    def gate_up(unit):
        ci, ft = unit
        x_ref, _, _, r0 = chunks[ci]
        x = x_ref[r0:r0 + FFN_ROWS, :]
        cols = slice(ft * FF_TILE, (ft + 1) * FF_TILE)
        return _dot(x, wgb_ref[:, cols]), _dot(x, wub_ref[:, cols])

    nxt = gate_up(units[0])
    for ft in range(tiles):
        rows = slice(ft * FF_TILE, (ft + 1) * FF_TILE)
        wdb_ref[rows, :] = wd_ref[rows, :].astype(BF16)
    total = None
    for idx, (ci, ft) in enumerate(units):
        a, up = nxt
        if idx + 1 < len(units):
            nxt = gate_up(units[idx + 1])
        contrib = _dot((_silu(a) * up).astype(BF16), wdb_ref[ft * FF_TILE:(ft + 1) * FF_TILE, :])
        total = contrib if ft == 0 else total + contrib
        if ft == tiles - 1:
            _, g_ref, y_ref, r0 = chunks[ci]
            rows = slice(r0, r0 + FFN_ROWS)
            gate = jnp.concatenate([g_ref[rows, :]] * (D // LANES), axis=1)
            y_ref[rows, :] = (total * gate).astype(BF16)


def _ffn_call(l, xe_c, xe_l, gs_c, gs_l, w_gate, w_up, w_down):
    rc = xe_c.shape[1]
    rl = xe_l.shape[1]
    ex = lambda r, w: pl.BlockSpec((None, r, w), lambda e: (e, 0, 0))
    wspec = lambda rows, cols: pl.BlockSpec((None, None, rows, cols), lambda e: (l, e, 0, 0))
    return pl.pallas_call(
        _ffn_kernel,
        grid=(N_EXPERTS,),
        in_specs=[ex(rc, D), ex(rl, D), ex(rc, LANES), ex(rl, LANES), wspec(D, FF), wspec(D, FF), wspec(FF, D)],
        out_specs=[ex(rc, D), ex(rl, D)],
        out_shape=[jax.ShapeDtypeStruct((N_EXPERTS, rc, D), BF16), jax.ShapeDtypeStruct((N_EXPERTS, rl, D), BF16)],
        scratch_shapes=[pltpu.VMEM((D, FF), BF16), pltpu.VMEM((D, FF), BF16), pltpu.VMEM((FF, D), BF16)],
        compiler_params=_params(1),
        name="ffn",
    )(xe_c, xe_l, gs_c, gs_l, w_gate, w_up, w_down)


COMB_ROWS = 256


def _scatter_rows(cap, ye_ref, slot, seq=0):
    own = slice(seq * cap, (seq + 1) * cap)
    gexp = GROUP_ROWS // cap
    shift = int(np.log2(cap))
    col_e = lax.shift_right_logical(_iota((LANES, GROUP_ROWS), 1), shift)
    col_s = (_iota((1, GROUP_ROWS), 1) & (cap - 1)).astype(F32)
    lane_e = _iota((LANES, GROUP_ROWS), 0)
    sb = slot.astype(BF16)

    def build_onehot_t(g):
        expand_m = jnp.where(lane_e == col_e + g * gexp, 1.0, 0.0).astype(BF16)
        return jnp.where(_dot(sb, expand_m) == col_s, 1.0, 0.0).astype(BF16)

    ngroups = N_EXPERTS * cap // GROUP_ROWS
    acc = jnp.zeros((COMB_ROWS, D), F32)
    onehot_next = build_onehot_t(0)
    for g in range(ngroups):
        onehot_t = onehot_next
        if g + 1 < ngroups:
            onehot_next = build_onehot_t(g + 1)
        if cap % LANES == 0:
            for j in range(gexp):
                acc = acc + _dot(onehot_t[:, j * cap:(j + 1) * cap], ye_ref[g * gexp + j, own, :])
        else:
            ye = jnp.concatenate([ye_ref[g * gexp + j, own, :] for j in range(gexp)], axis=0)
            acc = acc + _dot(onehot_t, ye)
    return acc


def _combine_kernel(cap, final, ye_ref, slot_ref, x1_ref, mod_ref, *rest):
    if final:
        fg_ref, x2_ref, yf_ref = rest
    else:
        (x2_ref,) = rest
    mod = mod_ref[...]
    x2 = x1_ref[...] + mod[:, 5 * D:6 * D] * _scatter_rows(cap, ye_ref, slot_ref[...])
    x2_ref[...] = x2
    if final:
        yf_ref[...] = _rms(x2) * fg_ref[...]


def _combine_call(ye, slot, x1, mod_p, n_seq_tokens, final_g):
    t = x1.shape[0]
    n = n_seq_tokens
    b = t // n
    nr = n // COMB_ROWS
    cap = EC_CAPACITY * n // N_EXPERTS
    bm = mod_p.shape[0]
    final = final_g is not None
    row = lambda i, j: (i * nr + j, 0)
    ins = [ye, slot, x1, mod_p]
    in_specs = [
        pl.BlockSpec((N_EXPERTS, cap, D), lambda i, j: (0, i, 0)),
        pl.BlockSpec((COMB_ROWS, LANES), row),
        pl.BlockSpec((COMB_ROWS, D), row),
        pl.BlockSpec((None, 1, 6 * D), (lambda i, j: (i, 0, 0)) if bm > 1 else (lambda i, j: (0, 0, 0))),
    ]
    out_shape = [jax.ShapeDtypeStruct((t, D), F32)]
    out_specs = [pl.BlockSpec((COMB_ROWS, D), row)]
    if final:
        ins.append(final_g)
        in_specs.append(pl.BlockSpec((1, D), lambda i, j: (0, 0)))
        out_shape.append(jax.ShapeDtypeStruct((t, D), F32))
        out_specs.append(pl.BlockSpec((COMB_ROWS, D), row))
    return pl.pallas_call(
        functools.partial(_combine_kernel, cap, final),
        grid=(b, nr),
        in_specs=in_specs,
        out_specs=out_specs,
        out_shape=out_shape,
        compiler_params=_params(2),
        name="combine",
    )(*ins)


def _rot_cols(w):
    a, b, c, d = (w[..., 8 * i:8 * (i + 1)] for i in range(4))
    return jnp.concatenate([-b, a, -d, c], axis=-1)


def _rope_tables(n):
    rows = n // GRID_W
    row = jnp.repeat(jnp.arange(rows, dtype=F32), GRID_W)
    col = jnp.tile(jnp.arange(GRID_W, dtype=F32), rows)
    inv = ROPE_THETA ** (-jnp.arange(0, ROPE // 2, 2, dtype=F32) / (ROPE // 2))
    ra = row[:, None] * inv
    ca = col[:, None] * inv
    ang = jnp.concatenate([ra, ra, ca, ca], axis=-1)
    cos, sin = jnp.cos(ang), jnp.sin(ang)
    z32, z64, z96 = (jnp.zeros((n, w), F32) for w in (32, 64, 96))
    cq = jnp.concatenate([jnp.ones((n, 64), F32), cos, z32], axis=1)
    sq = jnp.concatenate([z64, sin, z32], axis=1)
    ck = jnp.concatenate([z64, cos, z32], axis=1)
    sk = jnp.concatenate([z96, sin], axis=1)
    return cq, sq, ck, sk


def _retention_tables(p_f, p_b):
    pos = jnp.arange(CHUNK, dtype=F32)
    diff = pos[:, None] - pos[None, :]

    def one(p, backward):
        lg = jnp.log1p(-jnp.exp2(p.astype(F32)))[:, :, None]
        dd = -diff if backward else diff
        intra = jnp.where(dd >= 0, jnp.exp(jnp.maximum(dd, 0.0) * lg[..., None]), 0.0)
        qexp = (CHUNK - pos) if backward else (pos + 1.0)
        kexp = pos if backward else (CHUNK - 1.0 - pos)
        qd = jnp.exp(qexp * lg)
        kd = jnp.exp(kexp * lg)
        cd = jnp.exp(CHUNK * lg)
        nl = p.shape[0]
        intra_w = jnp.swapaxes(intra, 1, 2).reshape(nl, CHUNK, HEADS * CHUNK)
        qd_w = jnp.repeat(jnp.swapaxes(qd, 1, 2), HEAD_W, axis=2)
        kd_t = jnp.repeat(kd, HEAD_W, axis=1)
        cdw = jnp.broadcast_to(jnp.repeat(cd, HEAD_W, axis=1), (nl, 256, 256))
        return intra_w, qd_w, kd_t, cdw

    f = one(p_f, False)
    b = one(p_b, True)
    return tuple(jnp.stack([x, y], axis=1) for x, y in zip(f, b))


def _prepare(norm1_g, w_in_t, sg_norm_g, sg_w, sg_b, conv_w, conv_b, conv_ln_g, conv_ln_b, conv_pw, ret_decay_f,
             ret_decay_b, ret_gn_g, q_norm_g, w_uq, kv_norm_g, w_ukv, norm2_g, router):
    L = DEPTH
    row = lambda a: a.reshape(L, 1, -1)
    tail_src = w_in_t[:, MAIN_W:, :]
    kpe_rows = tail_src[:, Q_LORA + KV_LORA:, :]
    kpe_rot = jnp.swapaxes(_rot_cols(jnp.swapaxes(kpe_rows, 1, 2)), 1, 2)
    w_tail = jnp.concatenate([tail_src[:, Q_LORA:Q_LORA + KV_LORA, :], tail_src[:, :Q_LORA, :], kpe_rows, kpe_rot],
                             axis=1)
    uq = w_uq.reshape(L, Q_LORA, HEADS, HEAD_W + ROPE)
    pad_rows = lambda a: jnp.pad(a, ((0, 0), (0, 256 - Q_LORA), (0, 0)))
    wq = pad_rows(jnp.pad(uq, ((0, 0), (0, 0), (0, 0), (0, LANES - HEAD_W - ROPE))).reshape(L, Q_LORA, 512))
    uq_rot = jnp.pad(_rot_cols(uq[..., HEAD_W:]), ((0, 0), (0, 0), (0, 0), (HEAD_W, LANES - HEAD_W - ROPE)))
    wqr = pad_rows(uq_rot.reshape(L, Q_LORA, 512))
    ukv = w_ukv.reshape(L, KV_LORA, HEADS, 2 * HEAD_W)
    wk = jnp.pad(ukv[..., :HEAD_W], ((0, 0), (0, 0), (0, 0), (0, LANES - HEAD_W))).reshape(L, KV_LORA, 512)
    wv = ukv[..., HEAD_W:].reshape(L, KV_LORA, 256)
    intra, qd, kd, cd = _retention_tables(ret_decay_f, ret_decay_b)
    r_pad = jnp.pad(router, ((0, 0), (0, 0), (0, LANES - N_EXPERTS)))
    r_hi = r_pad.astype(BF16)
    r_lo = (r_pad - r_hi.astype(F32)).astype(BF16)
    return dict(
        norm1_g=row(norm1_g), w_tail=w_tail, kv_g=row(kv_norm_g),
        q_g=jnp.pad(row(q_norm_g), ((0, 0), (0, 0), (0, 256 - Q_LORA))),
        wq=wq.astype(BF16), wqr=wqr.astype(BF16), wk=wk.astype(BF16), wv=wv.astype(BF16),
        sg_g=row(sg_norm_g), sg_w=sg_w.astype(BF16), sg_bm=jnp.repeat(jnp.swapaxes(sg_b, 1, 2), HEAD_W, axis=2),
        conv_w=jnp.pad(conv_w, ((0, 0), (0, 1), (0, 0))), conv_b=row(conv_b), conv_ln_g=row(conv_ln_g),
        conv_ln_b=row(conv_ln_b), conv_pw=conv_pw.astype(BF16),
        ret_intra=intra, ret_qd=qd, ret_kdt=kd, ret_cd=cd, ret_gn_g=row(ret_gn_g),
        norm2_g=row(norm2_g), r_hilo=jnp.concatenate([r_hi, r_lo], axis=2),
    )


def _block_diag_states(state):
    eye = jnp.eye(HEADS, dtype=state.dtype)
    bd = state[:, :, :, :, None, :] * eye[None, None, :, None, :, None]
    return bd.reshape(state.shape[0], 2, HEADS * HEAD_W, HEADS * HEAD_W)


def _mixers(l, x, mod_p, n, W, w_in, w_out, rope_tabs, s0_bd, cache, results, pending=None):
    outs = _inproj_call(l, x, mod_p, n, W, w_in, rope_tabs, None if results is None else results[:2], pending)
    if pending is not None:
        x = outs[-1]
    ya, cv, rqkv, rg, qt, k, vt = outs[:7]
    yb = _conv_call(l, cv, n, W)
    ret_out = _ret_call(l, rqkv, rg, n, W, s0_bd, None if results is None else results[2])
    yd = _attn_call(l, qt, k, vt, cache, n)
    x1, h2, aff = _outproj_call(l, (ya, yb, ret_out[0], yd), x, mod_p, n, W, w_out)
    slot, xe, gs = _route_call(aff, h2, n)
    new_results = None if results is None else (outs[7], outs[8], ret_out[1])
    return x1, slot, xe, gs, new_results


def kernel(x_prompt, x_sample, cache_mla_ckv, cache_mla_kpe, state_ret, c, c_ctx, w_mod, b_mod, norm1_g, w_in, sg_norm_g, sg_w, sg_b, conv_w, conv_b, conv_ln_g, conv_ln_b, conv_pw, ret_decay_f, ret_decay_b, ret_gn_g, q_norm_g, w_uq, kv_norm_g, w_ukv, w_out, norm2_g, router, w_gate, w_up, w_down, final_norm_g):
    bc, nc_tok, _ = x_prompt.shape
    bl, nl_tok, _ = x_sample.shape
    w_in = jnp.swapaxes(w_in, 1, 2)
    W = _prepare(norm1_g, w_in, sg_norm_g, sg_w, sg_b, conv_w, conv_b, conv_ln_g, conv_ln_b, conv_pw, ret_decay_f,
                 ret_decay_b, ret_gn_g, q_norm_g, w_uq, kv_norm_g, w_ukv, norm2_g, router)
    cvec = jnp.zeros((8, D), F32).at[0].set(c_ctx).at[1:1 + bl].set(c)
    mod_all = _mod_call(cvec, w_mod, b_mod)
    rope_tabs = _rope_tables(nl_tok)
    kpe128 = jnp.pad(cache_mla_kpe, ((0, 0), (0, 0), (0, 0), (HEAD_W, LANES - HEAD_W - ROPE)))
    cache = _kvcache_call(cache_mla_ckv, kpe128, W["wk"], W["wv"])
    s0_all = _block_diag_states(jnp.swapaxes(state_ret, 0, 1).reshape(DEPTH * bl, 2, HEADS, HEAD_W, HEAD_W))
    s0_all = s0_all.reshape(DEPTH, bl, 2, 256, 256)
    fg = final_norm_g.reshape(1, D)
    xp = x_prompt.reshape(bc * nc_tok, D)
    xs = x_sample.reshape(bl * nl_tok, D)
    results = (jnp.zeros((bc, DEPTH, nc_tok, KV_LORA), F32), jnp.zeros((bc, DEPTH, nc_tok, ROPE), F32),
               jnp.zeros((bc, DEPTH, 2, HEADS * HEAD_W, HEAD_W), F32))
    yp = ys = None
    pend_c = pend_l = None
    for l in range(DEPTH):
        mod_c = mod_all[l, 0:1].reshape(1, 1, 6 * D)
        mod_l = mod_all[l, 1:1 + bl].reshape(bl, 1, 6 * D)
        x1c, slot_c, xe_c, gs_c, results = _mixers(l, xp, mod_c, nc_tok, W, w_in, w_out, None, None, None, results,
                                                   pend_c)
        x1l, slot_l, xe_l, gs_l, _ = _mixers(l, xs, mod_l, nl_tok, W, w_in, w_out, rope_tabs, s0_all[l], cache, None,
                                             pend_l)
        ye_c, ye_l = _ffn_call(l, xe_c, xe_l, gs_c, gs_l, w_gate, w_up, w_down)
        if l == DEPTH - 1:
            yp = _combine_call(ye_c, slot_c, x1c, mod_c, nc_tok, fg)[1]
            ys = _combine_call(ye_l, slot_l, x1l, mod_l, nl_tok, fg)[1]
        else:
            xp = xs = None
            pend_c, pend_l = (ye_c, slot_c, x1c, mod_c), (ye_l, slot_l, x1l, mod_l)
    ckv_all, kpe_all, ret_all = results
    return (yp.reshape(bc, nc_tok, D), ys.reshape(bl, nl_tok, D), ckv_all, kpe_all,
            ret_all.reshape(bc, DEPTH, 2, HEADS, HEAD_W, HEAD_W))
```

```python
import functools

import jax
import jax.numpy as jnp
import numpy as np
from jax import lax
from jax.experimental import pallas as pl
from jax.experimental.pallas import tpu as pltpu

F32 = jnp.float32
BF16 = jnp.bfloat16

D = 1024
DEPTH = 4
CHUNK = 128
EPS = 1e-6
GRID_W = 64
CONV_K = 31
HEADS = 4
HEAD_W = 64
Q_LORA = 192
KV_LORA = 128
ROPE = 32
ROPE_THETA = 10000.0
N_EXPERTS = 16
FF = 1024
EC_CAPACITY = 2
MAIN_W = 2304
TAIL_W = 384
ATT_SCALE = (HEAD_W + ROPE) ** -0.5
LOG2_E = 1.4426950408889634
LANES = 128
SUBLANES = 8
VMEM_LIMIT = 56 * 1024 * 1024


def _iota(shape, dim):
    return lax.broadcasted_iota(jnp.int32, shape, dim)


def _dot(a, b):
    return jnp.dot(a, b, preferred_element_type=F32)


def _dot_nt(a, b):
    return lax.dot_general(a, b, (((1,), (1,)), ((), ())), preferred_element_type=F32)


def _split_bf16(x):
    hi = x.astype(BF16)
    lo = (x - hi.astype(F32)).astype(BF16)
    return hi, lo


def _sigmoid(x):
    return 1.0 / (1.0 + jnp.exp(-x))


def _silu(x):
    return x * _sigmoid(x)


def _rms(x):
    return x * lax.rsqrt(jnp.mean(x * x, axis=-1, keepdims=True) + EPS)


def _wspec(arr, l):
    nd = arr.ndim
    return pl.BlockSpec((None,) + tuple(arr.shape[1:]), lambda *_: (l,) + (0,) * (nd - 1))


def _params(n_axes, sem="parallel"):
    return pltpu.CompilerParams(dimension_semantics=(sem,) * n_axes, vmem_limit_bytes=VMEM_LIMIT)


def _mod_kernel(c_ref, w_ref, b_ref, o_ref):
    cv = c_ref[...]
    s_hi, s_lo = _split_bf16(_silu(cv))
    w_hi, w_lo = _split_bf16(w_ref[...])
    o_ref[...] = _dot(s_hi, w_hi) + _dot(s_lo, w_hi) + _dot(s_hi, w_lo) + b_ref[...]


def _mod_call(cvec, w_mod, b_mod):
    nt = 1536
    return pl.pallas_call(
        _mod_kernel,
        grid=(DEPTH, 6 * D // nt),
        in_specs=[
            pl.BlockSpec((8, D), lambda l, j: (0, 0)),
            pl.BlockSpec((None, D, nt), lambda l, j: (l, 0, j)),
            pl.BlockSpec((None, 1, nt), lambda l, j: (l, 0, j)),
        ],
        out_specs=pl.BlockSpec((None, 8, nt), lambda l, j: (l, 0, j)),
        out_shape=jax.ShapeDtypeStruct((DEPTH, 8, 6 * D), F32),
        compiler_params=_params(2),
        name="mod",
    )(cvec, w_mod, b_mod.reshape(DEPTH, 1, 6 * D))


N_MAIN_BLOCKS = MAIN_W // 256
PROJ_ROWS = 512
OUT_ROWS = 1024
RET_K_BLOCK = 5


def _inproj_kernel(rope, moe_cap, *refs):
    if moe_cap is None:
        x_ref, refs = refs[0], refs[1:]
    else:
        ye_ref, slot_ref, x1_ref, modp_ref = refs[0:4]
        refs = refs[4:]
    (mod_ref, g1_ref, w_ref, wt_ref, kvg_ref, qg_ref, wq_ref, wqr_ref, wk_ref, wv_ref, sgg_ref, sgw_ref,
     sgb_ref) = refs[0:13]
    rest = list(refs[13:])
    wb_ref = rest.pop()
    x2_ref = rest.pop() if moe_cap is not None else None
    if rope:
        cq_ref, sq_ref, ck_ref, sk_ref, ya_ref, cv_ref, rqkv_ref, rg_ref, qt_ref, k_ref, vt_ref = rest
    else:
        _, _, ya_ref, cv_ref, rqkv_ref, rg_ref, qt_ref, k_ref, vt_ref, ckv_ref, kpe_ref = rest

    @pl.when(pl.program_id(0) == 0)
    def _():
        for cb in range(N_MAIN_BLOCKS):
            blk = w_ref[cb * 256:(cb + 1) * 256, :]
            if cb == RET_K_BLOCK:
                blk = blk * (HEAD_W ** -0.5)
            wb_ref[cb * 256:(cb + 1) * 256, :] = blk.astype(BF16)
        wb_ref[MAIN_W:MAIN_W + TAIL_W, :] = wt_ref[...].astype(BF16)

    if moe_cap is None:
        x = x_ref[...]
    else:
        per_block_seq = ye_ref.shape[1] > moe_cap
        moe = [_scatter_rows(moe_cap, ye_ref, slot_ref[rb * COMB_ROWS:(rb + 1) * COMB_ROWS, :],
                             rb if per_block_seq else 0) for rb in range(x1_ref.shape[0] // COMB_ROWS)]
        x = x1_ref[...] + modp_ref[...][:, 5 * D:6 * D] * (moe[0] if len(moe) == 1 else jnp.concatenate(moe, axis=0))
        x2_ref[...] = x
    mod = mod_ref[...]
    h = _rms(x) * g1_ref[...]
    h = h * (1.0 + mod[:, D:2 * D]) + mod[:, 0:D]
    hb = h.astype(BF16)
    proj = lambda cb: _dot_nt(hb, wb_ref[cb * 256:(cb + 1) * 256, :])
    v_gate = proj(1)
    tail = _dot_nt(hb, wb_ref[MAIN_W:MAIN_W + TAIL_W, :])
    u = proj(0)
    cv_ref[:, 0:256] = proj(2)
    cv_ref[:, 256:512] = proj(3)
    for j in range(3):
        rqkv_ref[:, j * 256:(j + 1) * 256] = proj(4 + j).astype(BF16)
    rg_ref[:, 0:256] = proj(7)
    rg_ref[:, 256:512] = proj(8)
    vn = _rms(v_gate) * sgg_ref[...]
    group = lax.shift_right_logical(_iota((1, 256), 1), 6)
    for cidx in range(x.shape[0] // CHUNK):
        rows = slice(cidx * CHUNK, (cidx + 1) * CHUNK)
        acc = jnp.zeros((CHUNK, 256), F32)
        for g in range(HEADS):
            acc = acc + _dot(sgw_ref[g], jnp.where(group == g, vn[rows, :], 0.0).astype(BF16))
        ya_ref[rows, :] = u[rows, :] * (acc + sgb_ref[...])
    ckv_n = _rms(tail[:, 0:128]) * kvg_ref[...]
    c256 = tail[:, 128:384]
    lane256 = _iota((1, 256), 1)
    ms = jnp.sum(jnp.where(lane256 < Q_LORA, c256 * c256, 0.0), axis=-1, keepdims=True) * (1.0 / Q_LORA)
    cqn = (c256 * lax.rsqrt(ms + EPS) * qg_ref[...]).astype(BF16)
    ckvb = ckv_n.astype(BF16)
    q = _dot(cqn, wq_ref[...])
    kn = _dot(ckvb, wk_ref[...])
    vt_ref[...] = _dot(ckvb, wv_ref[...]).T.astype(BF16)
    slab = tail[:, 256:384]
    if rope:
        qr = _dot(cqn, wqr_ref[...])
        kx = slab * ck_ref[...] + pltpu.roll(slab * sk_ref[...], LANES - ROPE, 1)
        q = q * jnp.concatenate([cq_ref[...]] * HEADS, axis=1) + qr * jnp.concatenate([sq_ref[...]] * HEADS, axis=1)
    else:
        lane128 = _iota((1, LANES), 1)
        kx = jnp.where((lane128 >= HEAD_W) & (lane128 < HEAD_W + ROPE), slab, 0.0)
        kpe = pltpu.roll(slab, LANES - HEAD_W, 1)[:, 0:ROPE]
        n_out = ckv_ref.shape[1]
        for sq in range(ckv_ref.shape[0]):
            ckv_ref[sq] = ckv_n[sq * n_out:(sq + 1) * n_out, :]
            kpe_ref[sq] = kpe[sq * n_out:(sq + 1) * n_out, :]
    qt_ref[...] = (q * (ATT_SCALE * LOG2_E)).T.astype(BF16)
    for hh in range(HEADS):
        sl = slice(hh * LANES, (hh + 1) * LANES)
        k_ref[:, sl] = (kn[:, sl] + kx).astype(BF16)


def _inproj_call(l, x, mod_p, n_seq_tokens, W, w_in, rope_tabs, cache_out, pending=None):
    tm = PROJ_ROWS
    tps = max(1, n_seq_tokens // tm)
    spt = max(1, tm // n_seq_tokens)
    bm = mod_p.shape[0]
    rope = rope_tabs is not None
    row = lambda i: (i, 0)
    col = lambda i: (0, i)
    mod_spec = lambda: pl.BlockSpec((None, 1, 6 * D), (lambda i: (i // tps, 0, 0)) if bm > 1 else (lambda i: (0, 0, 0)))
    moe_cap = None
    if pending is None:
        t = x.shape[0]
        ins = [x]
        in_specs = [pl.BlockSpec((tm, D), row)]
    else:
        ye, slot, x1, mod_prev = pending
        t = x1.shape[0]
        moe_cap = EC_CAPACITY * n_seq_tokens // N_EXPERTS
        assert n_seq_tokens == COMB_ROWS or n_seq_tokens % tm == 0
        ins = [ye, slot, x1, mod_prev]
        in_specs = [pl.BlockSpec((N_EXPERTS, spt * moe_cap, D), lambda i: (0, i // tps, 0),
                                 pipeline_mode=pl.Buffered(1) if tps > 1 else None),
                    pl.BlockSpec((tm, LANES), row), pl.BlockSpec((tm, D), row), mod_spec()]
    rest = [W["w_tail"], W["kv_g"], W["q_g"], W["wq"], W["wqr"], W["wk"], W["wv"], W["sg_g"], W["sg_w"], W["sg_bm"]]
    ins += [mod_p, W["norm1_g"], w_in] + rest
    in_specs += [
        mod_spec(),
        _wspec(W["norm1_g"], l),
        pl.BlockSpec((None, MAIN_W, D), lambda i: (l, 0, 0), pipeline_mode=pl.Buffered(1)),
        pl.BlockSpec((None, TAIL_W, D), lambda i: (l, 0, 0), pipeline_mode=pl.Buffered(1)),
    ] + [_wspec(a, l) for a in rest[1:]]
    out_shape = [
        jax.ShapeDtypeStruct((t, 256), F32), jax.ShapeDtypeStruct((t, 512), F32),
        jax.ShapeDtypeStruct((t, 768), BF16), jax.ShapeDtypeStruct((t, 512), F32),
        jax.ShapeDtypeStruct((512, t), BF16), jax.ShapeDtypeStruct((t, 512), BF16),
        jax.ShapeDtypeStruct((256, t), BF16),
    ]
    out_specs = [pl.BlockSpec((tm, 256), row), pl.BlockSpec((tm, 512), row), pl.BlockSpec((tm, 768), row),
                 pl.BlockSpec((tm, 512), row), pl.BlockSpec((512, tm), col), pl.BlockSpec((tm, 512), row),
                 pl.BlockSpec((256, tm), col)]
    aliases = {}
    if rope:
        ins += list(rope_tabs)
        in_specs += [pl.BlockSpec((tm, LANES), lambda i: (i % tps, 0))] * 4
    else:
        for acc in cache_out:
            aliases[len(ins)] = len(out_shape)
            ins.append(acc)
            in_specs.append(pl.BlockSpec(memory_space=pl.ANY))
            out_shape.append(jax.ShapeDtypeStruct(acc.shape, acc.dtype))
            out_specs.append(pl.BlockSpec((spt, None, tm // spt, acc.shape[-1]), lambda i: (i // tps, l, i % tps, 0)))
    if pending is not None:
        out_shape.append(jax.ShapeDtypeStruct((t, D), F32))
        out_specs.append(pl.BlockSpec((tm, D), row))
    return pl.pallas_call(
        functools.partial(_inproj_kernel, rope, moe_cap),
        grid=(t // tm,),
        in_specs=in_specs,
        out_specs=out_specs,
        out_shape=out_shape,
        input_output_aliases=aliases,
        scratch_shapes=[pltpu.VMEM((MAIN_W + TAIL_W, D), BF16)],
        compiler_params=_params(1, "arbitrary"),
        name="in_proj",
    )(*ins)


def _kvcache_kernel(ckv_ref, kpe_ref, wk_ref, wv_ref, k_ref, vt_ref):
    cb = ckv_ref[...].astype(BF16)
    kn = _dot(cb, wk_ref[...])
    kx = kpe_ref[...]
    for hh in range(HEADS):
        sl = slice(hh * LANES, (hh + 1) * LANES)
        k_ref[:, sl] = (kn[:, sl] + kx).astype(BF16)
    vt_ref[...] = _dot(cb, wv_ref[...]).T.astype(BF16)


def _kvcache_call(cache_ckv, kpe128, wk, wv):
    b, depth, m, _ = cache_ckv.shape
    blk = lambda w: pl.BlockSpec((None, None, m, w), lambda l, i: (i, l, 0, 0))
    wblk = lambda a: pl.BlockSpec((None,) + tuple(a.shape[1:]), lambda l, i: (l, 0, 0))
    return pl.pallas_call(
        _kvcache_kernel,
        grid=(depth, b),
        in_specs=[blk(128), blk(128), wblk(wk), wblk(wv)],
        out_specs=[pl.BlockSpec((None, None, m, 512), lambda l, i: (l, i, 0, 0)),
                   pl.BlockSpec((None, None, 256, m), lambda l, i: (l, i, 0, 0))],
        out_shape=[jax.ShapeDtypeStruct((depth, b, m, 512), BF16), jax.ShapeDtypeStruct((depth, b, 256, m), BF16)],
        compiler_params=_params(2),
        name="kv_cache",
    )(cache_ckv, kpe128, wk, wv)


HALO = 16


def _conv_kernel(rb, nblk, cur_ref, prev_ref, next_ref, wdw_ref, bdw_ref, lng_ref, lnb_ref, wpw_ref, o_ref, pad_ref,
                 sh_ref):
    i = pl.program_id(0)
    keep_prev = (i % nblk != 0).astype(F32)
    keep_next = (i % nblk != nblk - 1).astype(F32)

    def glu(blk):
        return blk[:, 0:256] * _sigmoid(blk[:, 256:512])

    pad_ref[0:HALO, :] = glu(prev_ref[...]) * keep_prev
    pad_ref[HALO:HALO + rb, :] = glu(cur_ref[...])
    pad_ref[HALO + rb:2 * HALO + rb, :] = glu(next_ref[...]) * keep_next
    span = rb + 2 * HALO - SUBLANES
    for s in range(SUBLANES):
        sh_ref[s] = pad_ref[s:s + span, :]
    off = HALO - CONV_K // 2
    for j in range(rb // CHUNK):
        acc = jnp.zeros((CHUNK, 256), F32)
        for k in range(CONV_K):
            start = j * CHUNK + (off + k) // SUBLANES * SUBLANES
            acc = acc + sh_ref[(off + k) % SUBLANES, start:start + CHUNK, :] * wdw_ref[k:k + 1, :]
        y = acc + bdw_ref[...]
        mu = jnp.mean(y, axis=-1, keepdims=True)
        dlt = y - mu
        var = jnp.mean(dlt * dlt, axis=-1, keepdims=True)
        z = _silu(dlt * lax.rsqrt(var + EPS) * lng_ref[...] + lnb_ref[...])
        o_ref[j * CHUNK:(j + 1) * CHUNK, :] = _dot(z.astype(BF16), wpw_ref[...])


def _conv_call(l, cv, n_seq_tokens, W):
    t = cv.shape[0]
    rb = min(n_seq_tokens, 512)
    nblk = n_seq_tokens // rb
    per = rb // HALO
    last = t // HALO - 1
    ins = [cv, cv, cv, W["conv_w"], W["conv_b"], W["conv_ln_g"], W["conv_ln_b"], W["conv_pw"]]
    return pl.pallas_call(
        functools.partial(_conv_kernel, rb, nblk),
        grid=(t // rb,),
        in_specs=[
            pl.BlockSpec((rb, 512), lambda i: (i, 0)),
            pl.BlockSpec((HALO, 512), lambda i: (jnp.maximum(i * per - 1, 0), 0)),
            pl.BlockSpec((HALO, 512), lambda i: (jnp.minimum((i + 1) * per, last), 0)),
        ] + [_wspec(a, l) for a in ins[3:]],
        out_specs=pl.BlockSpec((rb, 256), lambda i: (i, 0)),
        out_shape=jax.ShapeDtypeStruct((t, 256), F32),
        scratch_shapes=[pltpu.VMEM((rb + 2 * HALO, 256), F32),
                        pltpu.VMEM((SUBLANES, rb + 2 * HALO - SUBLANES, 256), F32)],
        compiler_params=_params(1),
        name="conv",
    )(*ins)


def _ret_kernel(nb, nc, has_s0, rq_ref, rk_ref, rv_ref, gf_ref, gb_ref, intra_ref, qd_ref, kdt_ref, cd_ref, gng_ref,
                *rest):
    if has_s0:
        s0_ref, y_ref, s_ref, kvb_ref, of_ref, ob_ref = rest
        st_ref = None
    else:
        _, y_ref, st_ref, s_ref, kvb_ref, of_ref, ob_ref = rest
    same_head = lax.shift_right_logical(_iota((256, 256), 0), 6) == lax.shift_right_logical(_iota((256, 256), 1), 6)
    avg = jnp.where(same_head, 1.0 / HEAD_W, 0.0).astype(BF16)
    row_head = lax.shift_right_logical(_iota((256, CHUNK), 0), 6)
    lane_head = lax.shift_right_logical(_iota((CHUNK, 256), 1), 6)

    def head_mean(x):
        hi, lo = _split_bf16(x)
        return _dot(hi, avg) + _dot(lo, avg)

    def rows_of(sq, cidx):
        return pl.ds(pl.multiple_of(sq * nc * CHUNK + cidx * CHUNK, CHUNK), CHUNK)

    def compact(st):
        return st[:, 0:64] + st[:, 64:128] + st[:, 128:192] + st[:, 192:256]

    for sq in range(nb):
        s_ref[sq] = s0_ref[sq, 0] if has_s0 else jnp.zeros((256, 256), F32)

    def fwd_one(sq, cidx):
        rows = rows_of(sq, cidx)
        qb = rq_ref[rows, :]
        vb = rv_ref[rows, :]
        kt = rk_ref[rows, :].astype(F32).T
        kbd = jnp.concatenate([jnp.where(row_head == hh, kt, 0.0).astype(BF16) for hh in range(HEADS)], axis=1)
        s = _dot(qb, kbd)
        p = jnp.concatenate([(s * intra_ref[0]).astype(BF16), (s * intra_ref[1]).astype(BF16)], axis=0)
        zero = jnp.zeros_like(vb)
        vbd = jnp.concatenate([jnp.where(lane_head == hh, vb, zero) for hh in range(HEADS)], axis=0)
        inner = _dot(p, vbd)
        kts = jnp.concatenate([(kt * kdt_ref[0]).astype(BF16), (kt * kdt_ref[1]).astype(BF16)], axis=0)
        kv = _dot(kts, vb)
        kvb_ref[sq * nc + cidx] = jnp.where(same_head, kv[256:512, :], 0.0)
        ob_ref[rows, :] = inner[CHUNK:2 * CHUNK, :]
        st = s_ref[sq]
        of_ref[rows, :] = inner[0:CHUNK, :] + _dot(qb, st.astype(BF16)) * qd_ref[0]
        s_ref[sq] = cd_ref[0] * st + jnp.where(same_head, kv[0:256, :], 0.0)

    def fwd_body(cidx, carry):
        for sq in range(nb):
            fwd_one(sq, cidx)
        return carry

    lax.fori_loop(0, nc, fwd_body, 0, unroll=2)
    for sq in range(nb):
        if st_ref is not None:
            st_ref[sq, 0] = compact(s_ref[sq])
        s_ref[sq] = s0_ref[sq, 1] if has_s0 else jnp.zeros((256, 256), F32)

    def bwd_body(it, carry):
        cidx = nc - 1 - it
        for sq in range(nb):
            rows = rows_of(sq, cidx)
            st = s_ref[sq]
            ob_ref[rows, :] = ob_ref[rows, :] + _dot(rq_ref[rows, :], st.astype(BF16)) * qd_ref[1]
            s_ref[sq] = cd_ref[1] * st + kvb_ref[sq * nc + cidx]
        return carry

    lax.fori_loop(0, nc, bwd_body, 0, unroll=2)
    if st_ref is not None:
        for sq in range(nb):
            st_ref[sq, 1] = compact(s_ref[sq])

    def norm_body(blk, carry):
        rows = pl.ds(pl.multiple_of(blk * 2 * CHUNK, 2 * CHUNK), 2 * CHUNK)
        o = jnp.concatenate([of_ref[rows, :], ob_ref[rows, :]], axis=0)
        dlt = o - head_mean(o)
        nrm = dlt * lax.rsqrt(head_mean(dlt * dlt) + EPS) * gng_ref[...]
        y_ref[rows, :] = (_silu(gf_ref[rows, :]) * nrm[0:2 * CHUNK, :]
                          + _silu(gb_ref[rows, :]) * nrm[2 * CHUNK:4 * CHUNK, :])
        return carry

    lax.fori_loop(0, nb * nc // 2, norm_body, 0, unroll=2)


RET_BLOCK_ROWS = 4096
RET_MAX_SEQS = 8


def _ret_call(l, rqkv, rg, n_seq_tokens, W, s0_bd, state_out):
    t = rqkv.shape[0]
    n = n_seq_tokens
    b = t // n
    nc = n // CHUNK
    nb = min(b, max(1, RET_BLOCK_ROWS // n), RET_MAX_SEQS)
    has_s0 = s0_bd is not None
    col = lambda j: pl.BlockSpec((nb * n, 256), lambda i: (i, j))
    ins = [rqkv] * 3 + [rg] * 2 + [W["ret_intra"], W["ret_qd"], W["ret_kdt"], W["ret_cd"], W["ret_gn_g"]]
    in_specs = [col(0), col(1), col(2), col(0), col(1)] + [_wspec(a, l) for a in ins[5:]]
    out_specs = [pl.BlockSpec((nb * n, 256), lambda i: (i, 0))]
    out_shape = [jax.ShapeDtypeStruct((t, 256), F32)]
    aliases = {}
    if has_s0:
        ins.append(s0_bd)
        in_specs.append(pl.BlockSpec((nb, 2, 256, 256), lambda i: (i, 0, 0, 0)))
    else:
        aliases[len(ins)] = 1
        ins.append(state_out)
        in_specs.append(pl.BlockSpec(memory_space=pl.ANY))
        out_specs.append(pl.BlockSpec((nb, None, 2, 256, HEAD_W), lambda i: (i, l, 0, 0, 0)))
        out_shape.append(jax.ShapeDtypeStruct(state_out.shape, state_out.dtype))
    return pl.pallas_call(
        functools.partial(_ret_kernel, nb, nc, has_s0),
        grid=(b // nb,),
        in_specs=in_specs,
        out_specs=out_specs,
        out_shape=out_shape,
        input_output_aliases=aliases,
        scratch_shapes=[pltpu.VMEM((nb, 256, 256), F32), pltpu.VMEM((nb * nc, 256, 256), F32),
                        pltpu.VMEM((nb * n, 256), F32), pltpu.VMEM((nb * n, 256), F32)],
        compiler_params=_params(1),
        name="ret",
    )(*ins)


ATT_TQ = 256


def _attn_kernel(nparts, nb, qt_ref, *refs):
    k_refs = refs[0:2 * nparts:2]
    vt_refs = refs[1:2 * nparts:2]
    o_ref = refs[2 * nparts]
    n_own = k_refs[-1].shape[0] // nb

    def operands(sq, hh):
        sl = slice(hh * LANES, (hh + 1) * LANES)
        vsl = slice(hh * HEAD_W, (hh + 1) * HEAD_W)
        own = slice(sq * n_own, (sq + 1) * n_own)
        parts = [(k_ref[:, sl], vt_ref[vsl, :]) for k_ref, vt_ref in zip(k_refs[:-1], vt_refs[:-1])]
        return parts + [(k_refs[-1][own, sl], vt_refs[-1][vsl, own])]

    work = [(sq, hh) for sq in range(nb) for hh in range(HEADS)]
    ss_all = [[_dot(kk, qt_ref[hh * LANES:(hh + 1) * LANES, sq * ATT_TQ:(sq + 1) * ATT_TQ])
               for kk, _ in operands(sq, hh)] for sq, hh in work]
    outs = []
    for (sq, hh), ss in zip(work, ss_all):
        m = ss[0].max(axis=0, keepdims=True)
        for s in ss[1:]:
            m = jnp.maximum(m, s.max(axis=0, keepdims=True))
        es = [jnp.exp2(s - m) for s in ss]
        den = es[0].sum(axis=0, keepdims=True)
        for e in es[1:]:
            den = den + e.sum(axis=0, keepdims=True)
        oh = None
        for e, (_, vv) in zip(es, operands(sq, hh)):
            part = _dot(vv, e.astype(BF16))
            oh = part if oh is None else oh + part
        outs.append(oh * (1.0 / den))
    for sq in range(nb):
        o_ref[sq * ATT_TQ:(sq + 1) * ATT_TQ, :] = jnp.concatenate(outs[sq * HEADS:(sq + 1) * HEADS], axis=0).T


def _attn_call(l, qt, k, vt, cache, n_seq_tokens):
    t = k.shape[0]
    n = n_seq_tokens
    tq = ATT_TQ
    nq = n // tq
    b = t // n
    nb = min(b, 8) if (cache is None and nq == 1) else 1
    ins = [qt]
    in_specs = [pl.BlockSpec((512, nb * tq), lambda i, j: (0, i * nq + j))]
    if cache is not None:
        kc, vtc = cache
        m = kc.shape[2]
        ins += [kc, vtc]
        in_specs += [pl.BlockSpec((None, None, m, 512), lambda i, j: (l, i, 0, 0)),
                     pl.BlockSpec((None, None, 256, m), lambda i, j: (l, i, 0, 0))]
    ins += [k, vt]
    in_specs += [pl.BlockSpec((nb * n, 512), lambda i, j: (i, 0)), pl.BlockSpec((256, nb * n), lambda i, j: (0, i))]
    return pl.pallas_call(
        functools.partial(_attn_kernel, len(ins) // 2, nb),
        grid=(b // nb, nq),
        in_specs=in_specs,
        out_specs=pl.BlockSpec((nb * tq, 256), lambda i, j: (i * nq + j, 0)),
        out_shape=jax.ShapeDtypeStruct((t, 256), F32),
        compiler_params=_params(2),
        name="attn",
    )(*ins)


def _outproj_kernel(ya_ref, yb_ref, yc_ref, yd_ref, x_ref, mod_ref, wo_ref, g2_ref, rw_ref,
                    x1_ref, h2_ref, aff_ref, wb_ref):
    @pl.when(pl.program_id(0) == 0)
    def _():
        for rb in range(4):
            wb_ref[rb * 256:(rb + 1) * 256, :] = wo_ref[rb * 256:(rb + 1) * 256, :].astype(BF16)

    mix = _dot(ya_ref[...].astype(BF16), wb_ref[0:256, :])
    mix = mix + _dot(yb_ref[...].astype(BF16), wb_ref[256:512, :])
    mix = mix + _dot(yc_ref[...].astype(BF16), wb_ref[512:768, :])
    mix = mix + _dot(yd_ref[...].astype(BF16), wb_ref[768:1024, :])
    mod = mod_ref[...]
    x1 = x_ref[...] + mod[:, 2 * D:3 * D] * mix
    x1_ref[...] = x1
    h2 = _rms(x1) * g2_ref[...]
    h2 = h2 * (1.0 + mod[:, 4 * D:5 * D]) + mod[:, 3 * D:4 * D]
    h_hi, h_lo = _split_bf16(h2)
    h2_ref[...] = h_hi
    both = _dot(h_hi, rw_ref[...])
    logits = both[:, 0:LANES] + both[:, LANES:2 * LANES] + _dot(h_lo, rw_ref[:, 0:LANES])
    lane = _iota((1, LANES), 1)
    logits = jnp.where(lane < N_EXPERTS, logits, -1e30)
    e = jnp.exp(logits - logits.max(axis=-1, keepdims=True))
    aff_ref[...] = e / e.sum(axis=-1, keepdims=True)


def _outproj_call(l, ys, x, mod_p, n_seq_tokens, W, w_out):
    t = x.shape[0]
    tm = OUT_ROWS
    tps = max(1, n_seq_tokens // tm)
    bm = mod_p.shape[0]
    row = lambda i: (i, 0)
    ins = list(ys) + [x, mod_p, w_out, W["norm2_g"], W["r_hilo"]]
    in_specs = [pl.BlockSpec((tm, 256), row)] * 4 + [
        pl.BlockSpec((tm, D), row),
        pl.BlockSpec((None, 1, 6 * D), (lambda i: (i // tps, 0, 0)) if bm > 1 else (lambda i: (0, 0, 0))),
        pl.BlockSpec((None, D, D), lambda i: (l, 0, 0), pipeline_mode=pl.Buffered(1)),
    ] + [_wspec(a, l) for a in ins[7:]]
    return pl.pallas_call(
        _outproj_kernel,
        grid=(t // tm,),
        in_specs=in_specs,
        out_specs=[pl.BlockSpec((tm, D), row), pl.BlockSpec((tm, D), row), pl.BlockSpec((tm, LANES), row)],
        out_shape=[jax.ShapeDtypeStruct((t, D), F32), jax.ShapeDtypeStruct((t, D), BF16),
                   jax.ShapeDtypeStruct((t, LANES), F32)],
        scratch_shapes=[pltpu.VMEM((D, D), BF16)],
        compiler_params=_params(1, "arbitrary"),
        name="out_proj",
    )(*ins)


GROUP_ROWS = 512
CUM_BLK = 256


def _route_kernel(n, cap, nseq, aff_ref, h2_ref, slot_ref, xe_ref, gs_ref, cum_ref, slot_t_ref):
    lane_row = _iota((1, LANES), 1)
    a = aff_ref[0:n, :]
    for s in range(1, nseq):
        a = a + pltpu.roll(aff_ref[s * n:(s + 1) * n, :], N_EXPERTS * s, 1)
    capf = jnp.float32(cap)
    used = N_EXPERTS * nseq
    fold = LANES // used
    folded = a[0:n // fold, :]
    for k in range(1, fold):
        folded = folded + pltpu.roll(a[k * (n // fold):(k + 1) * (n // fold), :], used * k, 1)

    def bisect(_, lohi):
        lo, hi = lohi
        mid = lo + lax.shift_right_logical(hi - lo, 1)
        cnt = jnp.sum(jnp.where(folded >= pltpu.bitcast(mid, F32), 1.0, 0.0), axis=0, keepdims=True)
        width = used
        while width < LANES:
            cnt = cnt + pltpu.roll(cnt, width, 1)
            width *= 2
        ok = cnt >= capf
        return jnp.where(ok, mid, lo), jnp.where(ok, hi, mid)

    lo0 = jnp.zeros((1, LANES), jnp.int32)
    hi0 = jnp.full((1, LANES), 0x3F800001, jnp.int32)
    lo, hi = lax.fori_loop(0, 31, bisect, (lo0, hi0))
    gt = a >= pltpu.bitcast(hi, F32)
    eq = (a >= pltpu.bitcast(lo, F32)) & jnp.logical_not(gt)
    need = capf - jnp.sum(jnp.where(gt, 1.0, 0.0), axis=0, keepdims=True)
    tri = jnp.where(_iota((CUM_BLK, CUM_BLK), 0) >= _iota((CUM_BLK, CUM_BLK), 1), 1.0, 0.0).astype(BF16)

    def cumsum_rows(flags):
        carry = jnp.zeros((1, LANES), F32)
        for rb in range(n // CUM_BLK):
            rows = slice(rb * CUM_BLK, (rb + 1) * CUM_BLK)
            part = _dot(tri, flags[rows, :].astype(BF16)) + carry
            cum_ref[rows, :] = part
            carry = part[CUM_BLK - 1:CUM_BLK, :]
        return cum_ref[...]

    eq_rank = cumsum_rows(jnp.where(eq, 1.0, 0.0))
    sel = gt | (eq & (eq_rank <= need))
    pos = cumsum_rows(jnp.where(sel, 1.0, 0.0))
    slot = jnp.where(sel, pos - 1.0, -1.0)
    for s in range(nseq):
        own = slot if s == 0 else pltpu.roll(slot, LANES - N_EXPERTS * s, 1)
        slot_ref[s * n:(s + 1) * n, :] = jnp.where(lane_row < N_EXPERTS, own, -1.0)
    slot_t_ref[...] = slot.T
    a_hi, a_lo = _split_bf16(a)
    a_hilo = jnp.concatenate([a_hi, a_lo], axis=1)
    ones = jnp.ones((LANES, LANES), BF16)
    gexp = GROUP_ROWS // cap
    shift = int(np.log2(cap))
    row_e = lax.shift_right_logical(_iota((GROUP_ROWS, LANES), 0), shift)
    slot_id = _iota((cap, n), 0).astype(F32)
    lane = _iota((GROUP_ROWS, LANES), 1)
    def build_onehot(s, g):
        pieces = []
        for j in range(gexp):
            e_lane = N_EXPERTS * s + g * gexp + j
            mine_row = jnp.broadcast_to(slot_t_ref[e_lane:e_lane + 1, :], (cap, n))
            pieces.append(jnp.where(mine_row == slot_id, 1.0, 0.0).astype(BF16))
        return pieces[0] if gexp == 1 else jnp.concatenate(pieces, axis=0)

    items = [(s, g) for s in range(nseq) for g in range(N_EXPERTS * cap // GROUP_ROWS)]
    onehot_next = build_onehot(*items[0])
    for idx, (s, g) in enumerate(items):
        onehot = onehot_next
        if idx + 1 < len(items):
            onehot_next = build_onehot(*items[idx + 1])
        xe = _dot(onehot, h2_ref[s * n:(s + 1) * n, :]).astype(BF16)
        gboth = _dot(onehot, a_hilo)
        mine = lane == row_e + (g * gexp + N_EXPERTS * s)
        g_hi, g_lo = _split_bf16(jnp.where(mine, gboth[:, 0:LANES] + gboth[:, LANES:2 * LANES], 0.0))
        gsb = _dot(g_hi, ones) + _dot(g_lo, ones)
        for j in range(gexp):
            xe_ref[g * gexp + j, s * cap:(s + 1) * cap, :] = xe[j * cap:(j + 1) * cap, :]
            gs_ref[g * gexp + j, s * cap:(s + 1) * cap, :] = gsb[j * cap:(j + 1) * cap, :]


def _route_call(aff, h2, n_seq_tokens):
    t = aff.shape[0]
    n = n_seq_tokens
    b = t // n
    cap = EC_CAPACITY * n // N_EXPERTS
    nseq = min(b, LANES // N_EXPERTS, max(1, 2048 // n))
    return pl.pallas_call(
        functools.partial(_route_kernel, n, cap, nseq),
        grid=(b // nseq,),
        in_specs=[pl.BlockSpec((nseq * n, LANES), lambda i: (i, 0)), pl.BlockSpec((nseq * n, D), lambda i: (i, 0))],
        out_specs=[pl.BlockSpec((nseq * n, LANES), lambda i: (i, 0)),
                   pl.BlockSpec((N_EXPERTS, nseq * cap, D), lambda i: (0, i, 0)),
                   pl.BlockSpec((N_EXPERTS, nseq * cap, LANES), lambda i: (0, i, 0))],
        out_shape=[jax.ShapeDtypeStruct((t, LANES), F32),
                   jax.ShapeDtypeStruct((N_EXPERTS, b * cap, D), BF16),
                   jax.ShapeDtypeStruct((N_EXPERTS, b * cap, LANES), F32)],
        scratch_shapes=[pltpu.VMEM((n, LANES), F32), pltpu.VMEM((LANES, n), F32)],
        compiler_params=_params(1),
        name="route",
    )(aff, h2)


FF_TILE = 512
FFN_ROWS = 512


def _ffn_kernel(xc_ref, xl_ref, gc_ref, gl_ref, wg_ref, wu_ref, wd_ref, yc_ref, yl_ref, wgb_ref, wub_ref, wdb_ref):
    chunks = ([(xc_ref, gc_ref, yc_ref, r0) for r0 in range(0, xc_ref.shape[0], FFN_ROWS)]
              + [(xl_ref, gl_ref, yl_ref, r0) for r0 in range(0, xl_ref.shape[0], FFN_ROWS)])
    tiles = FF // FF_TILE
    units = [(ci, ft) for ci in range(len(chunks)) for ft in range(tiles)]
    for ft in range(tiles):
        cols = slice(ft * FF_TILE, (ft + 1) * FF_TILE)
        wgb_ref[:, cols] = wg_ref[:, cols].astype(BF16)
        wub_ref[:, cols] = wu_ref[:, cols].astype(BF16)

    def gate_up(unit):
        ci, ft = unit
        x_ref, _, _, r0 = chunks[ci]
        x = x_ref[r0:r0 + FFN_ROWS, :]
        cols = slice(ft * FF_TILE, (ft + 1) * FF_TILE)
        return _dot(x, wgb_ref[:, cols]), _dot(x, wub_ref[:, cols])

    nxt = gate_up(units[0])
    for ft in range(tiles):
        rows = slice(ft * FF_TILE, (ft + 1) * FF_TILE)
        wdb_ref[rows, :] = wd_ref[rows, :].astype(BF16)
    total = None
    for idx, (ci, ft) in enumerate(units):
        a, up = nxt
        if idx + 1 < len(units):
            nxt = gate_up(units[idx + 1])
        contrib = _dot((_silu(a) * up).astype(BF16), wdb_ref[ft * FF_TILE:(ft + 1) * FF_TILE, :])
        total = contrib if ft == 0 else total + contrib
        if ft == tiles - 1:
            _, g_ref, y_ref, r0 = chunks[ci]
            rows = slice(r0, r0 + FFN_ROWS)
            gate = jnp.concatenate([g_ref[rows, :]] * (D // LANES), axis=1)
            y_ref[rows, :] = (total * gate).astype(BF16)


def _ffn_call(l, xe_c, xe_l, gs_c, gs_l, w_gate, w_up, w_down):
    rc = xe_c.shape[1]
    rl = xe_l.shape[1]
    ex = lambda r, w: pl.BlockSpec((None, r, w), lambda e: (e, 0, 0))
    wspec = lambda rows, cols: pl.BlockSpec((None, None, rows, cols), lambda e: (l, e, 0, 0))
    return pl.pallas_call(
        _ffn_kernel,
        grid=(N_EXPERTS,),
        in_specs=[ex(rc, D), ex(rl, D), ex(rc, LANES), ex(rl, LANES), wspec(D, FF), wspec(D, FF), wspec(FF, D)],
        out_specs=[ex(rc, D), ex(rl, D)],
        out_shape=[jax.ShapeDtypeStruct((N_EXPERTS, rc, D), BF16), jax.ShapeDtypeStruct((N_EXPERTS, rl, D), BF16)],
        scratch_shapes=[pltpu.VMEM((D, FF), BF16), pltpu.VMEM((D, FF), BF16), pltpu.VMEM((FF, D), BF16)],
        compiler_params=_params(1),
        name="ffn",
    )(xe_c, xe_l, gs_c, gs_l, w_gate, w_up, w_down)


COMB_ROWS = 256


def _scatter_rows(cap, ye_ref, slot, seq=0):
    own = slice(seq * cap, (seq + 1) * cap)
    gexp = GROUP_ROWS // cap
    shift = int(np.log2(cap))
    col_e = lax.shift_right_logical(_iota((LANES, GROUP_ROWS), 1), shift)
    col_s = (_iota((1, GROUP_ROWS), 1) & (cap - 1)).astype(F32)
    lane_e = _iota((LANES, GROUP_ROWS), 0)
    sb = slot.astype(BF16)

    def build_onehot_t(g):
        expand_m = jnp.where(lane_e == col_e + g * gexp, 1.0, 0.0).astype(BF16)
        return jnp.where(_dot(sb, expand_m) == col_s, 1.0, 0.0).astype(BF16)

    ngroups = N_EXPERTS * cap // GROUP_ROWS
    acc = jnp.zeros((COMB_ROWS, D), F32)
    onehot_next = build_onehot_t(0)
    for g in range(ngroups):
        onehot_t = onehot_next
        if g + 1 < ngroups:
            onehot_next = build_onehot_t(g + 1)
        if cap % LANES == 0:
            for j in range(gexp):
                acc = acc + _dot(onehot_t[:, j * cap:(j + 1) * cap], ye_ref[g * gexp + j, own, :])
        else:
            ye = jnp.concatenate([ye_ref[g * gexp + j, own, :] for j in range(gexp)], axis=0)
            acc = acc + _dot(onehot_t, ye)
    return acc


def _combine_kernel(cap, final, ye_ref, slot_ref, x1_ref, mod_ref, *rest):
    if final:
        fg_ref, x2_ref, yf_ref = rest
    else:
        (x2_ref,) = rest
    mod = mod_ref[...]
    x2 = x1_ref[...] + mod[:, 5 * D:6 * D] * _scatter_rows(cap, ye_ref, slot_ref[...])
    x2_ref[...] = x2
    if final:
        yf_ref[...] = _rms(x2) * fg_ref[...]


def _combine_call(ye, slot, x1, mod_p, n_seq_tokens, final_g):
    t = x1.shape[0]
    n = n_seq_tokens
    b = t // n
    nr = n // COMB_ROWS
    cap = EC_CAPACITY * n // N_EXPERTS
    bm = mod_p.shape[0]
    final = final_g is not None
    row = lambda i, j: (i * nr + j, 0)
    ins = [ye, slot, x1, mod_p]
    in_specs = [
        pl.BlockSpec((N_EXPERTS, cap, D), lambda i, j: (0, i, 0)),
        pl.BlockSpec((COMB_ROWS, LANES), row),
        pl.BlockSpec((COMB_ROWS, D), row),
        pl.BlockSpec((None, 1, 6 * D), (lambda i, j: (i, 0, 0)) if bm > 1 else (lambda i, j: (0, 0, 0))),
    ]
    out_shape = [jax.ShapeDtypeStruct((t, D), F32)]
    out_specs = [pl.BlockSpec((COMB_ROWS, D), row)]
    if final:
        ins.append(final_g)
        in_specs.append(pl.BlockSpec((1, D), lambda i, j: (0, 0)))
        out_shape.append(jax.ShapeDtypeStruct((t, D), F32))
        out_specs.append(pl.BlockSpec((COMB_ROWS, D), row))
    return pl.pallas_call(
        functools.partial(_combine_kernel, cap, final),
        grid=(b, nr),
        in_specs=in_specs,
        out_specs=out_specs,
        out_shape=out_shape,
        compiler_params=_params(2),
        name="combine",
    )(*ins)


def _rot_cols(w):
    a, b, c, d = (w[..., 8 * i:8 * (i + 1)] for i in range(4))
    return jnp.concatenate([-b, a, -d, c], axis=-1)


def _rope_tables(n):
    rows = n // GRID_W
    row = jnp.repeat(jnp.arange(rows, dtype=F32), GRID_W)
    col = jnp.tile(jnp.arange(GRID_W, dtype=F32), rows)
    inv = ROPE_THETA ** (-jnp.arange(0, ROPE // 2, 2, dtype=F32) / (ROPE // 2))
    ra = row[:, None] * inv
    ca = col[:, None] * inv
    ang = jnp.concatenate([ra, ra, ca, ca], axis=-1)
    cos, sin = jnp.cos(ang), jnp.sin(ang)
    z32, z64, z96 = (jnp.zeros((n, w), F32) for w in (32, 64, 96))
    cq = jnp.concatenate([jnp.ones((n, 64), F32), cos, z32], axis=1)
    sq = jnp.concatenate([z64, sin, z32], axis=1)
    ck = jnp.concatenate([z64, cos, z32], axis=1)
    sk = jnp.concatenate([z96, sin], axis=1)
    return cq, sq, ck, sk


def _retention_tables(p_f, p_b):
    pos = jnp.arange(CHUNK, dtype=F32)
    diff = pos[:, None] - pos[None, :]

    def one(p, backward):
        lg = jnp.log1p(-jnp.exp2(p.astype(F32)))[:, :, None]
        dd = -diff if backward else diff
        intra = jnp.where(dd >= 0, jnp.exp(jnp.maximum(dd, 0.0) * lg[..., None]), 0.0)
        qexp = (CHUNK - pos) if backward else (pos + 1.0)
        kexp = pos if backward else (CHUNK - 1.0 - pos)
        qd = jnp.exp(qexp * lg)
        kd = jnp.exp(kexp * lg)
        cd = jnp.exp(CHUNK * lg)
        nl = p.shape[0]
        intra_w = jnp.swapaxes(intra, 1, 2).reshape(nl, CHUNK, HEADS * CHUNK)
        qd_w = jnp.repeat(jnp.swapaxes(qd, 1, 2), HEAD_W, axis=2)
        kd_t = jnp.repeat(kd, HEAD_W, axis=1)
        cdw = jnp.broadcast_to(jnp.repeat(cd, HEAD_W, axis=1), (nl, 256, 256))
        return intra_w, qd_w, kd_t, cdw

    f = one(p_f, False)
    b = one(p_b, True)
    return tuple(jnp.stack([x, y], axis=1) for x, y in zip(f, b))


def _prepare(norm1_g, w_in_t, sg_norm_g, sg_w, sg_b, conv_w, conv_b, conv_ln_g, conv_ln_b, conv_pw, ret_decay_f,
             ret_decay_b, ret_gn_g, q_norm_g, w_uq, kv_norm_g, w_ukv, norm2_g, router):
    L = DEPTH
    row = lambda a: a.reshape(L, 1, -1)
    tail_src = w_in_t[:, MAIN_W:, :]
    kpe_rows = tail_src[:, Q_LORA + KV_LORA:, :]
    kpe_rot = jnp.swapaxes(_rot_cols(jnp.swapaxes(kpe_rows, 1, 2)), 1, 2)
    w_tail = jnp.concatenate([tail_src[:, Q_LORA:Q_LORA + KV_LORA, :], tail_src[:, :Q_LORA, :], kpe_rows, kpe_rot],
                             axis=1)
    uq = w_uq.reshape(L, Q_LORA, HEADS, HEAD_W + ROPE)
    pad_rows = lambda a: jnp.pad(a, ((0, 0), (0, 256 - Q_LORA), (0, 0)))
    wq = pad_rows(jnp.pad(uq, ((0, 0), (0, 0), (0, 0), (0, LANES - HEAD_W - ROPE))).reshape(L, Q_LORA, 512))
    uq_rot = jnp.pad(_rot_cols(uq[..., HEAD_W:]), ((0, 0), (0, 0), (0, 0), (HEAD_W, LANES - HEAD_W - ROPE)))
    wqr = pad_rows(uq_rot.reshape(L, Q_LORA, 512))
    ukv = w_ukv.reshape(L, KV_LORA, HEADS, 2 * HEAD_W)
    wk = jnp.pad(ukv[..., :HEAD_W], ((0, 0), (0, 0), (0, 0), (0, LANES - HEAD_W))).reshape(L, KV_LORA, 512)
    wv = ukv[..., HEAD_W:].reshape(L, KV_LORA, 256)
    intra, qd, kd, cd = _retention_tables(ret_decay_f, ret_decay_b)
    r_pad = jnp.pad(router, ((0, 0), (0, 0), (0, LANES - N_EXPERTS)))
    r_hi = r_pad.astype(BF16)
    r_lo = (r_pad - r_hi.astype(F32)).astype(BF16)
    return dict(
        norm1_g=row(norm1_g), w_tail=w_tail, kv_g=row(kv_norm_g),
        q_g=jnp.pad(row(q_norm_g), ((0, 0), (0, 0), (0, 256 - Q_LORA))),
        wq=wq.astype(BF16), wqr=wqr.astype(BF16), wk=wk.astype(BF16), wv=wv.astype(BF16),
        sg_g=row(sg_norm_g), sg_w=sg_w.astype(BF16), sg_bm=jnp.repeat(jnp.swapaxes(sg_b, 1, 2), HEAD_W, axis=2),
        conv_w=jnp.pad(conv_w, ((0, 0), (0, 1), (0, 0))), conv_b=row(conv_b), conv_ln_g=row(conv_ln_g),
        conv_ln_b=row(conv_ln_b), conv_pw=conv_pw.astype(BF16),
        ret_intra=intra, ret_qd=qd, ret_kdt=kd, ret_cd=cd, ret_gn_g=row(ret_gn_g),
        norm2_g=row(norm2_g), r_hilo=jnp.concatenate([r_hi, r_lo], axis=2),
    )


def _block_diag_states(state):
    eye = jnp.eye(HEADS, dtype=state.dtype)
    bd = state[:, :, :, :, None, :] * eye[None, None, :, None, :, None]
    return bd.reshape(state.shape[0], 2, HEADS * HEAD_W, HEADS * HEAD_W)


def _mixers(l, x, mod_p, n, W, w_in, w_out, rope_tabs, s0_bd, cache, results, pending=None):
    outs = _inproj_call(l, x, mod_p, n, W, w_in, rope_tabs, None if results is None else results[:2], pending)
    if pending is not None:
        x = outs[-1]
    ya, cv, rqkv, rg, qt, k, vt = outs[:7]
    yb = _conv_call(l, cv, n, W)
    ret_out = _ret_call(l, rqkv, rg, n, W, s0_bd, None if results is None else results[2])
    yd = _attn_call(l, qt, k, vt, cache, n)
    x1, h2, aff = _outproj_call(l, (ya, yb, ret_out[0], yd), x, mod_p, n, W, w_out)
    slot, xe, gs = _route_call(aff, h2, n)
    new_results = None if results is None else (outs[7], outs[8], ret_out[1])
    return x1, slot, xe, gs, new_results


def kernel(x_prompt, x_sample, cache_mla_ckv, cache_mla_kpe, state_ret, c, c_ctx, w_mod, b_mod, norm1_g, w_in, sg_norm_g, sg_w, sg_b, conv_w, conv_b, conv_ln_g, conv_ln_b, conv_pw, ret_decay_f, ret_decay_b, ret_gn_g, q_norm_g, w_uq, kv_norm_g, w_ukv, w_out, norm2_g, router, w_gate, w_up, w_down, final_norm_g):
    bc, nc_tok, _ = x_prompt.shape
    bl, nl_tok, _ = x_sample.shape
    w_in = jnp.swapaxes(w_in, 1, 2)
    W = _prepare(norm1_g, w_in, sg_norm_g, sg_w, sg_b, conv_w, conv_b, conv_ln_g, conv_ln_b, conv_pw, ret_decay_f,
                 ret_decay_b, ret_gn_g, q_norm_g, w_uq, kv_norm_g, w_ukv, norm2_g, router)
    cvec = jnp.zeros((8, D), F32).at[0].set(c_ctx).at[1:1 + bl].set(c)
    mod_all = _mod_call(cvec, w_mod, b_mod)
    rope_tabs = _rope_tables(nl_tok)
    kpe128 = jnp.pad(cache_mla_kpe, ((0, 0), (0, 0), (0, 0), (HEAD_W, LANES - HEAD_W - ROPE)))
    cache = _kvcache_call(cache_mla_ckv, kpe128, W["wk"], W["wv"])
    s0_all = _block_diag_states(jnp.swapaxes(state_ret, 0, 1).reshape(DEPTH * bl, 2, HEADS, HEAD_W, HEAD_W))
    s0_all = s0_all.reshape(DEPTH, bl, 2, 256, 256)
    fg = final_norm_g.reshape(1, D)
    xp = x_prompt.reshape(bc * nc_tok, D)
    xs = x_sample.reshape(bl * nl_tok, D)
    results = (jnp.zeros((bc, DEPTH, nc_tok, KV_LORA), F32), jnp.zeros((bc, DEPTH, nc_tok, ROPE), F32),
               jnp.zeros((bc, DEPTH, 2, HEADS * HEAD_W, HEAD_W), F32))
    yp = ys = None
    pend_c = pend_l = None
    for l in range(DEPTH):
        mod_c = mod_all[l, 0:1].reshape(1, 1, 6 * D)
        mod_l = mod_all[l, 1:1 + bl].reshape(bl, 1, 6 * D)
        x1c, slot_c, xe_c, gs_c, results = _mixers(l, xp, mod_c, nc_tok, W, w_in, w_out, None, None, None, results,
                                                   pend_c)
        x1l, slot_l, xe_l, gs_l, _ = _mixers(l, xs, mod_l, nl_tok, W, w_in, w_out, rope_tabs, s0_all[l], cache, None,
                                             pend_l)
        ye_c, ye_l = _ffn_call(l, xe_c, xe_l, gs_c, gs_l, w_gate, w_up, w_down)
        if l == DEPTH - 1:
            yp = _combine_call(ye_c, slot_c, x1c, mod_c, nc_tok, fg)[1]
            ys = _combine_call(ye_l, slot_l, x1l, mod_l, nl_tok, fg)[1]
        else:
            xp = xs = None
            pend_c, pend_l = (ye_c, slot_c, x1c, mod_c), (ye_l, slot_l, x1l, mod_l)
    ckv_all, kpe_all, ret_all = results
    return (yp.reshape(bc, nc_tok, D), ys.reshape(bl, nl_tok, D), ckv_all, kpe_all,
            ret_all.reshape(bc, DEPTH, 2, HEADS, HEAD_W, HEAD_W))
```
